```python
import functools
import jax
import jax.numpy as jnp
from jax import lax
import numpy as np

D_MODEL = 4096
BATCH = 4
SEQ = 2048
DEPTH = 1
DEC_BATCH = 128
DEC_SEQ = 1
PAST_LEN = 8192
PAGE_SIZE = 128

HEAD_DIM = 64
ATTN_WIDTH = D_MODEL // 2
N_Q_HEADS = ATTN_WIDTH // HEAD_DIM
N_KV_HEADS = N_Q_HEADS // 8
Q_PER_KV = N_Q_HEADS // N_KV_HEADS
KV_WIDTH = N_KV_HEADS * HEAD_DIM
WINDOW = 128
GM_WIDTH = D_MODEL // 2
GM_GROUPS = 8
GM_GROUP_DIM = GM_WIDTH // GM_GROUPS
CHUNK = 128
IN_WIDTH = ATTN_WIDTH + 2 * KV_WIDTH + 2 * GM_WIDTH + 2 * D_MODEL
N_EXPERTS = 64
TOP_K = 6
N_EXPERT_GROUPS = 8
TOPK_GROUPS = 4
D_EXPERT = D_MODEL // 4
D_SHARED = D_MODEL // 4
ROUTED_SCALE = 2.5
MOE_BLOCK = 128
EPS = 1e-6

kernel_name = 'hybrid_swa_gmlp_moe_adaln_step'


def rms_norm(x, g):
    xf = x.astype(jnp.float32)
    y = xf * lax.rsqrt(jnp.mean(xf * xf, axis=-1, keepdims=True) + EPS)
    return (y * g.astype(jnp.float32)).astype(x.dtype)


def layer_norm(x, g, b):
    xf = x.astype(jnp.float32)
    mu = jnp.mean(xf, axis=-1, keepdims=True)
    var = jnp.mean(jnp.square(xf - mu), axis=-1, keepdims=True)
    y = (xf - mu) * lax.rsqrt(var + EPS) * g.astype(jnp.float32) + b.astype(jnp.float32)
    return y.astype(x.dtype)


def adaln(c, w_ada, b_ada):
    m = jax.nn.silu(c) @ w_ada + b_ada
    return jnp.split(m[:, None, :], 6, axis=-1)


def modulate(x, g, shift, scale):
    return rms_norm(x, g) * (1 + scale) + shift


def sink_softmax(logits, sinks):
    s = jnp.broadcast_to(sinks[:, :, None, None].astype(jnp.float32), logits.shape[:-1] + (1,))
    p = jax.nn.softmax(jnp.concatenate([logits, s], axis=-1), axis=-1)
    return p[..., :-1]


def window_attn_prompt(q, k, v, sinks):
    b, s = q.shape[:2]
    nb = s // WINDOW
    qb = q.reshape(b, nb, WINDOW, N_KV_HEADS, Q_PER_KV, HEAD_DIM)
    kb = k.reshape(b, nb, WINDOW, N_KV_HEADS, HEAD_DIM)
    vb = v.reshape(b, nb, WINDOW, N_KV_HEADS, HEAD_DIM)
    zk = jnp.zeros_like(kb[:, :1])
    k_band = jnp.concatenate([jnp.concatenate([zk, kb[:, :-1]], axis=1), kb], axis=2)
    v_band = jnp.concatenate([jnp.concatenate([zk, vb[:, :-1]], axis=1), vb], axis=2)
    logits = jnp.einsum('bnqkgd,bnskd->bnkgqs', qb, k_band,
                        preferred_element_type=jnp.float32) * (HEAD_DIM ** -0.5)
    qi = jnp.arange(WINDOW)[:, None]
    sj = jnp.arange(2 * WINDOW)[None, :]
    diff = WINDOW + qi - sj
    abs_s = (jnp.arange(nb)[:, None, None] - 1) * WINDOW + sj
    mask = (diff >= 0) & (diff <= WINDOW) & (abs_s >= 0)
    logits = jnp.where(mask[None, :, None, None], logits, -jnp.inf)
    p = sink_softmax(logits, sinks)
    o = jnp.einsum('bnkgqs,bnskd->bnqkgd', p.astype(v.dtype), v_band)
    return o.reshape(b, s, ATTN_WIDTH)


def attend_prompt(q, k, v, sinks):
    o = window_attn_prompt(q, k, v, sinks)
    w = min(WINDOW, k.shape[1])
    return o, k[:, -w:], v[:, -w:]


def attend_sample(q, k, v, sinks, k_cache, v_cache):
    w = k_cache.shape[1]
    t = q.shape[1]
    k_all = jnp.concatenate([k_cache, k.astype(k_cache.dtype)], axis=1)
    v_all = jnp.concatenate([v_cache, v.astype(v_cache.dtype)], axis=1)
    key_rel = jnp.concatenate([jnp.arange(-w, 0), jnp.arange(t)])
    diff = jnp.arange(t)[:, None] - key_rel[None, :]
    mask = (diff >= 0) & (diff <= WINDOW)
    logits = jnp.einsum('btkgd,bskd->bkgts', q, k_all,
                        preferred_element_type=jnp.float32) * (HEAD_DIM ** -0.5)
    logits = jnp.where(mask[None, None, None], logits, -jnp.inf)
    p = sink_softmax(logits, sinks)
    o = jnp.einsum('bkgts,bskd->btkgd', p.astype(v_all.dtype), v_all)
    return o.reshape(q.shape[0], t, ATTN_WIDTH), k_all[:, -w:], v_all[:, -w:]


def chunk_mix(vn, w_s, b_s):
    b, t, _ = vn.shape
    nc = -(-t // CHUNK)
    vp = jnp.pad(vn, ((0, 0), (0, nc * CHUNK - t), (0, 0)))
    vp = vp.reshape(b, nc, CHUNK, GM_GROUPS, GM_GROUP_DIM)
    w_causal = jnp.where(jnp.tril(jnp.ones((CHUNK, CHUNK), dtype=bool)), w_s, 0)
    mixed = jnp.einsum('gts,bnsgc->bntgc', w_causal, vp) + b_s.T[None, None, :, :, None]
    return mixed.reshape(b, nc * CHUNK, GM_WIDTH)[:, :t]


def swiglu(x, w_g, w_u, w_d):
    return (jax.nn.silu(x @ w_g) * (x @ w_u)) @ w_d


def routed_experts(ht, eidx, ew, w_gate, w_up, w_down):
    n_tok, d = ht.shape
    n_assign = n_tok * TOP_K
    flat_e = eidx.reshape(-1).astype(jnp.int32)
    flat_tok = jnp.arange(n_assign, dtype=jnp.int32) // TOP_K
    order = jnp.argsort(flat_e)
    se = flat_e[order]
    stok = flat_tok[order]
    sw = ew.reshape(-1)[order]
    counts = jnp.bincount(flat_e, length=N_EXPERTS)
    padded = (counts + MOE_BLOCK - 1) // MOE_BLOCK * MOE_BLOCK
    pad_end = jnp.cumsum(padded)
    pad_start = pad_end - padded
    sort_start = jnp.cumsum(counts) - counts
    dest = pad_start[se] + jnp.arange(n_assign, dtype=jnp.int32) - sort_start[se]
    n_blocks = -(-n_assign // MOE_BLOCK) + N_EXPERTS
    x_pad = jnp.zeros((n_blocks * MOE_BLOCK, d), ht.dtype).at[dest].set(ht[stok])
    block_e = jnp.minimum(
        jnp.searchsorted(pad_end, jnp.arange(n_blocks) * MOE_BLOCK, side='right'), N_EXPERTS - 1)

    def expert_block(args):
        xb, e = args
        return swiglu(xb, w_gate[e], w_up[e], w_down[e])

    y_pad = lax.map(expert_block, (x_pad.reshape(n_blocks, MOE_BLOCK, d), block_e)).reshape(-1, d)
    return jax.ops.segment_sum(y_pad[dest] * sw[:, None].astype(ht.dtype), stok, num_segments=n_tok)


def moe_ffn(h, w_router, router_bias, w_gate, w_up, w_down, ws_gate, ws_up, ws_down):
    b, t, d = h.shape
    ht = h.reshape(-1, d)
    n = ht.shape[0]
    scores = jax.nn.sigmoid((ht @ w_router).astype(jnp.float32))
    biased = scores + router_bias.astype(jnp.float32)
    per_group = N_EXPERTS // N_EXPERT_GROUPS
    grp_score = lax.top_k(biased.reshape(n, N_EXPERT_GROUPS, per_group), 2)[0].sum(-1)
    _, gidx = lax.top_k(grp_score, TOPK_GROUPS)
    gmask = jnp.sum(jax.nn.one_hot(gidx, N_EXPERT_GROUPS, dtype=jnp.float32), axis=-2) > 0
    emask = jnp.repeat(gmask, per_group, axis=-1)
    _, eidx = lax.top_k(jnp.where(emask, biased, -jnp.inf), TOP_K)
    ew = jnp.take_along_axis(scores, eidx, axis=-1)
    ew = ew / jnp.sum(ew, axis=-1, keepdims=True) * ROUTED_SCALE
    y = routed_experts(ht, eidx, ew, w_gate, w_up, w_down) + swiglu(ht, ws_gate, ws_up, ws_down)
    return y.reshape(b, t, d)


def decoder_layer(x, c, attend, norm_mix_g, norm_ffn_g, w_ada, b_ada, w_in, q_norm_g, k_norm_g,
                  attn_sinks, gm_ln_g, gm_ln_b, w_spatial, b_spatial, w_branch_attn,
                  w_branch_gmlp, w_out, w_router, router_bias, w_gate, w_up, w_down,
                  ws_gate, ws_up, ws_down):
    b, t, _ = x.shape
    sh_m, sc_m, g_m, sh_f, sc_f, g_f = adaln(c, w_ada, b_ada)
    h = modulate(x, norm_mix_g, sh_m, sc_m)
    z = h @ w_in
    o1 = ATTN_WIDTH
    o2 = o1 + KV_WIDTH
    o3 = o2 + KV_WIDTH
    o4 = o3 + GM_WIDTH
    o5 = o4 + GM_WIDTH
    o6 = o5 + D_MODEL
    q = rms_norm(z[..., :o1].reshape(b, t, N_KV_HEADS, Q_PER_KV, HEAD_DIM), q_norm_g)
    k = rms_norm(z[..., o1:o2].reshape(b, t, N_KV_HEADS, HEAD_DIM), k_norm_g)
    v = z[..., o2:o3].reshape(b, t, N_KV_HEADS, HEAD_DIM)
    u = jax.nn.gelu(z[..., o3:o4], approximate=False)
    vn = layer_norm(jax.nn.gelu(z[..., o4:o5], approximate=False), gm_ln_g, gm_ln_b)
    gate_attn = jax.nn.sigmoid(z[..., o5:o6])
    gate_gmlp = jax.nn.sigmoid(z[..., o6:])
    o_attn, k_state, v_state = attend(q, k, v, attn_sinks)
    o_gmlp = u * chunk_mix(vn, w_spatial, b_spatial)
    merged = gate_attn * (o_attn @ w_branch_attn) + gate_gmlp * (o_gmlp @ w_branch_gmlp)
    x = x + g_m * (merged @ w_out)
    h2 = modulate(x, norm_ffn_g, sh_f, sc_f)
    x = x + g_f * moe_ffn(h2, w_router, router_bias, w_gate, w_up, w_down, ws_gate, ws_up, ws_down)
    return x, k_state, v_state, vn


def setup_inputs(seed: int = 0) -> dict:
    key = jax.random.key(seed)
    ks = jax.random.split(key, 32)
    f32 = jnp.float32
    L = DEPTH
    D = D_MODEL
    win = min(WINDOW, PAST_LEN)

    def nrm(k, shape, scale):
        return jax.random.normal(k, shape, f32) * scale

    return {
        'x_prompt': nrm(ks[0], (BATCH, SEQ, D), 1.0),
        'x_sample': nrm(ks[1], (DEC_BATCH, DEC_SEQ, D), 1.0),
        'cache_k_win': nrm(ks[2], (L, DEC_BATCH, win, N_KV_HEADS, HEAD_DIM), 1.0),
        'cache_v_win': nrm(ks[3], (L, DEC_BATCH, win, N_KV_HEADS, HEAD_DIM), 1.0),
        'c_prompt': nrm(ks[4], (BATCH, D), 1.0),
        'c_sample': nrm(ks[5], (DEC_BATCH, D), 1.0),
        'norm_mix_g': 1.0 + nrm(ks[6], (L, D), 0.02),
        'norm_ffn_g': 1.0 + nrm(ks[7], (L, D), 0.02),
        'w_ada': nrm(ks[8], (L, D, 6 * D), 0.5 * D ** -0.5),
        'b_ada': nrm(ks[9], (L, 6 * D), 0.01),
        'w_in': nrm(ks[10], (L, D, IN_WIDTH), D ** -0.5),
        'q_norm_g': 1.0 + nrm(ks[11], (L, HEAD_DIM), 0.02),
        'k_norm_g': 1.0 + nrm(ks[12], (L, HEAD_DIM), 0.02),
        'attn_sinks': nrm(ks[13], (L, N_KV_HEADS, Q_PER_KV), 0.5),
        'gm_ln_g': 1.0 + nrm(ks[14], (L, GM_WIDTH), 0.02),
        'gm_ln_b': nrm(ks[15], (L, GM_WIDTH), 0.01),
        'w_spatial': nrm(ks[16], (L, GM_GROUPS, CHUNK, CHUNK), CHUNK ** -0.5),
        'b_spatial': 1.0 + nrm(ks[17], (L, GM_GROUPS, CHUNK), 0.01),
        'w_branch_attn': nrm(ks[18], (L, ATTN_WIDTH, D), ATTN_WIDTH ** -0.5),
        'w_branch_gmlp': nrm(ks[19], (L, GM_WIDTH, D), GM_WIDTH ** -0.5),
        'w_out': nrm(ks[20], (L, D, D), D ** -0.5),
        'w_router': nrm(ks[21], (L, D, N_EXPERTS), D ** -0.5),
        'router_bias': nrm(ks[22], (L, N_EXPERTS), 0.01),
        'w_gate': nrm(ks[23], (L, N_EXPERTS, D, D_EXPERT), D ** -0.5),
        'w_up': nrm(ks[24], (L, N_EXPERTS, D, D_EXPERT), D ** -0.5),
        'w_down': nrm(ks[25], (L, N_EXPERTS, D_EXPERT, D), D_EXPERT ** -0.5),
        'ws_gate': nrm(ks[26], (L, D, D_SHARED), D ** -0.5),
        'ws_up': nrm(ks[27], (L, D, D_SHARED), D ** -0.5),
        'ws_down': nrm(ks[28], (L, D_SHARED, D), D_SHARED ** -0.5),
    }


def reference(x_prompt, x_sample, cache_k_win, cache_v_win, c_prompt, c_sample, norm_mix_g,
              norm_ffn_g, w_ada, b_ada, w_in, q_norm_g, k_norm_g, attn_sinks, gm_ln_g, gm_ln_b,
              w_spatial, b_spatial, w_branch_attn, w_branch_gmlp, w_out, w_router, router_bias,
              w_gate, w_up, w_down, ws_gate, ws_up, ws_down):
    xp = x_prompt
    xs = x_sample
    kp_rows, vp_rows, ks_rows, vs_rows, gs_rows = [], [], [], [], []
    for l in range(DEPTH):
        weights = (norm_mix_g[l], norm_ffn_g[l], w_ada[l], b_ada[l], w_in[l], q_norm_g[l],
                   k_norm_g[l], attn_sinks[l], gm_ln_g[l], gm_ln_b[l], w_spatial[l], b_spatial[l],
                   w_branch_attn[l], w_branch_gmlp[l], w_out[l], w_router[l], router_bias[l],
                   w_gate[l], w_up[l], w_down[l], ws_gate[l], ws_up[l], ws_down[l])
        xp, k_p, v_p, _ = decoder_layer(xp, c_prompt, attend_prompt, *weights)
        attend_s = functools.partial(attend_sample, k_cache=cache_k_win[l], v_cache=cache_v_win[l])
        xs, k_s, v_s, g_s = decoder_layer(xs, c_sample, attend_s, *weights)
        kp_rows.append(k_p)
        vp_rows.append(v_p)
        ks_rows.append(k_s)
        vs_rows.append(v_s)
        gs_rows.append(g_s)
    new_k_prompt = jnp.stack(kp_rows)
    new_v_prompt = jnp.stack(vp_rows)
    new_k_sample = jnp.stack(ks_rows)
    new_v_sample = jnp.stack(vs_rows)
    new_gm_v_sample = jnp.stack(gs_rows)
    return (xp, xs, new_k_prompt, new_v_prompt, new_k_sample, new_v_sample, new_gm_v_sample)
```

```python
import functools

import jax
import jax.numpy as jnp
from jax import lax
from jax.experimental import pallas as pl
from jax.experimental.pallas import tpu as pltpu

TOP_K = 6
N_EXPERT_GROUPS = 8
TOPK_GROUPS = 4
ROUTED_SCALE = 2.5
WINDOW = 128
EPS = 1e-6

V7X_VMEM_BYTES = 64 * 1024 * 1024
V7X_VMEM_REQUEST_CAP = 56 * 1024 * 1024
LANES = 128
SUBLANES = 8

BF16 = jnp.bfloat16
F32 = jnp.float32


def _pick_tile(total, preferred, multiple):
    t = min(preferred, total)
    t -= t % multiple
    while t > multiple and total % t:
        t -= multiple
    assert t > 0 and total % t == 0, (total, preferred, multiple)
    return t


def _params(semantics, vmem_bytes):
    return pltpu.CompilerParams(
        dimension_semantics=semantics,
        vmem_limit_bytes=int(min(V7X_VMEM_REQUEST_CAP, max(vmem_bytes, 16 * 1024 * 1024))))


def _nbytes(shape, dtype):
    n = 1
    for s in shape:
        n *= s
    return n * jnp.dtype(dtype).itemsize


def _mod_spec(mod3, tm, group_rows, width, row_of, col_of):
    if mod3.shape[1] == 1:
        return pl.BlockSpec((1, 1, width), lambda *g: ((row_of(*g) * tm) // group_rows, 0, col_of(*g)))
    return pl.BlockSpec((1, tm, width), lambda *g: (0, row_of(*g), col_of(*g)))


def _adaln_kernel(c_ref, w_ref, b_ref, o_ref, a_scr):
    @pl.when(pl.program_id(0) == 0)
    def _():
        c = c_ref[...]
        a_scr[...] = (c * jax.nn.sigmoid(c)).astype(BF16)

    acc = jnp.dot(a_scr[...], w_ref[...].astype(BF16), preferred_element_type=F32)
    o_ref[...] = acc + b_ref[...]


def _adaln(c_all, w_ada, b_ada):
    rows, d = c_all.shape
    n_out = w_ada.shape[1]
    tn = _pick_tile(n_out, 512, LANES)
    est = 2 * (_nbytes((d, tn), F32) + _nbytes((rows, tn), F32)) + _nbytes((rows, d), F32) * 2 \
        + _nbytes((rows, d), BF16) + _nbytes((d, tn), BF16) + _nbytes((rows, tn), F32)
    return pl.pallas_call(
        _adaln_kernel,
        out_shape=jax.ShapeDtypeStruct((rows, n_out), F32),
        grid=(n_out // tn,),
        in_specs=[pl.BlockSpec((rows, d), lambda n: (0, 0)),
                  pl.BlockSpec((d, tn), lambda n: (0, n)),
                  pl.BlockSpec((1, tn), lambda n: (0, n))],
        out_specs=pl.BlockSpec((rows, tn), lambda n: (0, n)),
        scratch_shapes=[pltpu.VMEM((rows, d), BF16)],
        compiler_params=_params(("arbitrary",), est),
        name="adaln",
    )(c_all, w_ada, b_ada.reshape(1, n_out))


def _rms_mod(x, g, scale, shift):
    y = x * lax.rsqrt(jnp.mean(x * x, axis=-1, keepdims=True) + EPS)
    return (y * g) * (1.0 + scale) + shift


def _modulate_kernel(x_ref, g_ref, sh_ref, sc_ref, o_ref):
    o_ref[...] = _rms_mod(x_ref[...], g_ref[...], sc_ref[0], sh_ref[0]).astype(o_ref.dtype)


def _modulate(x2, g, mod3, shift_chunk, scale_chunk, group_rows):
    m, d = x2.shape
    tm = _pick_tile(min(m, group_rows) if mod3.shape[1] == 1 else m, 256, SUBLANES)
    row_of = lambda i: i
    est = 2 * (_nbytes((tm, d), F32) * 3 + _nbytes((tm, d), BF16)) + 4 * _nbytes((tm, d), F32)
    return pl.pallas_call(
        _modulate_kernel,
        out_shape=jax.ShapeDtypeStruct((m, d), BF16),
        grid=(m // tm,),
        in_specs=[pl.BlockSpec((tm, d), lambda i: (i, 0)),
                  pl.BlockSpec((1, d), lambda i: (0, 0)),
                  _mod_spec(mod3, tm, group_rows, d, row_of, lambda i: shift_chunk),
                  _mod_spec(mod3, tm, group_rows, d, row_of, lambda i: scale_chunk)],
        out_specs=pl.BlockSpec((tm, d), lambda i: (i, 0)),
        compiler_params=_params(("arbitrary",), est),
        name="modulate",
    )(x2, g.reshape(1, d), mod3, mod3)


def _mm_kernel(*refs, n_a, n_w, n_extra, n_out, pairs, epilogue):
    a_refs = refs[:n_a]
    w_refs = refs[n_a:n_a + n_w]
    extra_refs = refs[n_a + n_w:n_a + n_w + n_extra]
    out_refs = refs[n_a + n_w + n_extra:n_a + n_w + n_extra + n_out]
    w_scr = refs[n_a + n_w + n_extra + n_out:]
    n = pl.program_id(0)
    mi = pl.program_id(1)

    @pl.when(mi == 0)
    def _():
        for w_ref, scr in zip(w_refs, w_scr):
            scr[...] = w_ref[...].astype(BF16)

    accs = [jnp.dot(a_refs[ia][...], w_scr[iw][...], preferred_element_type=F32) for ia, iw in pairs]
    outs = epilogue(accs, extra_refs, n, mi)
    for o_ref, val in zip(out_refs, outs):
        o_ref[...] = val.astype(o_ref.dtype)


def _mm(a_list, w_list, extras, epilogue, outs, *, m, tm, tn, n_tiles, pairs, name):
    in_specs, args, est = [], [], 0
    for arr, kw, cb in a_list:
        in_specs.append(pl.BlockSpec((tm, kw), lambda n, mi, cb=cb: (mi, cb)))
        args.append(arr)
        est += 2 * _nbytes((tm, kw), arr.dtype)
    scratch = []
    for arr, off in w_list:
        k = arr.shape[0]
        in_specs.append(pl.BlockSpec((k, tn), lambda n, mi, off=off: (0, n + off)))
        args.append(arr)
        scratch.append(pltpu.VMEM((k, tn), BF16))
        est += 2 * _nbytes((k, tn), F32) + _nbytes((k, tn), BF16)
    for arr, spec in extras:
        in_specs.append(spec)
        args.append(arr)
        est += 2 * _nbytes(spec.block_shape, arr.dtype)
    for sds, spec in outs:
        est += 2 * _nbytes(spec.block_shape, sds.dtype)
    est += (len(pairs) + 2) * _nbytes((tm, tn), F32)
    kern = functools.partial(_mm_kernel, n_a=len(a_list), n_w=len(w_list), n_extra=len(extras),
                             n_out=len(outs), pairs=tuple(pairs), epilogue=epilogue)
    res = pl.pallas_call(
        kern,
        out_shape=[sds for sds, _ in outs],
        grid=(n_tiles, m // tm),
        in_specs=in_specs,
        out_specs=[spec for _, spec in outs],
        scratch_shapes=scratch,
        compiler_params=_params(("arbitrary", "arbitrary"), est),
        name=name,
    )(*args)
    return res


def _gelu(x):
    return x * (lax.erf(x * (2.0 ** -0.5)) + 1.0) * 0.5


def _silu(x):
    return x * jax.nn.sigmoid(x)


def _attn_prompt_kernel(sinks_ref, q_ref, kc_ref, kp_ref, vc_ref, vp_ref, o_ref, *, n_kv, n_grp, hd):
    i = pl.program_id(1)
    w = q_ref.shape[0]
    scale = hd ** -0.5
    qi = lax.broadcasted_iota(jnp.int32, (w, w), 0)
    sj = lax.broadcasted_iota(jnp.int32, (w, w), 1)
    mask_c = sj <= qi
    mask_p = jnp.logical_and(sj >= qi, i > 0)
    dn = (((1,), (1,)), ((), ()))
    for kv in range(n_kv):
        ks = slice(kv * hd, (kv + 1) * hd)
        kc = kc_ref[:, ks].astype(BF16)
        kp = kp_ref[:, ks].astype(BF16)
        vc = vc_ref[:, ks].astype(BF16)
        vp = vp_ref[:, ks].astype(BF16)
        for g in range(n_grp):
            col = (kv * n_grp + g) * hd
            q = q_ref[:, col:col + hd].astype(BF16)
            lc = lax.dot_general(q, kc, dn, preferred_element_type=F32) * scale
            lp = lax.dot_general(q, kp, dn, preferred_element_type=F32) * scale
            lc = jnp.where(mask_c, lc, -jnp.inf)
            lp = jnp.where(mask_p, lp, -jnp.inf)
            sink = sinks_ref[kv, g]
            mx = jnp.maximum(jnp.maximum(jnp.max(lc, axis=-1, keepdims=True),
                                         jnp.max(lp, axis=-1, keepdims=True)), sink)
            pc = jnp.exp(lc - mx)
            pp = jnp.exp(lp - mx)
            den = jnp.sum(pc, axis=-1, keepdims=True) + jnp.sum(pp, axis=-1, keepdims=True) \
                + jnp.exp(sink - mx)
            o = jnp.dot(pc.astype(BF16), vc, preferred_element_type=F32) \
                + jnp.dot(pp.astype(BF16), vp, preferred_element_type=F32)
            o_ref[:, col:col + hd] = (o / den).astype(o_ref.dtype)


def _attn_prompt(qkv, sinks, batch, seq, n_kv, n_grp, hd):
    aw = n_kv * n_grp * hd
    kvw = n_kv * hd
    assert seq % WINDOW == 0 and aw % kvw == 0
    nb = seq // WINDOW
    kcol = aw // kvw
    est = 2 * (_nbytes((WINDOW, aw), F32) + 4 * _nbytes((WINDOW, kvw), F32) + _nbytes((WINDOW, aw), BF16)) \
        + 16 * _nbytes((WINDOW, WINDOW), F32)
    kern = functools.partial(_attn_prompt_kernel, n_kv=n_kv, n_grp=n_grp, hd=hd)
    return pl.pallas_call(
        kern,
        out_shape=jax.ShapeDtypeStruct((batch * seq, aw), BF16),
        grid=(batch, nb),
        in_specs=[pl.BlockSpec(memory_space=pltpu.SMEM),
                  pl.BlockSpec((WINDOW, aw), lambda b, i: (b * nb + i, 0)),
                  pl.BlockSpec((WINDOW, kvw), lambda b, i: (b * nb + i, kcol)),
                  pl.BlockSpec((WINDOW, kvw), lambda b, i: (b * nb + jnp.maximum(i - 1, 0), kcol)),
                  pl.BlockSpec((WINDOW, kvw), lambda b, i: (b * nb + i, kcol + 1)),
                  pl.BlockSpec((WINDOW, kvw), lambda b, i: (b * nb + jnp.maximum(i - 1, 0), kcol + 1))],
        out_specs=pl.BlockSpec((WINDOW, aw), lambda b, i: (b * nb + i, 0)),
        compiler_params=_params(("arbitrary", "arbitrary"), est),
        name="attn_prompt",
    )(sinks, qkv, qkv, qkv, qkv, qkv)


def _attn_sample_kernel(q_ref, kn_ref, vn_ref, ck_ref, cv_ref, sinks_ref, o_ref, ok_ref, ov_ref,
                        *, n_kv, n_grp, hd):
    bs, w, kvw = ck_ref.shape
    nh = n_kv * n_grp
    scale = hd ** -0.5
    head_kv = lax.broadcasted_iota(jnp.int32, (nh, kvw), 0) // n_grp
    lane_kv = lax.broadcasted_iota(jnp.int32, (nh, kvw), 1) // hd
    own = head_kv == lane_kv
    row = lax.broadcasted_iota(jnp.int32, (w, kvw), 0)
    sinks = sinks_ref[...]
    dn = (((1,), (1,)), ((), ()))
    for b in range(bs):
        q = q_ref[b]
        qbd = jnp.where(own, jnp.concatenate([q] * n_kv, axis=-1), 0.0)
        ck = ck_ref[b]
        cv = cv_ref[b]
        kn = kn_ref[pl.ds(b, 1), :]
        vn = vn_ref[pl.ds(b, 1), :]
        logits = lax.dot_general(qbd.astype(BF16), ck.astype(BF16), dn,
                                 preferred_element_type=F32) * scale
        l_new = jnp.sum(qbd.astype(BF16).astype(F32) * kn.astype(BF16).astype(F32),
                        axis=-1, keepdims=True) * scale
        mx = jnp.maximum(jnp.maximum(jnp.max(logits, axis=-1, keepdims=True), l_new), sinks)
        p = jnp.exp(logits - mx)
        p_new = jnp.exp(l_new - mx)
        den = jnp.sum(p, axis=-1, keepdims=True) + p_new + jnp.exp(sinks - mx)
        r = jnp.dot(p.astype(BF16), cv.astype(BF16), preferred_element_type=F32) \
            + p_new.astype(BF16).astype(F32) * vn.astype(BF16).astype(F32)
        r = jnp.where(own, r / den, 0.0)
        o = r[:, 0:hd]
        for kv in range(1, n_kv):
            o = o + r[:, kv * hd:(kv + 1) * hd]
        o_ref[b] = o.astype(o_ref.dtype)
        ok_ref[b] = jnp.where(row == w - 1, kn, pltpu.roll(ck, shift=w - 1, axis=0))
        ov_ref[b] = jnp.where(row == w - 1, vn, pltpu.roll(cv, shift=w - 1, axis=0))


def _attn_sample(q3, qkv_s, cache_k, cache_v, sinks_col, n_kv, n_grp, hd):
    db, w, kvw = cache_k.shape
    nh = n_kv * n_grp
    aw = nh * hd
    kcol = aw // kvw
    bs = _pick_tile(db, 8, SUBLANES)
    est = 2 * (4 * _nbytes((bs, w, kvw), F32) + 2 * _nbytes((bs, kvw), F32) + 2 * _nbytes((bs, nh, LANES), F32)) \
        + 16 * _nbytes((w, kvw), F32)
    kern = functools.partial(_attn_sample_kernel, n_kv=n_kv, n_grp=n_grp, hd=hd)
    return pl.pallas_call(
        kern,
        out_shape=[jax.ShapeDtypeStruct((db, nh, hd), BF16),
                   jax.ShapeDtypeStruct((db, w, kvw), F32),
                   jax.ShapeDtypeStruct((db, w, kvw), F32)],
        grid=(db // bs,),
        in_specs=[pl.BlockSpec((bs, nh, hd), lambda i: (i, 0, 0)),
                  pl.BlockSpec((bs, kvw), lambda i: (i, kcol)),
                  pl.BlockSpec((bs, kvw), lambda i: (i, kcol + 1)),
                  pl.BlockSpec((bs, w, kvw), lambda i: (i, 0, 0)),
                  pl.BlockSpec((bs, w, kvw), lambda i: (i, 0, 0)),
                  pl.BlockSpec((nh, 1), lambda i: (0, 0))],
        out_specs=[pl.BlockSpec((bs, nh, hd), lambda i: (i, 0, 0)),
                   pl.BlockSpec((bs, w, kvw), lambda i: (i, 0, 0)),
                   pl.BlockSpec((bs, w, kvw), lambda i: (i, 0, 0))],
        compiler_params=_params(("arbitrary",), est),
        name="attn_sample",
    )(q3, qkv_s, qkv_s, cache_k, cache_v, sinks_col)


def _layer_norm(v, g, b):
    mu = jnp.mean(v, axis=-1, keepdims=True)
    c = v - mu
    var = jnp.mean(c * c, axis=-1, keepdims=True)
    return c * lax.rsqrt(var + EPS) * g + b


def _gmlp_prompt_kernel(u_ref, v_ref, g_ref, b_ref, ws_ref, bs_ref, o_ref, *, n_groups):
    ch = u_ref.shape[0]
    gd = u_ref.shape[1] // n_groups
    vn = _layer_norm(v_ref[...].astype(F32), g_ref[...], b_ref[...])
    ti = lax.broadcasted_iota(jnp.int32, (ch, ch), 0)
    si = lax.broadcasted_iota(jnp.int32, (ch, ch), 1)
    causal = si <= ti
    for g in range(n_groups):
        cs = slice(g * gd, (g + 1) * gd)
        wc = jnp.where(causal, ws_ref[g], 0.0).astype(BF16)
        mixed = jnp.dot(wc, vn[:, cs].astype(BF16), preferred_element_type=F32) + bs_ref[:, g:g + 1]
        o_ref[:, cs] = (u_ref[:, cs].astype(F32) * mixed).astype(o_ref.dtype)


def _gmlp_prompt(ugv, ln_g, ln_b, w_spatial, b_spatial_t):
    m, two_w = ugv.shape
    gmw = two_w // 2
    n_groups, ch, _ = w_spatial.shape
    assert m % ch == 0
    est = 2 * (2 * _nbytes((ch, gmw), ugv.dtype) + _nbytes((ch, gmw), BF16) + _nbytes(w_spatial.shape, F32)) \
        + 6 * _nbytes((ch, gmw), F32)
    kern = functools.partial(_gmlp_prompt_kernel, n_groups=n_groups)
    return pl.pallas_call(
        kern,
        out_shape=jax.ShapeDtypeStruct((m, gmw), BF16),
        grid=(m // ch,),
        in_specs=[pl.BlockSpec((ch, gmw), lambda i: (i, 0)),
                  pl.BlockSpec((ch, gmw), lambda i: (i, 1)),
                  pl.BlockSpec((1, gmw), lambda i: (0, 0)),
                  pl.BlockSpec((1, gmw), lambda i: (0, 0)),
                  pl.BlockSpec((n_groups, ch, ch), lambda i: (0, 0, 0)),
                  pl.BlockSpec((ch, n_groups), lambda i: (0, 0))],
        out_specs=pl.BlockSpec((ch, gmw), lambda i: (i, 0)),
        compiler_params=_params(("arbitrary",), est),
        name="gmlp_prompt",
    )(ugv, ugv, ln_g.reshape(1, gmw), ln_b.reshape(1, gmw), w_spatial, b_spatial_t)


def _gmlp_sample_kernel(u_ref, v_ref, g_ref, b_ref, w0_ref, b0_ref, o_ref, vn_ref):
    vn = _layer_norm(v_ref[...], g_ref[...], b_ref[...])
    vn_ref[...] = vn
    o_ref[...] = (u_ref[...] * (w0_ref[...] * vn + b0_ref[...])).astype(o_ref.dtype)


def _gmlp_sample(ugv, ln_g, ln_b, w00, b0):
    m, two_w = ugv.shape
    gmw = two_w // 2
    est = 2 * (3 * _nbytes((m, gmw), F32) + _nbytes((m, gmw), BF16)) + 4 * _nbytes((m, gmw), F32)
    row = pl.BlockSpec((1, gmw), lambda i: (0, 0))
    return pl.pallas_call(
        _gmlp_sample_kernel,
        out_shape=[jax.ShapeDtypeStruct((m, gmw), BF16), jax.ShapeDtypeStruct((m, gmw), F32)],
        grid=(1,),
        in_specs=[pl.BlockSpec((m, gmw), lambda i: (0, 0)), pl.BlockSpec((m, gmw), lambda i: (0, 1)),
                  row, row, row, row],
        out_specs=[pl.BlockSpec((m, gmw), lambda i: (0, 0)), pl.BlockSpec((m, gmw), lambda i: (0, 0))],
        compiler_params=_params(("arbitrary",), est),
        name="gmlp_sample",
    )(ugv, ugv, ln_g.reshape(1, gmw), ln_b.reshape(1, gmw), w00, b0)


def _pack_bf16_pair(left, right):
    lb = pltpu.bitcast(left.astype(BF16).astype(F32), jnp.uint32)
    rb = pltpu.bitcast(right.astype(BF16).astype(F32), jnp.uint32)
    return lb | (rb >> 16)


def _unpack_bf16_pair(word):
    left = pltpu.bitcast(word & jnp.uint32(0xFFFF0000), F32)
    right = pltpu.bitcast(word << 16, F32)
    return left, right


def _split_bf16(x):
    hi = x.astype(BF16)
    lo = (x - hi.astype(F32)).astype(BF16)
    return hi, lo


def _ffn_norm_kernel(x_ref, g_ref, sh_ref, sc_ref, wr_ref, h_ref, hp_ref, lg_ref):
    h = _rms_mod(x_ref[...], g_ref[...], sc_ref[0], sh_ref[0])
    half = h.shape[1] // 2
    h_ref[...] = h.astype(BF16)
    hp_ref[...] = _pack_bf16_pair(h[:, :half], h[:, half:])
    h_hi, h_lo = _split_bf16(h)
    w_hi, w_lo = _split_bf16(wr_ref[...])
    dn = (((1,), (1,)), ((), ()))
    lg_ref[...] = lax.dot_general(w_hi, h_hi, dn, preferred_element_type=F32) \
        + lax.dot_general(w_hi, h_lo, dn, preferred_element_type=F32) \
        + lax.dot_general(w_lo, h_hi, dn, preferred_element_type=F32)


def _ffn_norm(x2, g, mod3, shift_chunk, scale_chunk, group_rows, w_router_t):
    m, d = x2.shape
    e = w_router_t.shape[0]
    tm = _pick_tile(min(m, group_rows) if mod3.shape[1] == 1 else m, 256, LANES)
    row_of = lambda i: i
    est = 2 * (3 * _nbytes((tm, d), F32) + 2 * _nbytes((tm, d), BF16) + _nbytes((e, d), F32)) \
        + 6 * _nbytes((tm, d), F32)
    return pl.pallas_call(
        _ffn_norm_kernel,
        out_shape=[jax.ShapeDtypeStruct((m, d), BF16),
                   jax.ShapeDtypeStruct((m, d // 2), jnp.uint32),
                   jax.ShapeDtypeStruct((e, m), F32)],
        grid=(m // tm,),
        in_specs=[pl.BlockSpec((tm, d), lambda i: (i, 0)),
                  pl.BlockSpec((1, d), lambda i: (0, 0)),
                  _mod_spec(mod3, tm, group_rows, d, row_of, lambda i: shift_chunk),
                  _mod_spec(mod3, tm, group_rows, d, row_of, lambda i: scale_chunk),
                  pl.BlockSpec((e, d), lambda i: (0, 0))],
        out_specs=[pl.BlockSpec((tm, d), lambda i: (i, 0)),
                   pl.BlockSpec((tm, d // 2), lambda i: (i, 0)),
                   pl.BlockSpec((e, tm), lambda i: (0, i))],
        compiler_params=_params(("arbitrary",), est),
        name="ffn_norm_router",
    )(x2, g.reshape(1, d), mod3, mod3, w_router_t)


def _first_max(vals, idx):
    mx = jnp.max(vals, axis=0, keepdims=True)
    first = jnp.min(jnp.where(vals == mx, idx, jnp.int32(2 ** 30)), axis=0, keepdims=True)
    return mx, first


def _route_kernel(lg_ref, bias_ref, eidx_ref, ew_ref, rank_ref, cnt_ref, base_scr,
                  *, n_groups, topk_groups, top_k, scale):
    e, tr = lg_ref.shape
    per = e // n_groups

    @pl.when(pl.program_id(0) == 0)
    def _():
        base_scr[...] = jnp.zeros_like(base_scr)

    scores = jax.nn.sigmoid(lg_ref[...])
    biased = scores + bias_ref[...]
    eid = lax.broadcasted_iota(jnp.int32, (e, tr), 0)
    neg = jnp.float32(-jnp.inf)

    grp_rows = []
    bid = lax.broadcasted_iota(jnp.int32, (per, tr), 0)
    for g in range(n_groups):
        blk = biased[g * per:(g + 1) * per]
        m1, i1 = _first_max(blk, bid)
        m2 = jnp.max(jnp.where(bid == i1, neg, blk), axis=0, keepdims=True)
        grp_rows.append(m1 + m2)
    grp = jnp.concatenate(grp_rows, axis=0)
    gid = lax.broadcasted_iota(jnp.int32, (n_groups, tr), 0)
    gsel = jnp.zeros((n_groups, tr), F32)
    work = grp
    for _ in range(topk_groups):
        _, gi = _first_max(work, gid)
        hit = gid == gi
        gsel = jnp.where(hit, 1.0, gsel)
        work = jnp.where(hit, neg, work)
    emask = jnp.concatenate(
        [jnp.broadcast_to(gsel[g:g + 1], (per, tr)) for g in range(n_groups)], axis=0)
    masked = jnp.where(emask > 0.0, biased, neg)

    onehots, idxs, wts = [], [], []
    for _ in range(top_k):
        _, ei = _first_max(masked, eid)
        hit = eid == ei
        onehots.append(hit)
        idxs.append(ei)
        wts.append(jnp.sum(jnp.where(hit, scores, 0.0), axis=0, keepdims=True))
        masked = jnp.where(hit, neg, masked)
    wsum = wts[0]
    for wk in wts[1:]:
        wsum = wsum + wk

    chosen = onehots[0]
    for oh in onehots[1:]:
        chosen = jnp.logical_or(chosen, oh)
    chosen_f = jnp.where(chosen, 1.0, 0.0)
    si = lax.broadcasted_iota(jnp.int32, (tr, tr), 0)
    ti = lax.broadcasted_iota(jnp.int32, (tr, tr), 1)
    upper = jnp.where(si < ti, 1.0, 0.0).astype(BF16)
    prefix = jnp.dot(chosen_f.astype(BF16), upper, preferred_element_type=F32)
    pos = prefix + base_scr[:, 0:1]
    for k in range(top_k):
        eidx_ref[k:k + 1, :] = idxs[k]
        ew_ref[k:k + 1, :] = wts[k] / wsum * scale
        rank_ref[k:k + 1, :] = jnp.sum(jnp.where(onehots[k], pos, 0.0), axis=0, keepdims=True).astype(jnp.int32)
    for k in range(top_k, eidx_ref.shape[0]):
        eidx_ref[k:k + 1, :] = jnp.zeros((1, tr), jnp.int32)
        ew_ref[k:k + 1, :] = jnp.zeros((1, tr), F32)
        rank_ref[k:k + 1, :] = jnp.zeros((1, tr), jnp.int32)
    base_scr[...] = base_scr[...] + jnp.sum(chosen_f, axis=1, keepdims=True)
    cnt_ref[...] = base_scr[...].astype(jnp.int32)


def _route(logits_t, router_bias):
    e, n = logits_t.shape
    tr = _pick_tile(n, 640, LANES)
    rows = SUBLANES
    assert TOP_K <= rows and e % N_EXPERT_GROUPS == 0
    kern = functools.partial(_route_kernel, n_groups=N_EXPERT_GROUPS, topk_groups=TOPK_GROUPS,
                             top_k=TOP_K, scale=ROUTED_SCALE)
    est = 2 * (_nbytes((e, tr), F32) + 3 * _nbytes((rows, tr), F32)) + 24 * _nbytes((e, tr), F32) \
        + 3 * _nbytes((tr, tr), F32)
    out_row = pl.BlockSpec((rows, tr), lambda i: (0, i))
    return pl.pallas_call(
        kern,
        out_shape=[jax.ShapeDtypeStruct((rows, n), jnp.int32),
                   jax.ShapeDtypeStruct((rows, n), F32),
                   jax.ShapeDtypeStruct((rows, n), jnp.int32),
                   jax.ShapeDtypeStruct((e, LANES), jnp.int32)],
        grid=(n // tr,),
        in_specs=[pl.BlockSpec((e, tr), lambda i: (0, i)),
                  pl.BlockSpec((e, 1), lambda i: (0, 0))],
        out_specs=[out_row, out_row, out_row, pl.BlockSpec((e, LANES), lambda i: (0, 0))],
        scratch_shapes=[pltpu.VMEM((e, LANES), F32)],
        compiler_params=_params(("arbitrary",), est),
        name="route",
    )(logits_t, router_bias.reshape(e, 1))


def _row_copy(src_ref, dst_ref, sem, src_row, dst_row):
    return pltpu.make_async_copy(src_ref.at[pl.ds(src_row, 1)], dst_ref.at[pl.ds(dst_row, 1)], sem)


def _dispatch_kernel(src_tok_ref, used_ref, hp_ref, xs_ref, zero_scr, sem, *, rows_per_step):
    step = pl.program_id(0)
    base = step * rows_per_step

    @pl.when(base < used_ref[0])
    def _():
        def start(r, c):
            _row_copy(hp_ref, xs_ref, sem, src_tok_ref[base + r], base + r).start()
            return c

        def wait(r, c):
            _row_copy(hp_ref, xs_ref, sem, 0, base + r).wait()
            return c

        lax.fori_loop(0, rows_per_step, start, 0)
        lax.fori_loop(0, rows_per_step, wait, 0)

    @pl.when(base >= used_ref[0])
    def _():
        zero_scr[...] = jnp.zeros_like(zero_scr)
        fill = pltpu.make_async_copy(zero_scr, xs_ref.at[pl.ds(base, rows_per_step)], sem)
        fill.start()
        fill.wait()


def _dispatch(src_tok, used_rows, hp, n_rows, rows_per_step):
    kern = functools.partial(_dispatch_kernel, rows_per_step=rows_per_step)
    return pl.pallas_call(
        kern,
        out_shape=jax.ShapeDtypeStruct((n_rows, hp.shape[1]), hp.dtype),
        grid_spec=pltpu.PrefetchScalarGridSpec(
            num_scalar_prefetch=2,
            grid=(n_rows // rows_per_step,),
            in_specs=[pl.BlockSpec(memory_space=pl.ANY)],
            out_specs=pl.BlockSpec(memory_space=pl.ANY),
            scratch_shapes=[pltpu.VMEM((rows_per_step, hp.shape[1]), hp.dtype), pltpu.SemaphoreType.DMA]),
        compiler_params=_params(("arbitrary",), 2 * _nbytes((rows_per_step, hp.shape[1]), hp.dtype)),
        name="moe_dispatch",
    )(src_tok, used_rows, hp)


def _expert_up_kernel(tile_e_ref, used_ref, xs_ref, wg_ref, wu_ref, sw_ref, o_ref, wg_scr, wu_scr):
    t = pl.program_id(1)
    n_used = used_ref[0]

    @pl.when(t < n_used)
    def _():
        prev = tile_e_ref[jnp.maximum(t - 1, 0)]
        fresh = jnp.logical_or(t == 0, tile_e_ref[t] != prev)

        @pl.when(fresh)
        def _():
            wg_scr[...] = wg_ref[0].astype(BF16)
            wu_scr[...] = wu_ref[0].astype(BF16)

        half = xs_ref.shape[1]
        xl, xr = _unpack_bf16_pair(xs_ref[...])
        xl = xl.astype(BF16)
        xr = xr.astype(BF16)
        gate = jnp.dot(xl, wg_scr[:half], preferred_element_type=F32) \
            + jnp.dot(xr, wg_scr[half:], preferred_element_type=F32)
        up = jnp.dot(xl, wu_scr[:half], preferred_element_type=F32) \
            + jnp.dot(xr, wu_scr[half:], preferred_element_type=F32)
        o_ref[...] = (_silu(gate) * up * sw_ref[...]).astype(o_ref.dtype)

    @pl.when(t >= n_used)
    def _():
        o_ref[...] = jnp.zeros_like(o_ref)


def _expert_up(tile_e, n_used, xs, w_gate, w_up, sw, tm, fc):
    p, half = xs.shape
    e, d, de = w_gate.shape
    n_t = p // tm
    clamp = lambda t, used: jnp.minimum(t, used[0] - 1)
    est = 2 * (_nbytes((tm, half), jnp.uint32) + 2 * _nbytes((d, fc), F32) + _nbytes((tm, LANES), F32)
               + _nbytes((tm, fc), BF16)) + 2 * _nbytes((d, fc), BF16) + 2 * _nbytes((tm, d), BF16) \
        + 4 * _nbytes((tm, fc), F32) + 2 * _nbytes((tm, half), F32)
    return pl.pallas_call(
        _expert_up_kernel,
        out_shape=jax.ShapeDtypeStruct((p, de), BF16),
        grid_spec=pltpu.PrefetchScalarGridSpec(
            num_scalar_prefetch=2,
            grid=(de // fc, n_t),
            in_specs=[pl.BlockSpec((tm, half), lambda j, t, te, used: (clamp(t, used), 0)),
                      pl.BlockSpec((1, d, fc), lambda j, t, te, used: (te[clamp(t, used)], 0, j)),
                      pl.BlockSpec((1, d, fc), lambda j, t, te, used: (te[clamp(t, used)], 0, j)),
                      pl.BlockSpec((tm, 1), lambda j, t, te, used: (clamp(t, used), 0))],
            out_specs=pl.BlockSpec((tm, fc), lambda j, t, te, used: (t, j)),
            scratch_shapes=[pltpu.VMEM((d, fc), BF16), pltpu.VMEM((d, fc), BF16)]),
        compiler_params=_params(("arbitrary", "arbitrary"), est),
        name="expert_up",
    )(tile_e, n_used, xs, w_gate, w_up, sw)


def _expert_down_kernel(tile_e_ref, used_ref, a_ref, wl_ref, wr_ref, o_ref, wl_scr, wr_scr):
    t = pl.program_id(1)
    n_used = used_ref[0]

    @pl.when(t < n_used)
    def _():
        prev = tile_e_ref[jnp.maximum(t - 1, 0)]
        fresh = jnp.logical_or(t == 0, tile_e_ref[t] != prev)

        @pl.when(fresh)
        def _():
            wl_scr[...] = wl_ref[0].astype(BF16)
            wr_scr[...] = wr_ref[0].astype(BF16)

        a = a_ref[...]
        yl = jnp.dot(a, wl_scr[...], preferred_element_type=F32)
        yr = jnp.dot(a, wr_scr[...], preferred_element_type=F32)
        o_ref[...] = _pack_bf16_pair(yl, yr)

    @pl.when(t >= n_used)
    def _():
        o_ref[...] = jnp.zeros_like(o_ref)


def _expert_down(tile_e, n_used, act, w_down, tm, nc):
    p, de = act.shape
    e, _, d = w_down.shape
    half = d // 2
    n_t = p // tm
    n_c = half // nc
    clamp = lambda t, used: jnp.minimum(t, used[0] - 1)
    est = 2 * (_nbytes((tm, de), BF16) + 2 * _nbytes((de, nc), F32) + _nbytes((tm, nc), jnp.uint32)) \
        + 2 * _nbytes((de, nc), BF16) + 6 * _nbytes((tm, nc), F32)
    return pl.pallas_call(
        _expert_down_kernel,
        out_shape=jax.ShapeDtypeStruct((p, half), jnp.uint32),
        grid_spec=pltpu.PrefetchScalarGridSpec(
            num_scalar_prefetch=2,
            grid=(n_c, n_t),
            in_specs=[pl.BlockSpec((tm, de), lambda c, t, te, used: (clamp(t, used), 0)),
                      pl.BlockSpec((1, de, nc), lambda c, t, te, used: (te[clamp(t, used)], 0, c)),
                      pl.BlockSpec((1, de, nc), lambda c, t, te, used: (te[clamp(t, used)], 0, c + n_c))],
            out_specs=pl.BlockSpec((tm, nc), lambda c, t, te, used: (t, c)),
            scratch_shapes=[pltpu.VMEM((de, nc), BF16), pltpu.VMEM((de, nc), BF16)]),
        compiler_params=_params(("arbitrary", "arbitrary"), est),
        name="expert_down",
    )(tile_e, n_used, act, w_down, w_down)


def _combine_kernel(dest_ref, ys_ref, x_ref, sh_ref, gf_ref, o_ref, buf, sem, *, top_k, n_tok, tok0):
    tt = x_ref.shape[0]
    half = ys_ref.shape[1]
    base = tok0 + pl.program_id(0) * tt

    def start(r, c):
        for k in range(top_k):
            pltpu.make_async_copy(ys_ref.at[pl.ds(dest_ref[k * n_tok + base + r], 1)],
                                  buf.at[k, pl.ds(r, 1)], sem).start()
        return c

    def wait(r, c):
        for k in range(top_k):
            pltpu.make_async_copy(ys_ref.at[pl.ds(0, 1)], buf.at[k, pl.ds(r, 1)], sem).wait()
        return c

    lax.fori_loop(0, tt, start, 0)
    lax.fori_loop(0, tt, wait, 0)
    left, right = _unpack_bf16_pair(buf[0])
    for k in range(1, top_k):
        l2, r2 = _unpack_bf16_pair(buf[k])
        left = left + l2
        right = right + r2
    gf = gf_ref[0]
    o_ref[:, :half] = x_ref[:, :half] + gf[:, :half] * (left + sh_ref[:, :half])
    o_ref[:, half:] = x_ref[:, half:] + gf[:, half:] * (right + sh_ref[:, half:])


def _combine(dest_flat, ys, x2, shared, mod3, gate_chunk, group_rows, n_tok, tok0):
    m, d = x2.shape
    half = d // 2
    tt = _pick_tile(min(m, group_rows) if mod3.shape[1] == 1 else m, 64, SUBLANES)
    sh_blk0 = tok0 // tt
    assert tok0 % tt == 0
    row_of = lambda i, dest: i
    kern = functools.partial(_combine_kernel, top_k=TOP_K, n_tok=n_tok, tok0=tok0)
    est = 2 * (3 * _nbytes((tt, d), F32) + _nbytes((tt, d), F32)) + _nbytes((TOP_K, tt, half), jnp.uint32) \
        + 6 * _nbytes((tt, d), F32)
    return pl.pallas_call(
        kern,
        out_shape=jax.ShapeDtypeStruct((m, d), F32),
        grid_spec=pltpu.PrefetchScalarGridSpec(
            num_scalar_prefetch=1,
            grid=(m // tt,),
            in_specs=[pl.BlockSpec(memory_space=pl.ANY),
                      pl.BlockSpec((tt, d), lambda i, dest: (i, 0)),
                      pl.BlockSpec((tt, d), lambda i, dest: (i + sh_blk0, 0)),
                      _mod_spec(mod3, tt, group_rows, d, row_of, lambda i, dest: gate_chunk)],
            out_specs=pl.BlockSpec((tt, d), lambda i, dest: (i, 0)),
            scratch_shapes=[pltpu.VMEM((TOP_K, tt, half), jnp.uint32), pltpu.SemaphoreType.DMA]),
        compiler_params=_params(("arbitrary",), est),
        name="moe_combine",
    )(dest_flat, ys, x2, shared, mod3)


def _token_mixing(x2, mod3, group_rows, p, attend):
    m, d = x2.shape
    aw, kvw, gmw, hd = p["aw"], p["kvw"], p["gmw"], p["hd"]
    tn = 512
    assert aw % tn == 0 and (2 * kvw) % tn == 0 and gmw % tn == 0 and d % tn == 0
    tm = _pick_tile(min(m, group_rows) if mod3.shape[1] == 1 else m, 1024, 16)
    h = _modulate(x2, p["norm_mix_g"], mod3, 0, 1, group_rows)
    w_in = p["w_in"]

    qkv_w = aw + 2 * kvw

    def qkv_epilogue(accs, extras, n, mi):
        gain_ref, flag_ref, bd_ref = extras
        z = accs[0]
        sq_hi, sq_lo = _split_bf16(z * z)
        ss = jnp.dot(sq_hi, bd_ref[...], preferred_element_type=F32) \
            + jnp.dot(sq_lo, bd_ref[...], preferred_element_type=F32)
        inv = lax.rsqrt(ss * (1.0 / hd) + EPS)
        return [z * jnp.where(flag_ref[...] > 0.0, inv, 1.0) * gain_ref[...]]

    qkv = _mm([(h, d, 0)], [(w_in, 0)],
              [(p["qkv_gain"], pl.BlockSpec((1, tn), lambda n, mi: (0, n))),
               (p["qkv_flag"], pl.BlockSpec((1, tn), lambda n, mi: (0, n))),
               (p["head_ones"], pl.BlockSpec((tn, tn), lambda n, mi: (0, 0)))],
              qkv_epilogue,
              [(jax.ShapeDtypeStruct((m, qkv_w), F32), pl.BlockSpec((tm, tn), lambda n, mi: (mi, n)))],
              m=m, tm=tm, tn=tn, n_tiles=qkv_w // tn, pairs=[(0, 0)], name="in_proj_qkv")[0]

    ugv_dtype = BF16 if mod3.shape[1] == 1 else F32
    ugv = _mm([(h, d, 0)], [(w_in, qkv_w // tn)], [],
              lambda accs, extras, n, mi: [_gelu(accs[0])],
              [(jax.ShapeDtypeStruct((m, 2 * gmw), ugv_dtype), pl.BlockSpec((tm, tn), lambda n, mi: (mi, n)))],
              m=m, tm=tm, tn=tn, n_tiles=2 * gmw // tn, pairs=[(0, 0)], name="in_proj_gmlp")[0]

    gates = _mm([(h, d, 0)], [(w_in, (qkv_w + 2 * gmw) // tn)], [],
                lambda accs, extras, n, mi: [jax.nn.sigmoid(accs[0])],
                [(jax.ShapeDtypeStruct((m, 2 * d), BF16), pl.BlockSpec((tm, tn), lambda n, mi: (mi, n)))],
                m=m, tm=tm, tn=tn, n_tiles=2 * d // tn, pairs=[(0, 0)], name="in_proj_gates")[0]

    o_attn, o_gmlp, aux = attend(qkv, ugv)

    nd = d // tn
    merged = _mm([(o_attn, aw, 0), (o_gmlp, gmw, 0)], [(p["w_branch_attn"], 0), (p["w_branch_gmlp"], 0)],
                 [(gates, pl.BlockSpec((tm, tn), lambda n, mi: (mi, n))),
                  (gates, pl.BlockSpec((tm, tn), lambda n, mi: (mi, n + nd)))],
                 lambda accs, extras, n, mi: [extras[0][...].astype(F32) * accs[0]
                                              + extras[1][...].astype(F32) * accs[1]],
                 [(jax.ShapeDtypeStruct((m, d), BF16), pl.BlockSpec((tm, tn), lambda n, mi: (mi, n)))],
                 m=m, tm=tm, tn=tn, n_tiles=nd, pairs=[(0, 0), (1, 1)], name="branch_merge")[0]

    x1 = _mm([(merged, d, 0)], [(p["w_out"], 0)],
             [(x2, pl.BlockSpec((tm, tn), lambda n, mi: (mi, n))),
              (mod3, _mod_spec(mod3, tm, group_rows, tn, lambda n, mi: mi, lambda n, mi: 2 * nd + n))],
             lambda accs, extras, n, mi: [extras[0][...] + extras[1][0] * accs[0]],
             [(jax.ShapeDtypeStruct((m, d), F32), pl.BlockSpec((tm, tn), lambda n, mi: (mi, n)))],
             m=m, tm=tm, tn=tn, n_tiles=nd, pairs=[(0, 0)], name="out_proj")[0]
    return x1, qkv, aux


def kernel(x_prompt, x_sample, cache_k_win, cache_v_win, c_prompt, c_sample, norm_mix_g, norm_ffn_g,
           w_ada, b_ada, w_in, q_norm_g, k_norm_g, attn_sinks, gm_ln_g, gm_ln_b, w_spatial, b_spatial,
           w_branch_attn, w_branch_gmlp, w_out, w_router, router_bias, w_gate, w_up, w_down,
           ws_gate, ws_up, ws_down):
    depth = norm_mix_g.shape[0]
    assert depth == 1, "single-layer step"
    batch, seq, d = x_prompt.shape
    db, t_new, _ = x_sample.shape
    assert t_new == 1, "one new token per sequence"
    _, _, win, n_kv, hd = cache_k_win.shape
    n_grp = attn_sinks.shape[-1]
    aw, kvw = n_kv * n_grp * hd, n_kv * hd
    gmw = gm_ln_g.shape[-1]
    n_groups, ch, _ = w_spatial.shape[1:]
    n_exp = w_router.shape[-1]
    de = w_gate.shape[-1]
    ds = ws_gate.shape[-1]
    tn = 512
    l = 0

    n_c = batch + db
    rows = -(-n_c // 16) * 16
    c_all = jnp.concatenate([c_prompt, c_sample, jnp.zeros((rows - n_c, d), F32)], axis=0)
    mods = _adaln(c_all, w_ada[l], b_ada[l])
    mod_p = mods[:batch].reshape(batch, 1, 6 * d)
    mod_s = mods[batch:n_c].reshape(1, db, 6 * d)

    gq = jnp.tile(q_norm_g[l], aw // hd)
    gk = jnp.tile(k_norm_g[l], kvw // hd)
    qkv_gain = jnp.concatenate([gq, gk, jnp.ones((kvw,), F32)]).reshape(1, -1)
    qkv_flag = jnp.concatenate([jnp.ones((aw + kvw,), F32), jnp.zeros((kvw,), F32)]).reshape(1, -1)
    hid = jnp.arange(tn) // hd
    head_ones = (hid[:, None] == hid[None, :]).astype(BF16)
    params = dict(aw=aw, kvw=kvw, gmw=gmw, hd=hd, norm_mix_g=norm_mix_g[l], w_in=w_in[l],
                  qkv_gain=qkv_gain, qkv_flag=qkv_flag, head_ones=head_ones,
                  w_branch_attn=w_branch_attn[l], w_branch_gmlp=w_branch_gmlp[l], w_out=w_out[l])
    sinks = attn_sinks[l]

    def attend_prompt(qkv, ugv):
        o_attn = _attn_prompt(qkv, sinks, batch, seq, n_kv, n_grp, hd)
        o_gmlp = _gmlp_prompt(ugv, gm_ln_g[l], gm_ln_b[l], w_spatial[l], b_spatial[l].T)
        return o_attn, o_gmlp, None

    xp2 = x_prompt.reshape(batch * seq, d)
    x1_p, qkv_p, _ = _token_mixing(xp2, mod_p, seq, params, attend_prompt)

    ck = cache_k_win[l].reshape(db, win, kvw)
    cv = cache_v_win[l].reshape(db, win, kvw)

    def attend_sample(qkv, ugv):
        q3 = qkv[:, :aw].reshape(db, n_kv * n_grp, hd)
        o3, new_k, new_v = _attn_sample(q3, qkv, ck, cv, sinks.reshape(-1, 1), n_kv, n_grp, hd)
        w00 = jnp.repeat(w_spatial[l][:, 0, 0], gmw // n_groups).reshape(1, gmw)
        b0 = jnp.repeat(b_spatial[l][:, 0], gmw // n_groups).reshape(1, gmw)
        o_gmlp, vn = _gmlp_sample(ugv, gm_ln_g[l], gm_ln_b[l], w00, b0)
        return o3.reshape(db, aw), o_gmlp, (new_k, new_v, vn)

    xs2 = x_sample.reshape(db, d)
    x1_s, _, (new_k_s, new_v_s, vn_s) = _token_mixing(xs2, mod_s, 1, params, attend_sample)

    w_router_t = w_router[l].T
    h2_p, hp_p, lg_p = _ffn_norm(x1_p, norm_ffn_g[l], mod_p, 3, 4, seq, w_router_t)
    h2_s, hp_s, lg_s = _ffn_norm(x1_s, norm_ffn_g[l], mod_s, 3, 4, 1, w_router_t)
    h2 = jnp.concatenate([h2_p, h2_s], axis=0)
    hp = jnp.concatenate([hp_p, hp_s], axis=0)
    logits_t = jnp.concatenate([lg_p, lg_s], axis=1)
    n_tok = h2.shape[0]

    eidx8, ew8, rank8, counts = _route(logits_t, router_bias[l])
    eidx, ew, rank = eidx8[:TOP_K], ew8[:TOP_K], rank8[:TOP_K]
    counts = counts[:, 0]

    tm_e = 256
    n_assign = n_tok * TOP_K
    n_tiles = -(-n_assign // tm_e) + n_exp
    n_rows = n_tiles * tm_e
    padded = (counts + tm_e - 1) // tm_e * tm_e
    pad_end = jnp.cumsum(padded)
    pad_start = pad_end - padded
    dest = pad_start[eidx] + rank
    tok = jnp.broadcast_to(jnp.arange(n_tok, dtype=jnp.int32)[None], dest.shape)
    src_tok = jnp.zeros((n_rows,), jnp.int32).at[dest.reshape(-1)].set(tok.reshape(-1))
    sw = jnp.zeros((n_rows,), F32).at[dest.reshape(-1)].set(ew.reshape(-1)).reshape(n_rows, 1)
    tile_e = jnp.minimum(jnp.searchsorted(pad_end, jnp.arange(n_tiles, dtype=jnp.int32) * tm_e,
                                          side="right"), n_exp - 1).astype(jnp.int32)
    used_rows = pad_end[-1:].astype(jnp.int32)
    used_tiles = (used_rows // tm_e).astype(jnp.int32)

    xs = _dispatch(src_tok, used_rows, hp, n_rows, tm_e)
    act = _expert_up(tile_e, used_tiles, xs, w_gate[l], w_up[l], sw, tm_e, _pick_tile(de, 512, LANES))
    ys = _expert_down(tile_e, used_tiles, act, w_down[l], tm_e, _pick_tile(d // 2, 512, LANES))

    tm_s = _pick_tile(n_tok, 640, 16)
    tn_s = _pick_tile(ds, 256, LANES)
    sh_act = _mm([(h2, d, 0)], [(ws_gate[l], 0), (ws_up[l], 0)], [],
                 lambda accs, extras, n, mi: [_silu(accs[0]) * accs[1]],
                 [(jax.ShapeDtypeStruct((n_tok, ds), BF16), pl.BlockSpec((tm_s, tn_s), lambda n, mi: (mi, n)))],
                 m=n_tok, tm=tm_s, tn=tn_s, n_tiles=ds // tn_s, pairs=[(0, 0), (0, 1)], name="shared_up")[0]
    shared = _mm([(sh_act, ds, 0)], [(ws_down[l], 0)], [],
                 lambda accs, extras, n, mi: [accs[0]],
                 [(jax.ShapeDtypeStruct((n_tok, d), F32), pl.BlockSpec((tm_s, tn), lambda n, mi: (mi, n)))],
                 m=n_tok, tm=tm_s, tn=tn, n_tiles=d // tn, pairs=[(0, 0)], name="shared_down")[0]

    dest_flat = dest.reshape(-1)
    y_p = _combine(dest_flat, ys, x1_p, shared, mod_p, 5, seq, n_tok, 0)
    y_s = _combine(dest_flat, ys, x1_s, shared, mod_s, 5, 1, n_tok, batch * seq)

    w_keep = min(WINDOW, seq)
    qkv_p3 = qkv_p.reshape(batch, seq, aw + 2 * kvw)
    new_k_p = qkv_p3[:, seq - w_keep:, aw:aw + kvw].reshape(1, batch, w_keep, n_kv, hd)
    new_v_p = qkv_p3[:, seq - w_keep:, aw + kvw:].reshape(1, batch, w_keep, n_kv, hd)
    return (y_p.reshape(batch, seq, d), y_s.reshape(db, 1, d), new_k_p, new_v_p,
            new_k_s.reshape(1, db, win, n_kv, hd), new_v_s.reshape(1, db, win, n_kv, hd),
            vn_s.reshape(1, db, 1, gmw))
```

```python
import functools

import jax
import jax.numpy as jnp
from jax import lax
from jax.experimental import pallas as pl
from jax.experimental.pallas import tpu as pltpu

TOP_K = 6
N_EXPERT_GROUPS = 8
TOPK_GROUPS = 4
ROUTED_SCALE = 2.5
WINDOW = 128
EPS = 1e-6

V7X_VMEM_BYTES = 64 * 1024 * 1024
V7X_VMEM_REQUEST_CAP = 56 * 1024 * 1024
LANES = 128
SUBLANES = 8

BF16 = jnp.bfloat16
F32 = jnp.float32


def _pick_tile(total, preferred, multiple):
    t = min(preferred, total)
    t -= t % multiple
    while t > multiple and total % t:
        t -= multiple
    assert t > 0 and total % t == 0, (total, preferred, multiple)
    return t


def _params(semantics, vmem_bytes):
    return pltpu.CompilerParams(
        dimension_semantics=semantics,
        vmem_limit_bytes=int(min(V7X_VMEM_REQUEST_CAP, max(vmem_bytes, 16 * 1024 * 1024))))


def _nbytes(shape, dtype):
    n = 1
    for s in shape:
        n *= s
    return n * jnp.dtype(dtype).itemsize


def _mod_spec(mod3, tm, group_rows, width, row_of, col_of):
    if mod3.shape[1] == 1:
        return pl.BlockSpec((1, 1, width), lambda *g: ((row_of(*g) * tm) // group_rows, 0, col_of(*g)))
    return pl.BlockSpec((1, tm, width), lambda *g: (0, row_of(*g), col_of(*g)))


def _adaln_kernel(c_ref, w_ref, b_ref, o_ref, a_scr):
    @pl.when(pl.program_id(0) == 0)
    def _():
        c = c_ref[...]
        a_scr[...] = (c * jax.nn.sigmoid(c)).astype(BF16)

    acc = jnp.dot(a_scr[...], w_ref[...].astype(BF16), preferred_element_type=F32)
    o_ref[...] = acc + b_ref[...]


def _adaln(c_all, w_ada, b_ada):
    rows, d = c_all.shape
    n_out = w_ada.shape[1]
    tn = _pick_tile(n_out, 512, LANES)
    est = 2 * (_nbytes((d, tn), F32) + _nbytes((rows, tn), F32)) + _nbytes((rows, d), F32) * 2 \
        + _nbytes((rows, d), BF16) + _nbytes((d, tn), BF16) + _nbytes((rows, tn), F32)
    return pl.pallas_call(
        _adaln_kernel,
        out_shape=jax.ShapeDtypeStruct((rows, n_out), F32),
        grid=(n_out // tn,),
        in_specs=[pl.BlockSpec((rows, d), lambda n: (0, 0)),
                  pl.BlockSpec((d, tn), lambda n: (0, n)),
                  pl.BlockSpec((1, tn), lambda n: (0, n))],
        out_specs=pl.BlockSpec((rows, tn), lambda n: (0, n)),
        scratch_shapes=[pltpu.VMEM((rows, d), BF16)],
        compiler_params=_params(("arbitrary",), est),
        name="adaln",
    )(c_all, w_ada, b_ada.reshape(1, n_out))


def _rms_mod(x, g, scale, shift):
    y = x * lax.rsqrt(jnp.mean(x * x, axis=-1, keepdims=True) + EPS)
    return (y * g) * (1.0 + scale) + shift


def _modulate_kernel(x_ref, g_ref, sh_ref, sc_ref, o_ref):
    o_ref[...] = _rms_mod(x_ref[...], g_ref[...], sc_ref[0], sh_ref[0]).astype(o_ref.dtype)


def _modulate(x2, g, mod3, shift_chunk, scale_chunk, group_rows):
    m, d = x2.shape
    tm = _pick_tile(min(m, group_rows) if mod3.shape[1] == 1 else m, 256, SUBLANES)
    row_of = lambda i: i
    est = 2 * (_nbytes((tm, d), F32) * 3 + _nbytes((tm, d), BF16)) + 4 * _nbytes((tm, d), F32)
    return pl.pallas_call(
        _modulate_kernel,
        out_shape=jax.ShapeDtypeStruct((m, d), BF16),
        grid=(m // tm,),
        in_specs=[pl.BlockSpec((tm, d), lambda i: (i, 0)),
                  pl.BlockSpec((1, d), lambda i: (0, 0)),
                  _mod_spec(mod3, tm, group_rows, d, row_of, lambda i: shift_chunk),
                  _mod_spec(mod3, tm, group_rows, d, row_of, lambda i: scale_chunk)],
        out_specs=pl.BlockSpec((tm, d), lambda i: (i, 0)),
        compiler_params=_params(("arbitrary",), est),
        name="modulate",
    )(x2, g.reshape(1, d), mod3, mod3)


def _mm_kernel(*refs, n_a, n_w, n_extra, n_out, pairs, epilogue):
    a_refs = refs[:n_a]
    w_refs = refs[n_a:n_a + n_w]
    extra_refs = refs[n_a + n_w:n_a + n_w + n_extra]
    out_refs = refs[n_a + n_w + n_extra:n_a + n_w + n_extra + n_out]
    w_scr = refs[n_a + n_w + n_extra + n_out:]
    n = pl.program_id(0)
    mi = pl.program_id(1)

    @pl.when(mi == 0)
    def _():
        for w_ref, scr in zip(w_refs, w_scr):
            scr[...] = w_ref[...].astype(BF16)

    accs = [jnp.dot(a_refs[ia][...], w_scr[iw][...], preferred_element_type=F32) for ia, iw in pairs]
    outs = epilogue(accs, extra_refs, n, mi)
    for o_ref, val in zip(out_refs, outs):
        o_ref[...] = val.astype(o_ref.dtype)


def _mm(a_list, w_list, extras, epilogue, outs, *, m, tm, tn, n_tiles, pairs, name):
    in_specs, args, est = [], [], 0
    for arr, kw, cb in a_list:
        in_specs.append(pl.BlockSpec((tm, kw), lambda n, mi, cb=cb: (mi, cb)))
        args.append(arr)
        est += 2 * _nbytes((tm, kw), arr.dtype)
    scratch = []
    for arr, off in w_list:
        k = arr.shape[0]
        in_specs.append(pl.BlockSpec((k, tn), lambda n, mi, off=off: (0, n + off)))
        args.append(arr)
        scratch.append(pltpu.VMEM((k, tn), BF16))
        est += 2 * _nbytes((k, tn), F32) + _nbytes((k, tn), BF16)
    for arr, spec in extras:
        in_specs.append(spec)
        args.append(arr)
        est += 2 * _nbytes(spec.block_shape, arr.dtype)
    for sds, spec in outs:
        est += 2 * _nbytes(spec.block_shape, sds.dtype)
    est += (len(pairs) + 2) * _nbytes((tm, tn), F32)
    kern = functools.partial(_mm_kernel, n_a=len(a_list), n_w=len(w_list), n_extra=len(extras),
                             n_out=len(outs), pairs=tuple(pairs), epilogue=epilogue)
    res = pl.pallas_call(
        kern,
        out_shape=[sds for sds, _ in outs],
        grid=(n_tiles, m // tm),
        in_specs=in_specs,
        out_specs=[spec for _, spec in outs],
        scratch_shapes=scratch,
        compiler_params=_params(("arbitrary", "arbitrary"), est),
        name=name,
    )(*args)
    return res


def _gelu(x):
    return x * (lax.erf(x * (2.0 ** -0.5)) + 1.0) * 0.5


def _silu(x):
    return x * jax.nn.sigmoid(x)


def _attn_prompt_kernel(sinks_ref, q_ref, kc_ref, kp_ref, vc_ref, vp_ref, o_ref, *, n_kv, n_grp, hd):
    i = pl.program_id(1)
    w = q_ref.shape[0]
    scale = hd ** -0.5
    rows = n_grp * w
    qi = lax.broadcasted_iota(jnp.int32, (rows, w), 0) % w
    sj = lax.broadcasted_iota(jnp.int32, (rows, w), 1)
    mask_c = sj <= qi
    mask_p = jnp.logical_and(sj >= qi, i > 0)
    ones = jnp.ones((w, w), BF16)
    dn = (((1,), (1,)), ((), ()))
    for kv in range(n_kv):
        ks = slice(kv * hd, (kv + 1) * hd)
        kc = kc_ref[:, ks].astype(BF16)
        kp = kp_ref[:, ks].astype(BF16)
        vc = vc_ref[:, ks].astype(BF16)
        vp = vp_ref[:, ks].astype(BF16)
        cols = [(kv * n_grp + g) * hd for g in range(n_grp)]
        q = jnp.concatenate([q_ref[:, c:c + hd] for c in cols], axis=0).astype(BF16)
        sink = jnp.concatenate([jnp.full((w, 1), sinks_ref[kv, g], F32) for g in range(n_grp)], axis=0)
        lc = jnp.where(mask_c, lax.dot_general(q, kc, dn, preferred_element_type=F32) * scale, -jnp.inf)
        lp = jnp.where(mask_p, lax.dot_general(q, kp, dn, preferred_element_type=F32) * scale, -jnp.inf)
        mx = jnp.maximum(jnp.max(jnp.maximum(lc, lp), axis=-1, keepdims=True), sink)
        pc = jnp.exp(lc - mx).astype(BF16)
        pp = jnp.exp(lp - mx).astype(BF16)
        den = jnp.dot(pc, ones, preferred_element_type=F32) + jnp.dot(pp, ones, preferred_element_type=F32)
        den = den[:, :hd] + jnp.exp(sink - mx)
        o = jnp.dot(pc, vc, preferred_element_type=F32) + jnp.dot(pp, vp, preferred_element_type=F32)
        o = (o / den).astype(o_ref.dtype)
        for g, c in enumerate(cols):
            o_ref[:, c:c + hd] = o[g * w:(g + 1) * w]


def _attn_prompt(qkv, sinks, batch, seq, n_kv, n_grp, hd):
    aw = n_kv * n_grp * hd
    kvw = n_kv * hd
    assert seq % WINDOW == 0 and aw % kvw == 0
    nb = seq // WINDOW
    kcol = aw // kvw
    est = 2 * (_nbytes((WINDOW, aw), F32) + 4 * _nbytes((WINDOW, kvw), F32) + _nbytes((WINDOW, aw), BF16)) \
        + 16 * _nbytes((WINDOW, WINDOW), F32)
    kern = functools.partial(_attn_prompt_kernel, n_kv=n_kv, n_grp=n_grp, hd=hd)
    return pl.pallas_call(
        kern,
        out_shape=jax.ShapeDtypeStruct((batch * seq, aw), BF16),
        grid=(batch, nb),
        in_specs=[pl.BlockSpec(memory_space=pltpu.SMEM),
                  pl.BlockSpec((WINDOW, aw), lambda b, i: (b * nb + i, 0)),
                  pl.BlockSpec((WINDOW, kvw), lambda b, i: (b * nb + i, kcol)),
                  pl.BlockSpec((WINDOW, kvw), lambda b, i: (b * nb + jnp.maximum(i - 1, 0), kcol)),
                  pl.BlockSpec((WINDOW, kvw), lambda b, i: (b * nb + i, kcol + 1)),
                  pl.BlockSpec((WINDOW, kvw), lambda b, i: (b * nb + jnp.maximum(i - 1, 0), kcol + 1))],
        out_specs=pl.BlockSpec((WINDOW, aw), lambda b, i: (b * nb + i, 0)),
        compiler_params=_params(("arbitrary", "arbitrary"), est),
        name="attn_prompt",
    )(sinks, qkv, qkv, qkv, qkv, qkv)


def _attn_sample_kernel(q_ref, kn_ref, vn_ref, ck_ref, cv_ref, sinks_ref, o_ref, ok_ref, ov_ref,
                        *, n_kv, n_grp, hd):
    bs, w, kvw = ck_ref.shape
    nh = n_kv * n_grp
    scale = hd ** -0.5
    head_kv = lax.broadcasted_iota(jnp.int32, (nh, kvw), 0) // n_grp
    lane_kv = lax.broadcasted_iota(jnp.int32, (nh, kvw), 1) // hd
    own = head_kv == lane_kv
    row = lax.broadcasted_iota(jnp.int32, (w, kvw), 0)
    sinks = sinks_ref[...]
    dn = (((1,), (1,)), ((), ()))
    for b in range(bs):
        q = q_ref[b]
        qbd = jnp.where(own, jnp.concatenate([q] * n_kv, axis=-1), 0.0)
        ck = ck_ref[b]
        cv = cv_ref[b]
        kn = kn_ref[pl.ds(b, 1), :]
        vn = vn_ref[pl.ds(b, 1), :]
        logits = lax.dot_general(qbd.astype(BF16), ck.astype(BF16), dn,
                                 preferred_element_type=F32) * scale
        l_new = jnp.sum(qbd.astype(BF16).astype(F32) * kn.astype(BF16).astype(F32),
                        axis=-1, keepdims=True) * scale
        mx = jnp.maximum(jnp.maximum(jnp.max(logits, axis=-1, keepdims=True), l_new), sinks)
        p = jnp.exp(logits - mx)
        p_new = jnp.exp(l_new - mx)
        den = jnp.sum(p, axis=-1, keepdims=True) + p_new + jnp.exp(sinks - mx)
        r = jnp.dot(p.astype(BF16), cv.astype(BF16), preferred_element_type=F32) \
            + p_new.astype(BF16).astype(F32) * vn.astype(BF16).astype(F32)
        r = jnp.where(own, r / den, 0.0)
        o = r[:, 0:hd]
        for kv in range(1, n_kv):
            o = o + r[:, kv * hd:(kv + 1) * hd]
        o_ref[b] = o.astype(o_ref.dtype)
        ok_ref[b] = jnp.where(row == w - 1, kn, pltpu.roll(ck, shift=w - 1, axis=0))
        ov_ref[b] = jnp.where(row == w - 1, vn, pltpu.roll(cv, shift=w - 1, axis=0))


def _attn_sample(q3, qkv_s, cache_k, cache_v, sinks_col, n_kv, n_grp, hd):
    db, w, kvw = cache_k.shape
    nh = n_kv * n_grp
    aw = nh * hd
    kcol = aw // kvw
    bs = _pick_tile(db, 8, SUBLANES)
    est = 2 * (4 * _nbytes((bs, w, kvw), F32) + 2 * _nbytes((bs, kvw), F32) + 2 * _nbytes((bs, nh, LANES), F32)) \
        + 16 * _nbytes((w, kvw), F32)
    kern = functools.partial(_attn_sample_kernel, n_kv=n_kv, n_grp=n_grp, hd=hd)
    return pl.pallas_call(
        kern,
        out_shape=[jax.ShapeDtypeStruct((db, nh, hd), BF16),
                   jax.ShapeDtypeStruct((db, w, kvw), F32),
                   jax.ShapeDtypeStruct((db, w, kvw), F32)],
        grid=(db // bs,),
        in_specs=[pl.BlockSpec((bs, nh, hd), lambda i: (i, 0, 0)),
                  pl.BlockSpec((bs, kvw), lambda i: (i, kcol)),
                  pl.BlockSpec((bs, kvw), lambda i: (i, kcol + 1)),
                  pl.BlockSpec((bs, w, kvw), lambda i: (i, 0, 0)),
                  pl.BlockSpec((bs, w, kvw), lambda i: (i, 0, 0)),
                  pl.BlockSpec((nh, 1), lambda i: (0, 0))],
        out_specs=[pl.BlockSpec((bs, nh, hd), lambda i: (i, 0, 0)),
                   pl.BlockSpec((bs, w, kvw), lambda i: (i, 0, 0)),
                   pl.BlockSpec((bs, w, kvw), lambda i: (i, 0, 0))],
        compiler_params=_params(("arbitrary",), est),
        name="attn_sample",
    )(q3, qkv_s, qkv_s, cache_k, cache_v, sinks_col)


def _layer_norm(v, g, b):
    mu = jnp.mean(v, axis=-1, keepdims=True)
    c = v - mu
    var = jnp.mean(c * c, axis=-1, keepdims=True)
    return c * lax.rsqrt(var + EPS) * g + b


def _gmlp_prompt_kernel(u_ref, v_ref, g_ref, b_ref, ws_ref, bs_ref, o_ref, *, n_groups):
    ch = u_ref.shape[0]
    gd = u_ref.shape[1] // n_groups
    vn = _layer_norm(v_ref[...].astype(F32), g_ref[...], b_ref[...])
    ti = lax.broadcasted_iota(jnp.int32, (ch, ch), 0)
    si = lax.broadcasted_iota(jnp.int32, (ch, ch), 1)
    causal = si <= ti
    for g in range(n_groups):
        cs = slice(g * gd, (g + 1) * gd)
        wc = jnp.where(causal, ws_ref[g], 0.0).astype(BF16)
        mixed = jnp.dot(wc, vn[:, cs].astype(BF16), preferred_element_type=F32) + bs_ref[:, g:g + 1]
        o_ref[:, cs] = (u_ref[:, cs].astype(F32) * mixed).astype(o_ref.dtype)


def _gmlp_prompt(ugv, ln_g, ln_b, w_spatial, b_spatial_t):
    m, two_w = ugv.shape
    gmw = two_w // 2
    n_groups, ch, _ = w_spatial.shape
    assert m % ch == 0
    est = 2 * (2 * _nbytes((ch, gmw), ugv.dtype) + _nbytes((ch, gmw), BF16) + _nbytes(w_spatial.shape, F32)) \
        + 6 * _nbytes((ch, gmw), F32)
    kern = functools.partial(_gmlp_prompt_kernel, n_groups=n_groups)
    return pl.pallas_call(
        kern,
        out_shape=jax.ShapeDtypeStruct((m, gmw), BF16),
        grid=(m // ch,),
        in_specs=[pl.BlockSpec((ch, gmw), lambda i: (i, 0)),
                  pl.BlockSpec((ch, gmw), lambda i: (i, 1)),
                  pl.BlockSpec((1, gmw), lambda i: (0, 0)),
                  pl.BlockSpec((1, gmw), lambda i: (0, 0)),
                  pl.BlockSpec((n_groups, ch, ch), lambda i: (0, 0, 0)),
                  pl.BlockSpec((ch, n_groups), lambda i: (0, 0))],
        out_specs=pl.BlockSpec((ch, gmw), lambda i: (i, 0)),
        compiler_params=_params(("arbitrary",), est),
        name="gmlp_prompt",
    )(ugv, ugv, ln_g.reshape(1, gmw), ln_b.reshape(1, gmw), w_spatial, b_spatial_t)


def _gmlp_sample_kernel(u_ref, v_ref, g_ref, b_ref, w0_ref, b0_ref, o_ref, vn_ref):
    vn = _layer_norm(v_ref[...], g_ref[...], b_ref[...])
    vn_ref[...] = vn
    o_ref[...] = (u_ref[...] * (w0_ref[...] * vn + b0_ref[...])).astype(o_ref.dtype)


def _gmlp_sample(ugv, ln_g, ln_b, w00, b0):
    m, two_w = ugv.shape
    gmw = two_w // 2
    est = 2 * (3 * _nbytes((m, gmw), F32) + _nbytes((m, gmw), BF16)) + 4 * _nbytes((m, gmw), F32)
    row = pl.BlockSpec((1, gmw), lambda i: (0, 0))
    return pl.pallas_call(
        _gmlp_sample_kernel,
        out_shape=[jax.ShapeDtypeStruct((m, gmw), BF16), jax.ShapeDtypeStruct((m, gmw), F32)],
        grid=(1,),
        in_specs=[pl.BlockSpec((m, gmw), lambda i: (0, 0)), pl.BlockSpec((m, gmw), lambda i: (0, 1)),
                  row, row, row, row],
        out_specs=[pl.BlockSpec((m, gmw), lambda i: (0, 0)), pl.BlockSpec((m, gmw), lambda i: (0, 0))],
        compiler_params=_params(("arbitrary",), est),
        name="gmlp_sample",
    )(ugv, ugv, ln_g.reshape(1, gmw), ln_b.reshape(1, gmw), w00, b0)


def _pack_bf16_pair(left, right):
    lb = pltpu.bitcast(left.astype(BF16).astype(F32), jnp.uint32)
    rb = pltpu.bitcast(right.astype(BF16).astype(F32), jnp.uint32)
    return lb | (rb >> 16)


def _unpack_bf16_pair(word):
    left = pltpu.bitcast(word & jnp.uint32(0xFFFF0000), F32)
    right = pltpu.bitcast(word << 16, F32)
    return left, right


def _split_bf16(x):
    hi = x.astype(BF16)
    lo = (x - hi.astype(F32)).astype(BF16)
    return hi, lo


def _ffn_norm_kernel(x_ref, g_ref, sh_ref, sc_ref, wr_ref, h_ref, hp_ref, lg_ref):
    h = _rms_mod(x_ref[...], g_ref[...], sc_ref[0], sh_ref[0])
    half = h.shape[1] // 2
    h_ref[...] = h.astype(BF16)
    hp_ref[...] = _pack_bf16_pair(h[:, :half], h[:, half:])
    h_hi, h_lo = _split_bf16(h)
    w_hi, w_lo = _split_bf16(wr_ref[...])
    dn = (((1,), (1,)), ((), ()))
    lg_ref[...] = lax.dot_general(w_hi, h_hi, dn, preferred_element_type=F32) \
        + lax.dot_general(w_hi, h_lo, dn, preferred_element_type=F32) \
        + lax.dot_general(w_lo, h_hi, dn, preferred_element_type=F32)


def _ffn_norm(x2, g, mod3, shift_chunk, scale_chunk, group_rows, w_router_t):
    m, d = x2.shape
    e = w_router_t.shape[0]
    tm = _pick_tile(min(m, group_rows) if mod3.shape[1] == 1 else m, 256, LANES)
    row_of = lambda i: i
    est = 2 * (3 * _nbytes((tm, d), F32) + 2 * _nbytes((tm, d), BF16) + _nbytes((e, d), F32)) \
        + 6 * _nbytes((tm, d), F32)
    return pl.pallas_call(
        _ffn_norm_kernel,
        out_shape=[jax.ShapeDtypeStruct((m, d), BF16),
                   jax.ShapeDtypeStruct((m, d // 2), jnp.uint32),
                   jax.ShapeDtypeStruct((e, m), F32)],
        grid=(m // tm,),
        in_specs=[pl.BlockSpec((tm, d), lambda i: (i, 0)),
                  pl.BlockSpec((1, d), lambda i: (0, 0)),
                  _mod_spec(mod3, tm, group_rows, d, row_of, lambda i: shift_chunk),
                  _mod_spec(mod3, tm, group_rows, d, row_of, lambda i: scale_chunk),
                  pl.BlockSpec((e, d), lambda i: (0, 0))],
        out_specs=[pl.BlockSpec((tm, d), lambda i: (i, 0)),
                   pl.BlockSpec((tm, d // 2), lambda i: (i, 0)),
                   pl.BlockSpec((e, tm), lambda i: (0, i))],
        compiler_params=_params(("arbitrary",), est),
        name="ffn_norm_router",
    )(x2, g.reshape(1, d), mod3, mod3, w_router_t)


def _first_max(vals, idx):
    mx = jnp.max(vals, axis=0, keepdims=True)
    first = jnp.min(jnp.where(vals == mx, idx, jnp.int32(2 ** 30)), axis=0, keepdims=True)
    return mx, first


def _route_kernel(lg_ref, bias_ref, eidx_ref, ew_ref, rank_ref, cnt_ref, base_scr,
                  *, n_groups, topk_groups, top_k, scale):
    e, tr = lg_ref.shape
    per = e // n_groups

    @pl.when(pl.program_id(0) == 0)
    def _():
        base_scr[...] = jnp.zeros_like(base_scr)

    scores = jax.nn.sigmoid(lg_ref[...])
    biased = scores + bias_ref[...]
    eid = lax.broadcasted_iota(jnp.int32, (e, tr), 0)
    neg = jnp.float32(-jnp.inf)

    grp_rows = []
    bid = lax.broadcasted_iota(jnp.int32, (per, tr), 0)
    for g in range(n_groups):
        blk = biased[g * per:(g + 1) * per]
        m1, i1 = _first_max(blk, bid)
        m2 = jnp.max(jnp.where(bid == i1, neg, blk), axis=0, keepdims=True)
        grp_rows.append(m1 + m2)
    grp = jnp.concatenate(grp_rows, axis=0)
    gid = lax.broadcasted_iota(jnp.int32, (n_groups, tr), 0)
    gsel = jnp.zeros((n_groups, tr), F32)
    work = grp
    for _ in range(topk_groups):
        _, gi = _first_max(work, gid)
        hit = gid == gi
        gsel = jnp.where(hit, 1.0, gsel)
        work = jnp.where(hit, neg, work)
    emask = jnp.concatenate(
        [jnp.broadcast_to(gsel[g:g + 1], (per, tr)) for g in range(n_groups)], axis=0)
    masked = jnp.where(emask > 0.0, biased, neg)

    onehots, idxs, wts = [], [], []
    for _ in range(top_k):
        _, ei = _first_max(masked, eid)
        hit = eid == ei
        onehots.append(hit)
        idxs.append(ei)
        wts.append(jnp.sum(jnp.where(hit, scores, 0.0), axis=0, keepdims=True))
        masked = jnp.where(hit, neg, masked)
    wsum = wts[0]
    for wk in wts[1:]:
        wsum = wsum + wk

    chosen = onehots[0]
    for oh in onehots[1:]:
        chosen = jnp.logical_or(chosen, oh)
    chosen_f = jnp.where(chosen, 1.0, 0.0)
    si = lax.broadcasted_iota(jnp.int32, (tr, tr), 0)
    ti = lax.broadcasted_iota(jnp.int32, (tr, tr), 1)
    upper = jnp.where(si < ti, 1.0, 0.0).astype(BF16)
    prefix = jnp.dot(chosen_f.astype(BF16), upper, preferred_element_type=F32)
    pos = prefix + base_scr[:, 0:1]
    for k in range(top_k):
        eidx_ref[k:k + 1, :] = idxs[k]
        ew_ref[k:k + 1, :] = wts[k] / wsum * scale
        rank_ref[k:k + 1, :] = jnp.sum(jnp.where(onehots[k], pos, 0.0), axis=0, keepdims=True).astype(jnp.int32)
    for k in range(top_k, eidx_ref.shape[0]):
        eidx_ref[k:k + 1, :] = jnp.zeros((1, tr), jnp.int32)
        ew_ref[k:k + 1, :] = jnp.zeros((1, tr), F32)
        rank_ref[k:k + 1, :] = jnp.zeros((1, tr), jnp.int32)
    base_scr[...] = base_scr[...] + jnp.sum(chosen_f, axis=1, keepdims=True)
    cnt_ref[...] = base_scr[...].astype(jnp.int32)


def _route(logits_t, router_bias):
    e, n = logits_t.shape
    tr = _pick_tile(n, 640, LANES)
    rows = SUBLANES
    assert TOP_K <= rows and e % N_EXPERT_GROUPS == 0
    kern = functools.partial(_route_kernel, n_groups=N_EXPERT_GROUPS, topk_groups=TOPK_GROUPS,
                             top_k=TOP_K, scale=ROUTED_SCALE)
    est = 2 * (_nbytes((e, tr), F32) + 3 * _nbytes((rows, tr), F32)) + 24 * _nbytes((e, tr), F32) \
        + 3 * _nbytes((tr, tr), F32)
    out_row = pl.BlockSpec((rows, tr), lambda i: (0, i))
    return pl.pallas_call(
        kern,
        out_shape=[jax.ShapeDtypeStruct((rows, n), jnp.int32),
                   jax.ShapeDtypeStruct((rows, n), F32),
                   jax.ShapeDtypeStruct((rows, n), jnp.int32),
                   jax.ShapeDtypeStruct((e, LANES), jnp.int32)],
        grid=(n // tr,),
        in_specs=[pl.BlockSpec((e, tr), lambda i: (0, i)),
                  pl.BlockSpec((e, 1), lambda i: (0, 0))],
        out_specs=[out_row, out_row, out_row, pl.BlockSpec((e, LANES), lambda i: (0, 0))],
        scratch_shapes=[pltpu.VMEM((e, LANES), F32)],
        compiler_params=_params(("arbitrary",), est),
        name="route",
    )(logits_t, router_bias.reshape(e, 1))


def _dispatch_kernel(dest_ref, fill_ref, nfill_ref, hp_ref, xs_ref, zero_scr, sem, *, top_k, n_tok):
    step = pl.program_id(0)
    tt = hp_ref.shape[0]
    fill_rows = zero_scr.shape[0]

    @pl.when(step == 0)
    def _():
        zero_scr[...] = jnp.zeros_like(zero_scr)
        n_fill = nfill_ref[0]

        def start_fill(i, c):
            row0 = pl.multiple_of(fill_ref[i] * fill_rows, fill_rows)
            pltpu.make_async_copy(zero_scr, xs_ref.at[pl.ds(row0, fill_rows)], sem).start()
            return c

        def wait_fill(i, c):
            pltpu.make_async_copy(zero_scr, xs_ref.at[pl.ds(0, fill_rows)], sem).wait()
            return c

        lax.fori_loop(0, n_fill, start_fill, 0)
        lax.fori_loop(0, n_fill, wait_fill, 0)

    base = step * tt

    def start(r, c):
        for k in range(top_k):
            pltpu.make_async_copy(hp_ref.at[pl.ds(r, 1)],
                                  xs_ref.at[pl.ds(dest_ref[k * n_tok + base + r], 1)], sem).start()
        return c

    def wait(r, c):
        for k in range(top_k):
            pltpu.make_async_copy(hp_ref.at[pl.ds(r, 1)], xs_ref.at[pl.ds(0, 1)], sem).wait()
        return c

    lax.fori_loop(0, tt, start, 0)
    lax.fori_loop(0, tt, wait, 0)


def _dispatch(dest_flat, fill_tiles, n_fill, hp, n_rows, fill_rows):
    n_tok, half = hp.shape
    tt = _pick_tile(n_tok, 64, SUBLANES)
    kern = functools.partial(_dispatch_kernel, top_k=TOP_K, n_tok=n_tok)
    est = 2 * _nbytes((tt, half), hp.dtype) + _nbytes((fill_rows, half), hp.dtype)
    return pl.pallas_call(
        kern,
        out_shape=jax.ShapeDtypeStruct((n_rows, half), hp.dtype),
        grid_spec=pltpu.PrefetchScalarGridSpec(
            num_scalar_prefetch=3,
            grid=(n_tok // tt,),
            in_specs=[pl.BlockSpec((tt, half), lambda i, dest, fill, nfill: (i, 0))],
            out_specs=pl.BlockSpec(memory_space=pl.ANY),
            scratch_shapes=[pltpu.VMEM((fill_rows, half), hp.dtype), pltpu.SemaphoreType.DMA]),
        compiler_params=_params(("arbitrary",), est),
        name="moe_dispatch",
    )(dest_flat, fill_tiles, n_fill, hp)


def _expert_up_kernel(tile_e_ref, used_ref, xs_ref, wg_ref, wu_ref, o_ref, wg_scr, wu_scr):
    t = pl.program_id(1)
    n_used = used_ref[0]

    @pl.when(t < n_used)
    def _():
        prev = tile_e_ref[jnp.maximum(t - 1, 0)]
        fresh = jnp.logical_or(t == 0, tile_e_ref[t] != prev)

        @pl.when(fresh)
        def _():
            wg_scr[...] = wg_ref[0].astype(BF16)
            wu_scr[...] = wu_ref[0].astype(BF16)

        half = xs_ref.shape[1]
        xl, xr = _unpack_bf16_pair(xs_ref[...])
        xl = xl.astype(BF16)
        xr = xr.astype(BF16)
        gate = jnp.dot(xl, wg_scr[:half], preferred_element_type=F32) \
            + jnp.dot(xr, wg_scr[half:], preferred_element_type=F32)
        up = jnp.dot(xl, wu_scr[:half], preferred_element_type=F32) \
            + jnp.dot(xr, wu_scr[half:], preferred_element_type=F32)
        o_ref[...] = (_silu(gate) * up).astype(o_ref.dtype)

    @pl.when(t >= n_used)
    def _():
        o_ref[...] = jnp.zeros_like(o_ref)


def _expert_up(tile_e, n_used, xs, w_gate, w_up, tm, fc):
    p, half = xs.shape
    e, d, de = w_gate.shape
    n_t = p // tm
    clamp = lambda t, used: jnp.minimum(t, used[0] - 1)
    est = 2 * (_nbytes((tm, half), jnp.uint32) + 2 * _nbytes((d, fc), F32)
               + _nbytes((tm, fc), BF16)) + 2 * _nbytes((d, fc), BF16) + 2 * _nbytes((tm, d), BF16) \
        + 4 * _nbytes((tm, fc), F32) + 2 * _nbytes((tm, half), F32)
    return pl.pallas_call(
        _expert_up_kernel,
        out_shape=jax.ShapeDtypeStruct((p, de), BF16),
        grid_spec=pltpu.PrefetchScalarGridSpec(
            num_scalar_prefetch=2,
            grid=(de // fc, n_t),
            in_specs=[pl.BlockSpec((tm, half), lambda j, t, te, used: (clamp(t, used), 0)),
                      pl.BlockSpec((1, d, fc), lambda j, t, te, used: (te[clamp(t, used)], 0, j)),
                      pl.BlockSpec((1, d, fc), lambda j, t, te, used: (te[clamp(t, used)], 0, j))],
            out_specs=pl.BlockSpec((tm, fc), lambda j, t, te, used: (t, j)),
            scratch_shapes=[pltpu.VMEM((d, fc), BF16), pltpu.VMEM((d, fc), BF16)]),
        compiler_params=_params(("arbitrary", "arbitrary"), est),
        name="expert_up",
    )(tile_e, n_used, xs, w_gate, w_up)


def _expert_down_kernel(tile_e_ref, used_ref, a_ref, wl_ref, wr_ref, o_ref, wl_scr, wr_scr):
    t = pl.program_id(1)
    n_used = used_ref[0]

    @pl.when(t < n_used)
    def _():
        prev = tile_e_ref[jnp.maximum(t - 1, 0)]
        fresh = jnp.logical_or(t == 0, tile_e_ref[t] != prev)

        @pl.when(fresh)
        def _():
            wl_scr[...] = wl_ref[0].astype(BF16)
            wr_scr[...] = wr_ref[0].astype(BF16)

        a = a_ref[...]
        yl = jnp.dot(a, wl_scr[...], preferred_element_type=F32)
        yr = jnp.dot(a, wr_scr[...], preferred_element_type=F32)
        o_ref[...] = _pack_bf16_pair(yl, yr)

    @pl.when(t >= n_used)
    def _():
        o_ref[...] = jnp.zeros_like(o_ref)


def _expert_down(tile_e, n_used, act, w_down, tm, nc):
    p, de = act.shape
    e, _, d = w_down.shape
    half = d // 2
    n_t = p // tm
    n_c = half // nc
    clamp = lambda t, used: jnp.minimum(t, used[0] - 1)
    est = 2 * (_nbytes((tm, de), BF16) + 2 * _nbytes((de, nc), F32) + _nbytes((tm, nc), jnp.uint32)) \
        + 2 * _nbytes((de, nc), BF16) + 6 * _nbytes((tm, nc), F32)
    return pl.pallas_call(
        _expert_down_kernel,
        out_shape=jax.ShapeDtypeStruct((p, half), jnp.uint32),
        grid_spec=pltpu.PrefetchScalarGridSpec(
            num_scalar_prefetch=2,
            grid=(n_c, n_t),
            in_specs=[pl.BlockSpec((tm, de), lambda c, t, te, used: (clamp(t, used), 0)),
                      pl.BlockSpec((1, de, nc), lambda c, t, te, used: (te[clamp(t, used)], 0, c)),
                      pl.BlockSpec((1, de, nc), lambda c, t, te, used: (te[clamp(t, used)], 0, c + n_c))],
            out_specs=pl.BlockSpec((tm, nc), lambda c, t, te, used: (t, c)),
            scratch_shapes=[pltpu.VMEM((de, nc), BF16), pltpu.VMEM((de, nc), BF16)]),
        compiler_params=_params(("arbitrary", "arbitrary"), est),
        name="expert_down",
    )(tile_e, n_used, act, w_down, w_down)


def _combine_kernel(dest_ref, ys_ref, x_ref, sh_ref, gf_ref, ew_ref, o_ref, buf, sem, *, top_k, n_tok, tok0):
    tt = x_ref.shape[0]
    half = ys_ref.shape[1]
    base = tok0 + pl.program_id(0) * tt

    def start(r, c):
        for k in range(top_k):
            pltpu.make_async_copy(ys_ref.at[pl.ds(dest_ref[k * n_tok + base + r], 1)],
                                  buf.at[k, pl.ds(r, 1)], sem).start()
        return c

    def wait(r, c):
        for k in range(top_k):
            pltpu.make_async_copy(ys_ref.at[pl.ds(0, 1)], buf.at[k, pl.ds(r, 1)], sem).wait()
        return c

    lax.fori_loop(0, tt, start, 0)
    lax.fori_loop(0, tt, wait, 0)
    ew = ew_ref[...]
    left, right = _unpack_bf16_pair(buf[0])
    left = left * ew[:, 0:1]
    right = right * ew[:, 0:1]
    for k in range(1, top_k):
        l2, r2 = _unpack_bf16_pair(buf[k])
        left = left + l2 * ew[:, k:k + 1]
        right = right + r2 * ew[:, k:k + 1]
    gf = gf_ref[0]
    o_ref[:, :half] = x_ref[:, :half] + gf[:, :half] * (left + sh_ref[:, :half])
    o_ref[:, half:] = x_ref[:, half:] + gf[:, half:] * (right + sh_ref[:, half:])


def _combine(dest_flat, ys, x2, shared, mod3, ew_t, gate_chunk, group_rows, n_tok, tok0):
    m, d = x2.shape
    half = d // 2
    ew_rows = ew_t.shape[1]
    tt = _pick_tile(min(m, group_rows) if mod3.shape[1] == 1 else m, 64, SUBLANES)
    sh_blk0 = tok0 // tt
    assert tok0 % tt == 0
    row_of = lambda i, dest: i
    kern = functools.partial(_combine_kernel, top_k=TOP_K, n_tok=n_tok, tok0=tok0)
    est = 2 * (3 * _nbytes((tt, d), F32) + _nbytes((tt, d), F32)) + _nbytes((TOP_K, tt, half), jnp.uint32) \
        + 6 * _nbytes((tt, d), F32)
    return pl.pallas_call(
        kern,
        out_shape=jax.ShapeDtypeStruct((m, d), F32),
        grid_spec=pltpu.PrefetchScalarGridSpec(
            num_scalar_prefetch=1,
            grid=(m // tt,),
            in_specs=[pl.BlockSpec(memory_space=pl.ANY),
                      pl.BlockSpec((tt, d), lambda i, dest: (i, 0)),
                      pl.BlockSpec((tt, d), lambda i, dest: (i + sh_blk0, 0)),
                      _mod_spec(mod3, tt, group_rows, d, row_of, lambda i, dest: gate_chunk),
                      pl.BlockSpec((tt, ew_rows), lambda i, dest: (i + sh_blk0, 0))],
            out_specs=pl.BlockSpec((tt, d), lambda i, dest: (i, 0)),
            scratch_shapes=[pltpu.VMEM((TOP_K, tt, half), jnp.uint32), pltpu.SemaphoreType.DMA]),
        compiler_params=_params(("arbitrary",), est),
        name="moe_combine",
    )(dest_flat, ys, x2, shared, mod3, ew_t)


def _token_mixing(x2, mod3, group_rows, p, attend):
    m, d = x2.shape
    aw, kvw, gmw, hd = p["aw"], p["kvw"], p["gmw"], p["hd"]
    tn = 512
    assert aw % tn == 0 and (2 * kvw) % tn == 0 and gmw % tn == 0 and d % tn == 0
    tm = _pick_tile(min(m, group_rows) if mod3.shape[1] == 1 else m, 1024, 16)
    h = _modulate(x2, p["norm_mix_g"], mod3, 0, 1, group_rows)
    w_in = p["w_in"]

    qkv_w = aw + 2 * kvw

    def qkv_epilogue(accs, extras, n, mi):
        gain_ref, flag_ref, bd_ref = extras
        z = accs[0]
        sq_hi, sq_lo = _split_bf16(z * z)
        ss = jnp.dot(sq_hi, bd_ref[...], preferred_element_type=F32) \
            + jnp.dot(sq_lo, bd_ref[...], preferred_element_type=F32)
        inv = lax.rsqrt(ss * (1.0 / hd) + EPS)
        return [z * jnp.where(flag_ref[...] > 0.0, inv, 1.0) * gain_ref[...]]

    qkv = _mm([(h, d, 0)], [(w_in, 0)],
              [(p["qkv_gain"], pl.BlockSpec((1, tn), lambda n, mi: (0, n))),
               (p["qkv_flag"], pl.BlockSpec((1, tn), lambda n, mi: (0, n))),
               (p["head_ones"], pl.BlockSpec((tn, tn), lambda n, mi: (0, 0)))],
              qkv_epilogue,
              [(jax.ShapeDtypeStruct((m, qkv_w), F32), pl.BlockSpec((tm, tn), lambda n, mi: (mi, n)))],
              m=m, tm=tm, tn=tn, n_tiles=qkv_w // tn, pairs=[(0, 0)], name="in_proj_qkv")[0]

    ugv_dtype = BF16 if mod3.shape[1] == 1 else F32
    ugv = _mm([(h, d, 0)], [(w_in, qkv_w // tn)], [],
              lambda accs, extras, n, mi: [_gelu(accs[0])],
              [(jax.ShapeDtypeStruct((m, 2 * gmw), ugv_dtype), pl.BlockSpec((tm, tn), lambda n, mi: (mi, n)))],
              m=m, tm=tm, tn=tn, n_tiles=2 * gmw // tn, pairs=[(0, 0)], name="in_proj_gmlp")[0]

    gates = _mm([(h, d, 0)], [(w_in, (qkv_w + 2 * gmw) // tn)], [],
                lambda accs, extras, n, mi: [jax.nn.sigmoid(accs[0])],
                [(jax.ShapeDtypeStruct((m, 2 * d), BF16), pl.BlockSpec((tm, tn), lambda n, mi: (mi, n)))],
                m=m, tm=tm, tn=tn, n_tiles=2 * d // tn, pairs=[(0, 0)], name="in_proj_gates")[0]

    o_attn, o_gmlp, aux = attend(qkv, ugv)

    nd = d // tn
    merged = _mm([(o_attn, aw, 0), (o_gmlp, gmw, 0)], [(p["w_branch_attn"], 0), (p["w_branch_gmlp"], 0)],
                 [(gates, pl.BlockSpec((tm, tn), lambda n, mi: (mi, n))),
                  (gates, pl.BlockSpec((tm, tn), lambda n, mi: (mi, n + nd)))],
                 lambda accs, extras, n, mi: [extras[0][...].astype(F32) * accs[0]
                                              + extras[1][...].astype(F32) * accs[1]],
                 [(jax.ShapeDtypeStruct((m, d), BF16), pl.BlockSpec((tm, tn), lambda n, mi: (mi, n)))],
                 m=m, tm=tm, tn=tn, n_tiles=nd, pairs=[(0, 0), (1, 1)], name="branch_merge")[0]

    x1 = _mm([(merged, d, 0)], [(p["w_out"], 0)],
             [(x2, pl.BlockSpec((tm, tn), lambda n, mi: (mi, n))),
              (mod3, _mod_spec(mod3, tm, group_rows, tn, lambda n, mi: mi, lambda n, mi: 2 * nd + n))],
             lambda accs, extras, n, mi: [extras[0][...] + extras[1][0] * accs[0]],
             [(jax.ShapeDtypeStruct((m, d), F32), pl.BlockSpec((tm, tn), lambda n, mi: (mi, n)))],
             m=m, tm=tm, tn=tn, n_tiles=nd, pairs=[(0, 0)], name="out_proj")[0]
    return x1, qkv, aux


def kernel(x_prompt, x_sample, cache_k_win, cache_v_win, c_prompt, c_sample, norm_mix_g, norm_ffn_g,
           w_ada, b_ada, w_in, q_norm_g, k_norm_g, attn_sinks, gm_ln_g, gm_ln_b, w_spatial, b_spatial,
           w_branch_attn, w_branch_gmlp, w_out, w_router, router_bias, w_gate, w_up, w_down,
           ws_gate, ws_up, ws_down):
    depth = norm_mix_g.shape[0]
    assert depth == 1, "single-layer step"
    batch, seq, d = x_prompt.shape
    db, t_new, _ = x_sample.shape
    assert t_new == 1, "one new token per sequence"
    _, _, win, n_kv, hd = cache_k_win.shape
    n_grp = attn_sinks.shape[-1]
    aw, kvw = n_kv * n_grp * hd, n_kv * hd
    gmw = gm_ln_g.shape[-1]
    n_groups, ch, _ = w_spatial.shape[1:]
    n_exp = w_router.shape[-1]
    de = w_gate.shape[-1]
    ds = ws_gate.shape[-1]
    tn = 512
    l = 0

    n_c = batch + db
    rows = -(-n_c // 16) * 16
    c_all = jnp.concatenate([c_prompt, c_sample, jnp.zeros((rows - n_c, d), F32)], axis=0)
    mods = _adaln(c_all, w_ada[l], b_ada[l])
    mod_p = mods[:batch].reshape(batch, 1, 6 * d)
    mod_s = mods[batch:n_c].reshape(1, db, 6 * d)

    gq = jnp.tile(q_norm_g[l], aw // hd)
    gk = jnp.tile(k_norm_g[l], kvw // hd)
    qkv_gain = jnp.concatenate([gq, gk, jnp.ones((kvw,), F32)]).reshape(1, -1)
    qkv_flag = jnp.concatenate([jnp.ones((aw + kvw,), F32), jnp.zeros((kvw,), F32)]).reshape(1, -1)
    hid = jnp.arange(tn) // hd
    head_ones = (hid[:, None] == hid[None, :]).astype(BF16)
    params = dict(aw=aw, kvw=kvw, gmw=gmw, hd=hd, norm_mix_g=norm_mix_g[l], w_in=w_in[l],
                  qkv_gain=qkv_gain, qkv_flag=qkv_flag, head_ones=head_ones,
                  w_branch_attn=w_branch_attn[l], w_branch_gmlp=w_branch_gmlp[l], w_out=w_out[l])
    sinks = attn_sinks[l]

    def attend_prompt(qkv, ugv):
        o_attn = _attn_prompt(qkv, sinks, batch, seq, n_kv, n_grp, hd)
        o_gmlp = _gmlp_prompt(ugv, gm_ln_g[l], gm_ln_b[l], w_spatial[l], b_spatial[l].T)
        return o_attn, o_gmlp, None

    xp2 = x_prompt.reshape(batch * seq, d)
    x1_p, qkv_p, _ = _token_mixing(xp2, mod_p, seq, params, attend_prompt)

    ck = cache_k_win[l].reshape(db, win, kvw)
    cv = cache_v_win[l].reshape(db, win, kvw)

    def attend_sample(qkv, ugv):
        q3 = qkv[:, :aw].reshape(db, n_kv * n_grp, hd)
        o3, new_k, new_v = _attn_sample(q3, qkv, ck, cv, sinks.reshape(-1, 1), n_kv, n_grp, hd)
        w00 = jnp.repeat(w_spatial[l][:, 0, 0], gmw // n_groups).reshape(1, gmw)
        b0 = jnp.repeat(b_spatial[l][:, 0], gmw // n_groups).reshape(1, gmw)
        o_gmlp, vn = _gmlp_sample(ugv, gm_ln_g[l], gm_ln_b[l], w00, b0)
        return o3.reshape(db, aw), o_gmlp, (new_k, new_v, vn)

    xs2 = x_sample.reshape(db, d)
    x1_s, _, (new_k_s, new_v_s, vn_s) = _token_mixing(xs2, mod_s, 1, params, attend_sample)

    w_router_t = w_router[l].T
    h2_p, hp_p, lg_p = _ffn_norm(x1_p, norm_ffn_g[l], mod_p, 3, 4, seq, w_router_t)
    h2_s, hp_s, lg_s = _ffn_norm(x1_s, norm_ffn_g[l], mod_s, 3, 4, 1, w_router_t)
    h2 = jnp.concatenate([h2_p, h2_s], axis=0)
    hp = jnp.concatenate([hp_p, hp_s], axis=0)
    logits_t = jnp.concatenate([lg_p, lg_s], axis=1)
    n_tok = h2.shape[0]

    eidx8, ew8, rank8, counts = _route(logits_t, router_bias[l])
    eidx, ew, rank = eidx8[:TOP_K], ew8[:TOP_K], rank8[:TOP_K]
    counts = counts[:, 0]

    tm_e = 256
    n_assign = n_tok * TOP_K
    n_tiles = -(-n_assign // tm_e) + n_exp
    n_rows = n_tiles * tm_e
    padded = (counts + tm_e - 1) // tm_e * tm_e
    pad_end = jnp.cumsum(padded)
    pad_start = pad_end - padded
    expert_ids = jnp.arange(n_exp, dtype=jnp.int32)
    dest = rank + jnp.sum(jnp.where(eidx[:, :, None] == expert_ids, pad_start.astype(jnp.int32), 0), axis=-1)
    dest_flat = dest.reshape(-1)
    tile_ids = jnp.arange(n_tiles, dtype=jnp.int32)
    tile_e = jnp.minimum(jnp.sum((pad_end[None, :] <= tile_ids[:, None] * tm_e).astype(jnp.int32), axis=1),
                         n_exp - 1).astype(jnp.int32)
    used_tiles = (pad_end[-1:] // tm_e).astype(jnp.int32)
    pad_fill = jnp.maximum(pad_end // tm_e - 1, 0).astype(jnp.int32)
    tail_fill = jnp.minimum(used_tiles + tile_ids, n_tiles - 1)
    fill_tiles = jnp.concatenate([pad_fill, tail_fill])
    n_fill = (n_exp + n_tiles - used_tiles).astype(jnp.int32)

    xs = _dispatch(dest_flat, fill_tiles, n_fill, hp, n_rows, tm_e)
    act = _expert_up(tile_e, used_tiles, xs, w_gate[l], w_up[l], tm_e, _pick_tile(de, 512, LANES))
    ys = _expert_down(tile_e, used_tiles, act, w_down[l], tm_e, _pick_tile(d // 2, 1024, LANES))

    tm_s = _pick_tile(n_tok, 640, 16)
    tn_s = _pick_tile(ds, 256, LANES)
    sh_act = _mm([(h2, d, 0)], [(ws_gate[l], 0), (ws_up[l], 0)], [],
                 lambda accs, extras, n, mi: [_silu(accs[0]) * accs[1]],
                 [(jax.ShapeDtypeStruct((n_tok, ds), BF16), pl.BlockSpec((tm_s, tn_s), lambda n, mi: (mi, n)))],
                 m=n_tok, tm=tm_s, tn=tn_s, n_tiles=ds // tn_s, pairs=[(0, 0), (0, 1)], name="shared_up")[0]
    shared = _mm([(sh_act, ds, 0)], [(ws_down[l], 0)], [],
                 lambda accs, extras, n, mi: [accs[0]],
                 [(jax.ShapeDtypeStruct((n_tok, d), F32), pl.BlockSpec((tm_s, tn), lambda n, mi: (mi, n)))],
                 m=n_tok, tm=tm_s, tn=tn, n_tiles=d // tn, pairs=[(0, 0)], name="shared_down")[0]

    ew_t = ew8.T
    y_p = _combine(dest_flat, ys, x1_p, shared, mod_p, ew_t, 5, seq, n_tok, 0)
    y_s = _combine(dest_flat, ys, x1_s, shared, mod_s, ew_t, 5, 1, n_tok, batch * seq)

    w_keep = min(WINDOW, seq)
    qkv_p3 = qkv_p.reshape(batch, seq, aw + 2 * kvw)
    new_k_p = qkv_p3[:, seq - w_keep:, aw:aw + kvw].reshape(1, batch, w_keep, n_kv, hd)
    new_v_p = qkv_p3[:, seq - w_keep:, aw + kvw:].reshape(1, batch, w_keep, n_kv, hd)
    return (y_p.reshape(batch, seq, d), y_s.reshape(db, 1, d), new_k_p, new_v_p,
            new_k_s.reshape(1, db, win, n_kv, hd), new_v_s.reshape(1, db, win, n_kv, hd),
            vn_s.reshape(1, db, 1, gmw))
```

```python
import functools

import jax
import jax.numpy as jnp
from jax import lax
from jax.experimental import pallas as pl
from jax.experimental.pallas import tpu as pltpu

TOP_K = 6
N_EXPERT_GROUPS = 8
TOPK_GROUPS = 4
ROUTED_SCALE = 2.5
WINDOW = 128
EPS = 1e-6

TILE_HAS_NEXT = 1
TILE_HALF_FULL = 2

V7X_VMEM_BYTES = 64 * 1024 * 1024
V7X_VMEM_REQUEST_CAP = 56 * 1024 * 1024
LANES = 128
SUBLANES = 8

BF16 = jnp.bfloat16
F32 = jnp.float32


def _pick_tile(total, preferred, multiple):
    t = min(preferred, total)
    t -= t % multiple
    while t > multiple and total % t:
        t -= multiple
    assert t > 0 and total % t == 0, (total, preferred, multiple)
    return t


def _params(semantics, vmem_bytes):
    return pltpu.CompilerParams(
        dimension_semantics=semantics,
        vmem_limit_bytes=int(min(V7X_VMEM_REQUEST_CAP, max(vmem_bytes, 16 * 1024 * 1024))))


def _nbytes(shape, dtype):
    n = 1
    for s in shape:
        n *= s
    return n * jnp.dtype(dtype).itemsize


def _mod_spec(mod3, tm, group_rows, width, row_of, col_of):
    if mod3.shape[1] == 1:
        return pl.BlockSpec((1, 1, width), lambda *g: ((row_of(*g) * tm) // group_rows, 0, col_of(*g)))
    return pl.BlockSpec((1, tm, width), lambda *g: (0, row_of(*g), col_of(*g)))


def _adaln_kernel(c_ref, w_ref, b_ref, o_ref, a_scr):
    @pl.when(pl.program_id(0) == 0)
    def _():
        c = c_ref[...]
        a_scr[...] = (c * jax.nn.sigmoid(c)).astype(BF16)

    acc = jnp.dot(a_scr[...], w_ref[...].astype(BF16), preferred_element_type=F32)
    o_ref[...] = acc + b_ref[...]


def _adaln(c_all, w_ada, b_ada):
    rows, d = c_all.shape
    n_out = w_ada.shape[1]
    tn = _pick_tile(n_out, 512, LANES)
    est = 2 * (_nbytes((d, tn), F32) + _nbytes((rows, tn), F32)) + _nbytes((rows, d), F32) * 2 \
        + _nbytes((rows, d), BF16) + _nbytes((d, tn), BF16) + _nbytes((rows, tn), F32)
    return pl.pallas_call(
        _adaln_kernel,
        out_shape=jax.ShapeDtypeStruct((rows, n_out), F32),
        grid=(n_out // tn,),
        in_specs=[pl.BlockSpec((rows, d), lambda n: (0, 0)),
                  pl.BlockSpec((d, tn), lambda n: (0, n)),
                  pl.BlockSpec((1, tn), lambda n: (0, n))],
        out_specs=pl.BlockSpec((rows, tn), lambda n: (0, n)),
        scratch_shapes=[pltpu.VMEM((rows, d), BF16)],
        compiler_params=_params(("arbitrary",), est),
        name="adaln",
    )(c_all, w_ada, b_ada.reshape(1, n_out))


def _rms_mod(x, g, scale, shift):
    y = x * lax.rsqrt(jnp.mean(x * x, axis=-1, keepdims=True) + EPS)
    return (y * g) * (1.0 + scale) + shift


def _modulate_kernel(x_ref, g_ref, sh_ref, sc_ref, o_ref):
    o_ref[...] = _rms_mod(x_ref[...], g_ref[...], sc_ref[0], sh_ref[0]).astype(o_ref.dtype)


def _modulate(x2, g, mod3, shift_chunk, scale_chunk, group_rows):
    m, d = x2.shape
    tm = _pick_tile(min(m, group_rows) if mod3.shape[1] == 1 else m, 256, SUBLANES)
    row_of = lambda i: i
    est = 2 * (_nbytes((tm, d), F32) * 3 + _nbytes((tm, d), BF16)) + 4 * _nbytes((tm, d), F32)
    return pl.pallas_call(
        _modulate_kernel,
        out_shape=jax.ShapeDtypeStruct((m, d), BF16),
        grid=(m // tm,),
        in_specs=[pl.BlockSpec((tm, d), lambda i: (i, 0)),
                  pl.BlockSpec((1, d), lambda i: (0, 0)),
                  _mod_spec(mod3, tm, group_rows, d, row_of, lambda i: shift_chunk),
                  _mod_spec(mod3, tm, group_rows, d, row_of, lambda i: scale_chunk)],
        out_specs=pl.BlockSpec((tm, d), lambda i: (i, 0)),
        compiler_params=_params(("arbitrary",), est),
        name="modulate",
    )(x2, g.reshape(1, d), mod3, mod3)


def _mm_kernel(*refs, n_a, n_w, n_extra, n_out, pairs, epilogue):
    a_refs = refs[:n_a]
    w_refs = refs[n_a:n_a + n_w]
    extra_refs = refs[n_a + n_w:n_a + n_w + n_extra]
    out_refs = refs[n_a + n_w + n_extra:n_a + n_w + n_extra + n_out]
    w_scr = refs[n_a + n_w + n_extra + n_out:]
    n = pl.program_id(0)
    mi = pl.program_id(1)

    @pl.when(mi == 0)
    def _():
        for w_ref, scr in zip(w_refs, w_scr):
            scr[...] = w_ref[...].astype(BF16)

    accs = [jnp.dot(a_refs[ia][...], w_scr[iw][...], preferred_element_type=F32) for ia, iw in pairs]
    outs = epilogue(accs, extra_refs, n, mi)
    for o_ref, val in zip(out_refs, outs):
        o_ref[...] = val.astype(o_ref.dtype)


def _mm(a_list, w_list, extras, epilogue, outs, *, m, tm, tn, n_tiles, pairs, name):
    in_specs, args, est = [], [], 0
    for arr, kw, cb in a_list:
        in_specs.append(pl.BlockSpec((tm, kw), lambda n, mi, cb=cb: (mi, cb)))
        args.append(arr)
        est += 2 * _nbytes((tm, kw), arr.dtype)
    scratch = []
    for arr, off in w_list:
        k = arr.shape[0]
        in_specs.append(pl.BlockSpec((k, tn), lambda n, mi, off=off: (0, n + off)))
        args.append(arr)
        scratch.append(pltpu.VMEM((k, tn), BF16))
        est += 2 * _nbytes((k, tn), F32) + _nbytes((k, tn), BF16)
    for arr, spec in extras:
        in_specs.append(spec)
        args.append(arr)
        est += 2 * _nbytes(spec.block_shape, arr.dtype)
    for sds, spec in outs:
        est += 2 * _nbytes(spec.block_shape, sds.dtype)
    est += (len(pairs) + 2) * _nbytes((tm, tn), F32)
    kern = functools.partial(_mm_kernel, n_a=len(a_list), n_w=len(w_list), n_extra=len(extras),
                             n_out=len(outs), pairs=tuple(pairs), epilogue=epilogue)
    res = pl.pallas_call(
        kern,
        out_shape=[sds for sds, _ in outs],
        grid=(n_tiles, m // tm),
        in_specs=in_specs,
        out_specs=[spec for _, spec in outs],
        scratch_shapes=scratch,
        compiler_params=_params(("arbitrary", "arbitrary"), est),
        name=name,
    )(*args)
    return res


def _gelu(x):
    return x * (lax.erf(x * (2.0 ** -0.5)) + 1.0) * 0.5


def _silu(x):
    return x * jax.nn.sigmoid(x)


def _attn_prompt_kernel(sinks_ref, q_ref, kc_ref, kp_ref, vc_ref, vp_ref, o_ref, *, n_kv, n_grp, hd):
    i = pl.program_id(1)
    w = q_ref.shape[0]
    scale = hd ** -0.5
    rows = n_grp * w
    qi = lax.broadcasted_iota(jnp.int32, (rows, w), 0) % w
    sj = lax.broadcasted_iota(jnp.int32, (rows, w), 1)
    mask_c = sj <= qi
    mask_p = jnp.logical_and(sj >= qi, i > 0)
    ones = jnp.ones((w, w), BF16)
    dn = (((1,), (1,)), ((), ()))
    for kv in range(n_kv):
        ks = slice(kv * hd, (kv + 1) * hd)
        kc = kc_ref[:, ks].astype(BF16)
        kp = kp_ref[:, ks].astype(BF16)
        vc = vc_ref[:, ks].astype(BF16)
        vp = vp_ref[:, ks].astype(BF16)
        cols = [(kv * n_grp + g) * hd for g in range(n_grp)]
        q = jnp.concatenate([q_ref[:, c:c + hd] for c in cols], axis=0).astype(BF16)
        sink = jnp.concatenate([jnp.full((w, 1), sinks_ref[kv, g], F32) for g in range(n_grp)], axis=0)
        lc = jnp.where(mask_c, lax.dot_general(q, kc, dn, preferred_element_type=F32) * scale, -jnp.inf)
        lp = jnp.where(mask_p, lax.dot_general(q, kp, dn, preferred_element_type=F32) * scale, -jnp.inf)
        mx = jnp.maximum(jnp.max(jnp.maximum(lc, lp), axis=-1, keepdims=True), sink)
        pc = jnp.exp(lc - mx).astype(BF16)
        pp = jnp.exp(lp - mx).astype(BF16)
        den = jnp.dot(pc, ones, preferred_element_type=F32) + jnp.dot(pp, ones, preferred_element_type=F32)
        den = den[:, :hd] + jnp.exp(sink - mx)
        o = jnp.dot(pc, vc, preferred_element_type=F32) + jnp.dot(pp, vp, preferred_element_type=F32)
        o = (o / den).astype(o_ref.dtype)
        for g, c in enumerate(cols):
            o_ref[:, c:c + hd] = o[g * w:(g + 1) * w]


def _attn_prompt(qkv, sinks, batch, seq, n_kv, n_grp, hd):
    aw = n_kv * n_grp * hd
    kvw = n_kv * hd
    assert seq % WINDOW == 0 and aw % kvw == 0
    nb = seq // WINDOW
    kcol = aw // kvw
    est = 2 * (_nbytes((WINDOW, aw), F32) + 4 * _nbytes((WINDOW, kvw), F32) + _nbytes((WINDOW, aw), BF16)) \
        + 16 * _nbytes((WINDOW, WINDOW), F32)
    kern = functools.partial(_attn_prompt_kernel, n_kv=n_kv, n_grp=n_grp, hd=hd)
    return pl.pallas_call(
        kern,
        out_shape=jax.ShapeDtypeStruct((batch * seq, aw), BF16),
        grid=(batch, nb),
        in_specs=[pl.BlockSpec(memory_space=pltpu.SMEM),
                  pl.BlockSpec((WINDOW, aw), lambda b, i: (b * nb + i, 0)),
                  pl.BlockSpec((WINDOW, kvw), lambda b, i: (b * nb + i, kcol)),
                  pl.BlockSpec((WINDOW, kvw), lambda b, i: (b * nb + jnp.maximum(i - 1, 0), kcol)),
                  pl.BlockSpec((WINDOW, kvw), lambda b, i: (b * nb + i, kcol + 1)),
                  pl.BlockSpec((WINDOW, kvw), lambda b, i: (b * nb + jnp.maximum(i - 1, 0), kcol + 1))],
        out_specs=pl.BlockSpec((WINDOW, aw), lambda b, i: (b * nb + i, 0)),
        compiler_params=_params(("arbitrary", "arbitrary"), est),
        name="attn_prompt",
    )(sinks, qkv, qkv, qkv, qkv, qkv)


def _attn_sample_kernel(q_ref, kn_ref, vn_ref, ck_ref, cv_ref, sinks_ref, o_ref, ok_ref, ov_ref,
                        *, n_kv, n_grp, hd):
    bs, w, kvw = ck_ref.shape
    nh = n_kv * n_grp
    scale = hd ** -0.5
    head_kv = lax.broadcasted_iota(jnp.int32, (nh, kvw), 0) // n_grp
    lane_kv = lax.broadcasted_iota(jnp.int32, (nh, kvw), 1) // hd
    own = head_kv == lane_kv
    row = lax.broadcasted_iota(jnp.int32, (w, kvw), 0)
    sinks = sinks_ref[...]
    dn = (((1,), (1,)), ((), ()))
    for b in range(bs):
        q = q_ref[b]
        qbd = jnp.where(own, jnp.concatenate([q] * n_kv, axis=-1), 0.0)
        ck = ck_ref[b]
        cv = cv_ref[b]
        kn = kn_ref[pl.ds(b, 1), :]
        vn = vn_ref[pl.ds(b, 1), :]
        logits = lax.dot_general(qbd.astype(BF16), ck.astype(BF16), dn,
                                 preferred_element_type=F32) * scale
        l_new = jnp.sum(qbd.astype(BF16).astype(F32) * kn.astype(BF16).astype(F32),
                        axis=-1, keepdims=True) * scale
        mx = jnp.maximum(jnp.maximum(jnp.max(logits, axis=-1, keepdims=True), l_new), sinks)
        p = jnp.exp(logits - mx)
        p_new = jnp.exp(l_new - mx)
        den = jnp.sum(p, axis=-1, keepdims=True) + p_new + jnp.exp(sinks - mx)
        r = jnp.dot(p.astype(BF16), cv.astype(BF16), preferred_element_type=F32) \
            + p_new.astype(BF16).astype(F32) * vn.astype(BF16).astype(F32)
        r = jnp.where(own, r / den, 0.0)
        o = r[:, 0:hd]
        for kv in range(1, n_kv):
            o = o + r[:, kv * hd:(kv + 1) * hd]
        o_ref[b] = o.astype(o_ref.dtype)
        ok_ref[b] = jnp.where(row == w - 1, kn, pltpu.roll(ck, shift=w - 1, axis=0))
        ov_ref[b] = jnp.where(row == w - 1, vn, pltpu.roll(cv, shift=w - 1, axis=0))


def _attn_sample(q3, qkv_s, cache_k, cache_v, sinks_col, n_kv, n_grp, hd):
    db, w, kvw = cache_k.shape
    nh = n_kv * n_grp
    aw = nh * hd
    kcol = aw // kvw
    bs = _pick_tile(db, 8, SUBLANES)
    est = 2 * (4 * _nbytes((bs, w, kvw), F32) + 2 * _nbytes((bs, kvw), F32) + 2 * _nbytes((bs, nh, LANES), F32)) \
        + 16 * _nbytes((w, kvw), F32)
    kern = functools.partial(_attn_sample_kernel, n_kv=n_kv, n_grp=n_grp, hd=hd)
    return pl.pallas_call(
        kern,
        out_shape=[jax.ShapeDtypeStruct((db, nh, hd), BF16),
                   jax.ShapeDtypeStruct((db, w, kvw), F32),
                   jax.ShapeDtypeStruct((db, w, kvw), F32)],
        grid=(db // bs,),
        in_specs=[pl.BlockSpec((bs, nh, hd), lambda i: (i, 0, 0)),
                  pl.BlockSpec((bs, kvw), lambda i: (i, kcol)),
                  pl.BlockSpec((bs, kvw), lambda i: (i, kcol + 1)),
                  pl.BlockSpec((bs, w, kvw), lambda i: (i, 0, 0)),
                  pl.BlockSpec((bs, w, kvw), lambda i: (i, 0, 0)),
                  pl.BlockSpec((nh, 1), lambda i: (0, 0))],
        out_specs=[pl.BlockSpec((bs, nh, hd), lambda i: (i, 0, 0)),
                   pl.BlockSpec((bs, w, kvw), lambda i: (i, 0, 0)),
                   pl.BlockSpec((bs, w, kvw), lambda i: (i, 0, 0))],
        compiler_params=_params(("arbitrary",), est),
        name="attn_sample",
    )(q3, qkv_s, qkv_s, cache_k, cache_v, sinks_col)


def _layer_norm(v, g, b):
    mu = jnp.mean(v, axis=-1, keepdims=True)
    c = v - mu
    var = jnp.mean(c * c, axis=-1, keepdims=True)
    return c * lax.rsqrt(var + EPS) * g + b


def _gmlp_prompt_kernel(u_ref, v_ref, g_ref, b_ref, ws_ref, bs_ref, o_ref, *, n_groups):
    ch = u_ref.shape[0]
    gd = u_ref.shape[1] // n_groups
    vn = _layer_norm(v_ref[...].astype(F32), g_ref[...], b_ref[...])
    ti = lax.broadcasted_iota(jnp.int32, (ch, ch), 0)
    si = lax.broadcasted_iota(jnp.int32, (ch, ch), 1)
    causal = si <= ti
    for g in range(n_groups):
        cs = slice(g * gd, (g + 1) * gd)
        wc = jnp.where(causal, ws_ref[g], 0.0).astype(BF16)
        mixed = jnp.dot(wc, vn[:, cs].astype(BF16), preferred_element_type=F32) + bs_ref[:, g:g + 1]
        o_ref[:, cs] = (u_ref[:, cs].astype(F32) * mixed).astype(o_ref.dtype)


def _gmlp_prompt(ugv, ln_g, ln_b, w_spatial, b_spatial_t):
    m, two_w = ugv.shape
    gmw = two_w // 2
    n_groups, ch, _ = w_spatial.shape
    assert m % ch == 0
    est = 2 * (2 * _nbytes((ch, gmw), ugv.dtype) + _nbytes((ch, gmw), BF16) + _nbytes(w_spatial.shape, F32)) \
        + 6 * _nbytes((ch, gmw), F32)
    kern = functools.partial(_gmlp_prompt_kernel, n_groups=n_groups)
    return pl.pallas_call(
        kern,
        out_shape=jax.ShapeDtypeStruct((m, gmw), BF16),
        grid=(m // ch,),
        in_specs=[pl.BlockSpec((ch, gmw), lambda i: (i, 0)),
                  pl.BlockSpec((ch, gmw), lambda i: (i, 1)),
                  pl.BlockSpec((1, gmw), lambda i: (0, 0)),
                  pl.BlockSpec((1, gmw), lambda i: (0, 0)),
                  pl.BlockSpec((n_groups, ch, ch), lambda i: (0, 0, 0)),
                  pl.BlockSpec((ch, n_groups), lambda i: (0, 0))],
        out_specs=pl.BlockSpec((ch, gmw), lambda i: (i, 0)),
        compiler_params=_params(("arbitrary",), est),
        name="gmlp_prompt",
    )(ugv, ugv, ln_g.reshape(1, gmw), ln_b.reshape(1, gmw), w_spatial, b_spatial_t)


def _gmlp_sample_kernel(u_ref, v_ref, g_ref, b_ref, w0_ref, b0_ref, o_ref, vn_ref):
    vn = _layer_norm(v_ref[...], g_ref[...], b_ref[...])
    vn_ref[...] = vn
    o_ref[...] = (u_ref[...] * (w0_ref[...] * vn + b0_ref[...])).astype(o_ref.dtype)


def _gmlp_sample(ugv, ln_g, ln_b, w00, b0):
    m, two_w = ugv.shape
    gmw = two_w // 2
    est = 2 * (3 * _nbytes((m, gmw), F32) + _nbytes((m, gmw), BF16)) + 4 * _nbytes((m, gmw), F32)
    row = pl.BlockSpec((1, gmw), lambda i: (0, 0))
    return pl.pallas_call(
        _gmlp_sample_kernel,
        out_shape=[jax.ShapeDtypeStruct((m, gmw), BF16), jax.ShapeDtypeStruct((m, gmw), F32)],
        grid=(1,),
        in_specs=[pl.BlockSpec((m, gmw), lambda i: (0, 0)), pl.BlockSpec((m, gmw), lambda i: (0, 1)),
                  row, row, row, row],
        out_specs=[pl.BlockSpec((m, gmw), lambda i: (0, 0)), pl.BlockSpec((m, gmw), lambda i: (0, 0))],
        compiler_params=_params(("arbitrary",), est),
        name="gmlp_sample",
    )(ugv, ugv, ln_g.reshape(1, gmw), ln_b.reshape(1, gmw), w00, b0)


def _pack_bf16_pair(left, right):
    lb = pltpu.bitcast(left.astype(BF16).astype(F32), jnp.uint32)
    rb = pltpu.bitcast(right.astype(BF16).astype(F32), jnp.uint32)
    return lb | (rb >> 16)


def _unpack_bf16_pair(word):
    left = pltpu.bitcast(word & jnp.uint32(0xFFFF0000), F32)
    right = pltpu.bitcast(word << 16, F32)
    return left, right


def _split_bf16(x):
    hi = x.astype(BF16)
    lo = (x - hi.astype(F32)).astype(BF16)
    return hi, lo


def _ffn_norm_kernel(x_ref, g_ref, sh_ref, sc_ref, wr_ref, h_ref, hp_ref, lg_ref):
    h = _rms_mod(x_ref[...], g_ref[...], sc_ref[0], sh_ref[0])
    half = h.shape[1] // 2
    h_ref[...] = h.astype(BF16)
    hp_ref[...] = _pack_bf16_pair(h[:, :half], h[:, half:])
    h_hi, h_lo = _split_bf16(h)
    w_hi, w_lo = _split_bf16(wr_ref[...])
    dn = (((1,), (1,)), ((), ()))
    lg_ref[...] = lax.dot_general(w_hi, h_hi, dn, preferred_element_type=F32) \
        + lax.dot_general(w_hi, h_lo, dn, preferred_element_type=F32) \
        + lax.dot_general(w_lo, h_hi, dn, preferred_element_type=F32)


def _ffn_norm(x2, g, mod3, shift_chunk, scale_chunk, group_rows, w_router_t):
    m, d = x2.shape
    e = w_router_t.shape[0]
    tm = _pick_tile(min(m, group_rows) if mod3.shape[1] == 1 else m, 256, LANES)
    row_of = lambda i: i
    est = 2 * (3 * _nbytes((tm, d), F32) + 2 * _nbytes((tm, d), BF16) + _nbytes((e, d), F32)) \
        + 6 * _nbytes((tm, d), F32)
    return pl.pallas_call(
        _ffn_norm_kernel,
        out_shape=[jax.ShapeDtypeStruct((m, d), BF16),
                   jax.ShapeDtypeStruct((m, d // 2), jnp.uint32),
                   jax.ShapeDtypeStruct((e, m), F32)],
        grid=(m // tm,),
        in_specs=[pl.BlockSpec((tm, d), lambda i: (i, 0)),
                  pl.BlockSpec((1, d), lambda i: (0, 0)),
                  _mod_spec(mod3, tm, group_rows, d, row_of, lambda i: shift_chunk),
                  _mod_spec(mod3, tm, group_rows, d, row_of, lambda i: scale_chunk),
                  pl.BlockSpec((e, d), lambda i: (0, 0))],
        out_specs=[pl.BlockSpec((tm, d), lambda i: (i, 0)),
                   pl.BlockSpec((tm, d // 2), lambda i: (i, 0)),
                   pl.BlockSpec((e, tm), lambda i: (0, i))],
        compiler_params=_params(("arbitrary",), est),
        name="ffn_norm_router",
    )(x2, g.reshape(1, d), mod3, mod3, w_router_t)


def _first_max(vals, idx):
    mx = jnp.max(vals, axis=0, keepdims=True)
    first = jnp.min(jnp.where(vals == mx, idx, jnp.int32(2 ** 30)), axis=0, keepdims=True)
    return mx, first


def _route_kernel(lg_ref, bias_ref, eidx_ref, ew_ref, rank_ref, cnt_ref, base_scr,
                  *, n_groups, topk_groups, top_k, scale):
    e, tr = lg_ref.shape
    per = e // n_groups

    @pl.when(pl.program_id(0) == 0)
    def _():
        base_scr[...] = jnp.zeros_like(base_scr)

    scores = jax.nn.sigmoid(lg_ref[...])
    biased = scores + bias_ref[...]
    eid = lax.broadcasted_iota(jnp.int32, (e, tr), 0)
    neg = jnp.float32(-jnp.inf)

    grp_rows = []
    bid = lax.broadcasted_iota(jnp.int32, (per, tr), 0)
    for g in range(n_groups):
        blk = biased[g * per:(g + 1) * per]
        m1, i1 = _first_max(blk, bid)
        m2 = jnp.max(jnp.where(bid == i1, neg, blk), axis=0, keepdims=True)
        grp_rows.append(m1 + m2)
    grp = jnp.concatenate(grp_rows, axis=0)
    gid = lax.broadcasted_iota(jnp.int32, (n_groups, tr), 0)
    gsel = jnp.zeros((n_groups, tr), F32)
    work = grp
    for _ in range(topk_groups):
        _, gi = _first_max(work, gid)
        hit = gid == gi
        gsel = jnp.where(hit, 1.0, gsel)
        work = jnp.where(hit, neg, work)
    emask = jnp.concatenate(
        [jnp.broadcast_to(gsel[g:g + 1], (per, tr)) for g in range(n_groups)], axis=0)
    masked = jnp.where(emask > 0.0, biased, neg)

    onehots, idxs, wts = [], [], []
    for _ in range(top_k):
        _, ei = _first_max(masked, eid)
        hit = eid == ei
        onehots.append(hit)
        idxs.append(ei)
        wts.append(jnp.sum(jnp.where(hit, scores, 0.0), axis=0, keepdims=True))
        masked = jnp.where(hit, neg, masked)
    wsum = wts[0]
    for wk in wts[1:]:
        wsum = wsum + wk

    chosen = onehots[0]
    for oh in onehots[1:]:
        chosen = jnp.logical_or(chosen, oh)
    chosen_f = jnp.where(chosen, 1.0, 0.0)
    si = lax.broadcasted_iota(jnp.int32, (tr, tr), 0)
    ti = lax.broadcasted_iota(jnp.int32, (tr, tr), 1)
    upper = jnp.where(si < ti, 1.0, 0.0).astype(BF16)
    prefix = jnp.dot(chosen_f.astype(BF16), upper, preferred_element_type=F32)
    pos = prefix + base_scr[:, 0:1]
    for k in range(top_k):
        eidx_ref[k:k + 1, :] = idxs[k]
        ew_ref[k:k + 1, :] = wts[k] / wsum * scale
        rank_ref[k:k + 1, :] = jnp.sum(jnp.where(onehots[k], pos, 0.0), axis=0, keepdims=True).astype(jnp.int32)
    for k in range(top_k, eidx_ref.shape[0]):
        eidx_ref[k:k + 1, :] = jnp.zeros((1, tr), jnp.int32)
        ew_ref[k:k + 1, :] = jnp.zeros((1, tr), F32)
        rank_ref[k:k + 1, :] = jnp.zeros((1, tr), jnp.int32)
    base_scr[...] = base_scr[...] + jnp.sum(chosen_f, axis=1, keepdims=True)
    cnt_ref[...] = base_scr[...].astype(jnp.int32)


def _route(logits_t, router_bias):
    e, n = logits_t.shape
    tr = _pick_tile(n, 640, LANES)
    rows = SUBLANES
    assert TOP_K <= rows and e % N_EXPERT_GROUPS == 0
    kern = functools.partial(_route_kernel, n_groups=N_EXPERT_GROUPS, topk_groups=TOPK_GROUPS,
                             top_k=TOP_K, scale=ROUTED_SCALE)
    est = 2 * (_nbytes((e, tr), F32) + 3 * _nbytes((rows, tr), F32)) + 24 * _nbytes((e, tr), F32) \
        + 3 * _nbytes((tr, tr), F32)
    out_row = pl.BlockSpec((rows, tr), lambda i: (0, i))
    return pl.pallas_call(
        kern,
        out_shape=[jax.ShapeDtypeStruct((rows, n), jnp.int32),
                   jax.ShapeDtypeStruct((rows, n), F32),
                   jax.ShapeDtypeStruct((rows, n), jnp.int32),
                   jax.ShapeDtypeStruct((e, LANES), jnp.int32)],
        grid=(n // tr,),
        in_specs=[pl.BlockSpec((e, tr), lambda i: (0, i)),
                  pl.BlockSpec((e, 1), lambda i: (0, 0))],
        out_specs=[out_row, out_row, out_row, pl.BlockSpec((e, LANES), lambda i: (0, 0))],
        scratch_shapes=[pltpu.VMEM((e, LANES), F32)],
        compiler_params=_params(("arbitrary",), est),
        name="route",
    )(logits_t, router_bias.reshape(e, 1))


def _dispatch_kernel(dest_ref, fill_ref, nfill_ref, hp_ref, xs_ref, zero_scr, sem, *, top_k, n_tok):
    step = pl.program_id(0)
    tt = hp_ref.shape[0]
    fill_rows = zero_scr.shape[0]

    @pl.when(step == 0)
    def _():
        zero_scr[...] = jnp.zeros_like(zero_scr)
        n_fill = nfill_ref[0]

        def start_fill(i, c):
            row0 = pl.multiple_of(fill_ref[i] * fill_rows, fill_rows)
            pltpu.make_async_copy(zero_scr, xs_ref.at[pl.ds(row0, fill_rows)], sem).start()
            return c

        def wait_fill(i, c):
            pltpu.make_async_copy(zero_scr, xs_ref.at[pl.ds(0, fill_rows)], sem).wait()
            return c

        lax.fori_loop(0, n_fill, start_fill, 0)
        lax.fori_loop(0, n_fill, wait_fill, 0)

    base = step * tt

    def start(r, c):
        for k in range(top_k):
            pltpu.make_async_copy(hp_ref.at[pl.ds(r, 1)],
                                  xs_ref.at[pl.ds(dest_ref[k * n_tok + base + r], 1)], sem).start()
        return c

    def wait(r, c):
        for k in range(top_k):
            pltpu.make_async_copy(hp_ref.at[pl.ds(r, 1)], xs_ref.at[pl.ds(0, 1)], sem).wait()
        return c

    lax.fori_loop(0, tt, start, 0)
    lax.fori_loop(0, tt, wait, 0)


def _dispatch(dest_flat, fill_tiles, n_fill, hp, n_rows, fill_rows):
    n_tok, half = hp.shape
    tt = _pick_tile(n_tok, 128, SUBLANES)
    kern = functools.partial(_dispatch_kernel, top_k=TOP_K, n_tok=n_tok)
    est = 2 * _nbytes((tt, half), hp.dtype) + _nbytes((fill_rows, half), hp.dtype)
    return pl.pallas_call(
        kern,
        out_shape=jax.ShapeDtypeStruct((n_rows, half), hp.dtype),
        grid_spec=pltpu.PrefetchScalarGridSpec(
            num_scalar_prefetch=3,
            grid=(n_tok // tt,),
            in_specs=[pl.BlockSpec((tt, half), lambda i, dest, fill, nfill: (i, 0))],
            out_specs=pl.BlockSpec(memory_space=pl.ANY),
            scratch_shapes=[pltpu.VMEM((fill_rows, half), hp.dtype), pltpu.SemaphoreType.DMA]),
        compiler_params=_params(("arbitrary",), est),
        name="moe_dispatch",
    )(dest_flat, fill_tiles, n_fill, hp)


def _stream_expert_weights(tile_e_ref, next_e_ref, flags_ref, blk_ref, w_refs, col_of, stage, w_scr, sems):
    j = pl.program_id(0)
    t = pl.program_id(1)
    n_pass = pl.num_programs(0)
    width = stage.shape[-1]

    def copies(e, jj, slot):
        return [pltpu.make_async_copy(
            w_ref.at[e, :, pl.ds(pl.multiple_of(col(jj), width), width)], stage.at[slot, i], sems.at[slot])
            for i, (w_ref, col) in enumerate(zip(w_refs, col_of))]

    @pl.when(jnp.logical_and(j == 0, t == 0))
    def _():
        blk_ref[0] = 0
        for c in copies(tile_e_ref[0], 0, 0):
            c.start()

    slot = blk_ref[0] % 2
    for c in copies(tile_e_ref[t], j, slot):
        c.wait()
    in_pass = (flags_ref[t] & TILE_HAS_NEXT) > 0
    e_next = jnp.where(in_pass, next_e_ref[t], tile_e_ref[0])
    j_next = jnp.where(in_pass, j, j + 1)

    @pl.when(jnp.logical_or(in_pass, j + 1 < n_pass))
    def _():
        for c in copies(e_next, j_next, 1 - slot):
            c.start()

    for i, scr in enumerate(w_scr):
        scr[...] = stage[slot, i].astype(BF16)
    blk_ref[0] = blk_ref[0] + 1


def _expert_up_kernel(tile_e_ref, used_ref, next_e_ref, flags_ref, xs_ref, wg_ref, wu_ref, o_ref,
                      stage, wg_scr, wu_scr, sems, blk_ref):
    t = pl.program_id(1)
    n_used = used_ref[0]
    fc = o_ref.shape[1]

    @pl.when(t < n_used)
    def _():
        prev = tile_e_ref[jnp.maximum(t - 1, 0)]
        fresh = jnp.logical_or(t == 0, tile_e_ref[t] != prev)

        @pl.when(fresh)
        def _():
            col = lambda jj: jj * fc
            _stream_expert_weights(tile_e_ref, next_e_ref, flags_ref, blk_ref, [wg_ref, wu_ref], [col, col],
                                   stage, [wg_scr, wu_scr], sems)

        half = xs_ref.shape[1]
        tm = xs_ref.shape[0]

        def project(rows):
            xl, xr = _unpack_bf16_pair(xs_ref[:rows])
            xl = xl.astype(BF16)
            xr = xr.astype(BF16)
            gate = jnp.dot(xl, wg_scr[:half], preferred_element_type=F32) \
                + jnp.dot(xr, wg_scr[half:], preferred_element_type=F32)
            up = jnp.dot(xl, wu_scr[:half], preferred_element_type=F32) \
                + jnp.dot(xr, wu_scr[half:], preferred_element_type=F32)
            o_ref[:rows] = (_silu(gate) * up).astype(o_ref.dtype)
            if rows < tm:
                o_ref[rows:] = jnp.zeros((tm - rows, fc), o_ref.dtype)

        half_full = (flags_ref[t] & TILE_HALF_FULL) > 0

        @pl.when(half_full)
        def _():
            project(tm // 2)

        @pl.when(jnp.logical_not(half_full))
        def _():
            project(tm)

    @pl.when(t >= n_used)
    def _():
        o_ref[...] = jnp.zeros_like(o_ref)


def _expert_up(tile_e, n_used, next_e, tile_flags, xs, w_gate, w_up, tm, fc):
    p, half = xs.shape
    e, d, de = w_gate.shape
    n_t = p // tm
    clamp = lambda t, used: jnp.minimum(t, used[0] - 1)
    est = 2 * (_nbytes((tm, half), jnp.uint32) + 2 * _nbytes((d, fc), F32)
               + _nbytes((tm, fc), BF16)) + 2 * _nbytes((d, fc), BF16) + 2 * _nbytes((tm, d), BF16) \
        + 4 * _nbytes((tm, fc), F32) + 2 * _nbytes((tm, half), F32)
    return pl.pallas_call(
        _expert_up_kernel,
        out_shape=jax.ShapeDtypeStruct((p, de), BF16),
        grid_spec=pltpu.PrefetchScalarGridSpec(
            num_scalar_prefetch=4,
            grid=(de // fc, n_t),
            in_specs=[pl.BlockSpec((tm, half), lambda j, t, te, used, ne, hn: (clamp(t, used), 0)),
                      pl.BlockSpec(memory_space=pl.ANY),
                      pl.BlockSpec(memory_space=pl.ANY)],
            out_specs=pl.BlockSpec((tm, fc), lambda j, t, te, used, ne, hn: (t, j)),
            scratch_shapes=[pltpu.VMEM((2, 2, d, fc), F32), pltpu.VMEM((d, fc), BF16), pltpu.VMEM((d, fc), BF16),
                            pltpu.SemaphoreType.DMA((2,)), pltpu.SMEM((1,), jnp.int32)]),
        compiler_params=_params(("arbitrary", "arbitrary"), est),
        name="expert_up",
    )(tile_e, n_used, next_e, tile_flags, xs, w_gate, w_up)


def _expert_down_kernel(tile_e_ref, used_ref, next_e_ref, flags_ref, a_ref, wd_ref, o_ref,
                        stage, wl_scr, wr_scr, sems, blk_ref):
    t = pl.program_id(1)
    n_used = used_ref[0]
    nc = o_ref.shape[1]
    half = wd_ref.shape[2] // 2

    @pl.when(t < n_used)
    def _():
        prev = tile_e_ref[jnp.maximum(t - 1, 0)]
        fresh = jnp.logical_or(t == 0, tile_e_ref[t] != prev)

        @pl.when(fresh)
        def _():
            _stream_expert_weights(tile_e_ref, next_e_ref, flags_ref, blk_ref, [wd_ref, wd_ref],
                                   [lambda cc: cc * nc, lambda cc: half + cc * nc],
                                   stage, [wl_scr, wr_scr], sems)

        tm = a_ref.shape[0]

        def project(rows):
            a = a_ref[:rows]
            yl = jnp.dot(a, wl_scr[...], preferred_element_type=F32)
            yr = jnp.dot(a, wr_scr[...], preferred_element_type=F32)
            o_ref[:rows] = _pack_bf16_pair(yl, yr)
            if rows < tm:
                o_ref[rows:] = jnp.zeros((tm - rows, nc), o_ref.dtype)

        half_full = (flags_ref[t] & TILE_HALF_FULL) > 0

        @pl.when(half_full)
        def _():
            project(tm // 2)

        @pl.when(jnp.logical_not(half_full))
        def _():
            project(tm)

    @pl.when(t >= n_used)
    def _():
        o_ref[...] = jnp.zeros_like(o_ref)


def _expert_down(tile_e, n_used, next_e, tile_flags, act, w_down, tm, nc):
    p, de = act.shape
    e, _, d = w_down.shape
    half = d // 2
    n_t = p // tm
    n_c = half // nc
    clamp = lambda t, used: jnp.minimum(t, used[0] - 1)
    est = 2 * (_nbytes((tm, de), BF16) + 2 * _nbytes((de, nc), F32) + _nbytes((tm, nc), jnp.uint32)) \
        + 2 * _nbytes((de, nc), BF16) + 6 * _nbytes((tm, nc), F32)
    return pl.pallas_call(
        _expert_down_kernel,
        out_shape=jax.ShapeDtypeStruct((p, half), jnp.uint32),
        grid_spec=pltpu.PrefetchScalarGridSpec(
            num_scalar_prefetch=4,
            grid=(n_c, n_t),
            in_specs=[pl.BlockSpec((tm, de), lambda c, t, te, used, ne, hn: (clamp(t, used), 0)),
                      pl.BlockSpec(memory_space=pl.ANY)],
            out_specs=pl.BlockSpec((tm, nc), lambda c, t, te, used, ne, hn: (t, c)),
            scratch_shapes=[pltpu.VMEM((2, 2, de, nc), F32), pltpu.VMEM((de, nc), BF16), pltpu.VMEM((de, nc), BF16),
                            pltpu.SemaphoreType.DMA((2,)), pltpu.SMEM((1,), jnp.int32)]),
        compiler_params=_params(("arbitrary", "arbitrary"), est),
        name="expert_down",
    )(tile_e, n_used, next_e, tile_flags, act, w_down)


def _combine_kernel(dest_ref, ys_ref, x_ref, sh_ref, gf_ref, ew_ref, o_ref, buf, sem, *, top_k, n_tok, tok0):
    tt = x_ref.shape[0]
    half = ys_ref.shape[1]
    step = pl.program_id(0)
    slot = step % 2

    def gather(for_step, into, start):
        base = tok0 + for_step * tt

        def body(r, c):
            for k in range(top_k):
                row = dest_ref[k * n_tok + base + r] if start else 0
                cp = pltpu.make_async_copy(ys_ref.at[pl.ds(row, 1)], buf.at[into, k, pl.ds(r, 1)], sem.at[into])
                if start:
                    cp.start()
                else:
                    cp.wait()
            return c

        lax.fori_loop(0, tt, body, 0)

    @pl.when(step == 0)
    def _():
        gather(0, 0, True)

    @pl.when(step + 1 < pl.num_programs(0))
    def _():
        gather(step + 1, 1 - slot, True)

    gather(step, slot, False)
    ew = ew_ref[...]
    left, right = _unpack_bf16_pair(buf[slot, 0])
    left = left * ew[:, 0:1]
    right = right * ew[:, 0:1]
    for k in range(1, top_k):
        l2, r2 = _unpack_bf16_pair(buf[slot, k])
        left = left + l2 * ew[:, k:k + 1]
        right = right + r2 * ew[:, k:k + 1]
    gf = gf_ref[0]
    o_ref[:, :half] = x_ref[:, :half] + gf[:, :half] * (left + sh_ref[:, :half])
    o_ref[:, half:] = x_ref[:, half:] + gf[:, half:] * (right + sh_ref[:, half:])


def _combine(dest_flat, ys, x2, shared, mod3, ew_t, gate_chunk, group_rows, n_tok, tok0):
    m, d = x2.shape
    half = d // 2
    ew_rows = ew_t.shape[1]
    tt = _pick_tile(min(m, group_rows) if mod3.shape[1] == 1 else m, 128, SUBLANES)
    sh_blk0 = tok0 // tt
    assert tok0 % tt == 0
    row_of = lambda i, dest: i
    kern = functools.partial(_combine_kernel, top_k=TOP_K, n_tok=n_tok, tok0=tok0)
    est = 2 * (3 * _nbytes((tt, d), F32) + _nbytes((tt, d), F32)) + 2 * _nbytes((TOP_K, tt, half), jnp.uint32) \
        + 6 * _nbytes((tt, d), F32)
    return pl.pallas_call(
        kern,
        out_shape=jax.ShapeDtypeStruct((m, d), F32),
        grid_spec=pltpu.PrefetchScalarGridSpec(
            num_scalar_prefetch=1,
            grid=(m // tt,),
            in_specs=[pl.BlockSpec(memory_space=pl.ANY),
                      pl.BlockSpec((tt, d), lambda i, dest: (i, 0)),
                      pl.BlockSpec((tt, d), lambda i, dest: (i + sh_blk0, 0)),
                      _mod_spec(mod3, tt, group_rows, d, row_of, lambda i, dest: gate_chunk),
                      pl.BlockSpec((tt, ew_rows), lambda i, dest: (i + sh_blk0, 0))],
            out_specs=pl.BlockSpec((tt, d), lambda i, dest: (i, 0)),
            scratch_shapes=[pltpu.VMEM((2, TOP_K, tt, half), jnp.uint32), pltpu.SemaphoreType.DMA((2,))]),
        compiler_params=_params(("arbitrary",), est),
        name="moe_combine",
    )(dest_flat, ys, x2, shared, mod3, ew_t)


def _token_mixing(x2, mod3, group_rows, p, attend):
    m, d = x2.shape
    aw, kvw, gmw, hd = p["aw"], p["kvw"], p["gmw"], p["hd"]
    tn = 512
    assert aw % tn == 0 and (2 * kvw) % tn == 0 and gmw % tn == 0 and d % tn == 0
    tm = _pick_tile(min(m, group_rows) if mod3.shape[1] == 1 else m, 1024, 16)
    h = _modulate(x2, p["norm_mix_g"], mod3, 0, 1, group_rows)
    w_in = p["w_in"]

    qkv_w = aw + 2 * kvw

    def qkv_epilogue(accs, extras, n, mi):
        gain_ref, flag_ref, bd_ref = extras
        z = accs[0]
        sq_hi, sq_lo = _split_bf16(z * z)
        ss = jnp.dot(sq_hi, bd_ref[...], preferred_element_type=F32) \
            + jnp.dot(sq_lo, bd_ref[...], preferred_element_type=F32)
        inv = lax.rsqrt(ss * (1.0 / hd) + EPS)
        return [z * jnp.where(flag_ref[...] > 0.0, inv, 1.0) * gain_ref[...]]

    qkv = _mm([(h, d, 0)], [(w_in, 0)],
              [(p["qkv_gain"], pl.BlockSpec((1, tn), lambda n, mi: (0, n))),
               (p["qkv_flag"], pl.BlockSpec((1, tn), lambda n, mi: (0, n))),
               (p["head_ones"], pl.BlockSpec((tn, tn), lambda n, mi: (0, 0)))],
              qkv_epilogue,
              [(jax.ShapeDtypeStruct((m, qkv_w), F32), pl.BlockSpec((tm, tn), lambda n, mi: (mi, n)))],
              m=m, tm=tm, tn=tn, n_tiles=qkv_w // tn, pairs=[(0, 0)], name="in_proj_qkv")[0]

    ugv_dtype = BF16 if mod3.shape[1] == 1 else F32
    ugv = _mm([(h, d, 0)], [(w_in, qkv_w // tn)], [],
              lambda accs, extras, n, mi: [_gelu(accs[0])],
              [(jax.ShapeDtypeStruct((m, 2 * gmw), ugv_dtype), pl.BlockSpec((tm, tn), lambda n, mi: (mi, n)))],
              m=m, tm=tm, tn=tn, n_tiles=2 * gmw // tn, pairs=[(0, 0)], name="in_proj_gmlp")[0]

    gates = _mm([(h, d, 0)], [(w_in, (qkv_w + 2 * gmw) // tn)], [],
                lambda accs, extras, n, mi: [jax.nn.sigmoid(accs[0])],
                [(jax.ShapeDtypeStruct((m, 2 * d), BF16), pl.BlockSpec((tm, tn), lambda n, mi: (mi, n)))],
                m=m, tm=tm, tn=tn, n_tiles=2 * d // tn, pairs=[(0, 0)], name="in_proj_gates")[0]

    o_attn, o_gmlp, aux = attend(qkv, ugv)

    nd = d // tn
    merged = _mm([(o_attn, aw, 0), (o_gmlp, gmw, 0)], [(p["w_branch_attn"], 0), (p["w_branch_gmlp"], 0)],
                 [(gates, pl.BlockSpec((tm, tn), lambda n, mi: (mi, n))),
                  (gates, pl.BlockSpec((tm, tn), lambda n, mi: (mi, n + nd)))],
                 lambda accs, extras, n, mi: [extras[0][...].astype(F32) * accs[0]
                                              + extras[1][...].astype(F32) * accs[1]],
                 [(jax.ShapeDtypeStruct((m, d), BF16), pl.BlockSpec((tm, tn), lambda n, mi: (mi, n)))],
                 m=m, tm=tm, tn=tn, n_tiles=nd, pairs=[(0, 0), (1, 1)], name="branch_merge")[0]

    x1 = _mm([(merged, d, 0)], [(p["w_out"], 0)],
             [(x2, pl.BlockSpec((tm, tn), lambda n, mi: (mi, n))),
              (mod3, _mod_spec(mod3, tm, group_rows, tn, lambda n, mi: mi, lambda n, mi: 2 * nd + n))],
             lambda accs, extras, n, mi: [extras[0][...] + extras[1][0] * accs[0]],
             [(jax.ShapeDtypeStruct((m, d), F32), pl.BlockSpec((tm, tn), lambda n, mi: (mi, n)))],
             m=m, tm=tm, tn=tn, n_tiles=nd, pairs=[(0, 0)], name="out_proj")[0]
    return x1, qkv, aux


def kernel(x_prompt, x_sample, cache_k_win, cache_v_win, c_prompt, c_sample, norm_mix_g, norm_ffn_g,
           w_ada, b_ada, w_in, q_norm_g, k_norm_g, attn_sinks, gm_ln_g, gm_ln_b, w_spatial, b_spatial,
           w_branch_attn, w_branch_gmlp, w_out, w_router, router_bias, w_gate, w_up, w_down,
           ws_gate, ws_up, ws_down):
    depth = norm_mix_g.shape[0]
    assert depth == 1, "single-layer step"
    batch, seq, d = x_prompt.shape
    db, t_new, _ = x_sample.shape
    assert t_new == 1, "one new token per sequence"
    _, _, win, n_kv, hd = cache_k_win.shape
    n_grp = attn_sinks.shape[-1]
    aw, kvw = n_kv * n_grp * hd, n_kv * hd
    gmw = gm_ln_g.shape[-1]
    n_groups, ch, _ = w_spatial.shape[1:]
    n_exp = w_router.shape[-1]
    de = w_gate.shape[-1]
    ds = ws_gate.shape[-1]
    tn = 512
    l = 0

    n_c = batch + db
    rows = -(-n_c // 16) * 16
    c_all = jnp.concatenate([c_prompt, c_sample, jnp.zeros((rows - n_c, d), F32)], axis=0)
    mods = _adaln(c_all, w_ada[l], b_ada[l])
    mod_p = mods[:batch].reshape(batch, 1, 6 * d)
    mod_s = mods[batch:n_c].reshape(1, db, 6 * d)

    gq = jnp.tile(q_norm_g[l], aw // hd)
    gk = jnp.tile(k_norm_g[l], kvw // hd)
    qkv_gain = jnp.concatenate([gq, gk, jnp.ones((kvw,), F32)]).reshape(1, -1)
    qkv_flag = jnp.concatenate([jnp.ones((aw + kvw,), F32), jnp.zeros((kvw,), F32)]).reshape(1, -1)
    hid = jnp.arange(tn) // hd
    head_ones = (hid[:, None] == hid[None, :]).astype(BF16)
    params = dict(aw=aw, kvw=kvw, gmw=gmw, hd=hd, norm_mix_g=norm_mix_g[l], w_in=w_in[l],
                  qkv_gain=qkv_gain, qkv_flag=qkv_flag, head_ones=head_ones,
                  w_branch_attn=w_branch_attn[l], w_branch_gmlp=w_branch_gmlp[l], w_out=w_out[l])
    sinks = attn_sinks[l]

    def attend_prompt(qkv, ugv):
        o_attn = _attn_prompt(qkv, sinks, batch, seq, n_kv, n_grp, hd)
        o_gmlp = _gmlp_prompt(ugv, gm_ln_g[l], gm_ln_b[l], w_spatial[l], b_spatial[l].T)
        return o_attn, o_gmlp, None

    xp2 = x_prompt.reshape(batch * seq, d)
    x1_p, qkv_p, _ = _token_mixing(xp2, mod_p, seq, params, attend_prompt)

    ck = cache_k_win[l].reshape(db, win, kvw)
    cv = cache_v_win[l].reshape(db, win, kvw)

    def attend_sample(qkv, ugv):
        q3 = qkv[:, :aw].reshape(db, n_kv * n_grp, hd)
        o3, new_k, new_v = _attn_sample(q3, qkv, ck, cv, sinks.reshape(-1, 1), n_kv, n_grp, hd)
        w00 = jnp.repeat(w_spatial[l][:, 0, 0], gmw // n_groups).reshape(1, gmw)
        b0 = jnp.repeat(b_spatial[l][:, 0], gmw // n_groups).reshape(1, gmw)
        o_gmlp, vn = _gmlp_sample(ugv, gm_ln_g[l], gm_ln_b[l], w00, b0)
        return o3.reshape(db, aw), o_gmlp, (new_k, new_v, vn)

    xs2 = x_sample.reshape(db, d)
    x1_s, _, (new_k_s, new_v_s, vn_s) = _token_mixing(xs2, mod_s, 1, params, attend_sample)

    w_router_t = w_router[l].T
    h2_p, hp_p, lg_p = _ffn_norm(x1_p, norm_ffn_g[l], mod_p, 3, 4, seq, w_router_t)
    h2_s, hp_s, lg_s = _ffn_norm(x1_s, norm_ffn_g[l], mod_s, 3, 4, 1, w_router_t)
    h2 = jnp.concatenate([h2_p, h2_s], axis=0)
    hp = jnp.concatenate([hp_p, hp_s], axis=0)
    logits_t = jnp.concatenate([lg_p, lg_s], axis=1)
    n_tok = h2.shape[0]

    eidx8, ew8, rank8, counts = _route(logits_t, router_bias[l])
    eidx, ew, rank = eidx8[:TOP_K], ew8[:TOP_K], rank8[:TOP_K]
    counts = counts[:, 0]

    tm_e = 256
    n_assign = n_tok * TOP_K
    n_tiles = -(-n_assign // tm_e) + n_exp
    n_rows = n_tiles * tm_e
    padded = (counts + tm_e - 1) // tm_e * tm_e
    pad_end = jnp.cumsum(padded)
    pad_start = pad_end - padded
    expert_ids = jnp.arange(n_exp, dtype=jnp.int32)
    dest = rank + jnp.sum(jnp.where(eidx[:, :, None] == expert_ids, pad_start.astype(jnp.int32), 0), axis=-1)
    dest_flat = dest.reshape(-1)
    tile_ids = jnp.arange(n_tiles, dtype=jnp.int32)
    tile_e = jnp.minimum(jnp.sum((pad_end[None, :] <= tile_ids[:, None] * tm_e).astype(jnp.int32), axis=1),
                         n_exp - 1).astype(jnp.int32)
    used_tiles = (pad_end[-1:] // tm_e).astype(jnp.int32)
    pad_fill = jnp.maximum(pad_end // tm_e - 1, 0).astype(jnp.int32)
    tail_fill = jnp.minimum(used_tiles + tile_ids, n_tiles - 1)
    fill_tiles = jnp.concatenate([pad_fill, tail_fill])
    n_fill = (n_exp + n_tiles - used_tiles).astype(jnp.int32)

    seg_end_tile = (pad_end // tm_e).astype(jnp.int32)[tile_e]
    next_e = tile_e[jnp.minimum(seg_end_tile, n_tiles - 1)]
    rows_in_tile = counts[tile_e] - (tile_ids - (pad_start // tm_e).astype(jnp.int32)[tile_e]) * tm_e
    tile_flags = jnp.where(seg_end_tile < used_tiles, TILE_HAS_NEXT, 0) \
        + jnp.where(rows_in_tile <= tm_e // 2, TILE_HALF_FULL, 0)
    tile_flags = tile_flags.astype(jnp.int32)

    xs = _dispatch(dest_flat, fill_tiles, n_fill, hp, n_rows, tm_e)
    act = _expert_up(tile_e, used_tiles, next_e, tile_flags, xs, w_gate[l], w_up[l], tm_e,
                     _pick_tile(de, 512, LANES))
    ys = _expert_down(tile_e, used_tiles, next_e, tile_flags, act, w_down[l], tm_e,
                      _pick_tile(d // 2, 1024, LANES))

    tm_s = _pick_tile(n_tok, 640, 16)
    tn_s = _pick_tile(ds, 256, LANES)
    sh_act = _mm([(h2, d, 0)], [(ws_gate[l], 0), (ws_up[l], 0)], [],
                 lambda accs, extras, n, mi: [_silu(accs[0]) * accs[1]],
                 [(jax.ShapeDtypeStruct((n_tok, ds), BF16), pl.BlockSpec((tm_s, tn_s), lambda n, mi: (mi, n)))],
                 m=n_tok, tm=tm_s, tn=tn_s, n_tiles=ds // tn_s, pairs=[(0, 0), (0, 1)], name="shared_up")[0]
    shared = _mm([(sh_act, ds, 0)], [(ws_down[l], 0)], [],
                 lambda accs, extras, n, mi: [accs[0]],
                 [(jax.ShapeDtypeStruct((n_tok, d), F32), pl.BlockSpec((tm_s, tn), lambda n, mi: (mi, n)))],
                 m=n_tok, tm=tm_s, tn=tn, n_tiles=d // tn, pairs=[(0, 0)], name="shared_down")[0]

    ew_t = ew8.T
    y_p = _combine(dest_flat, ys, x1_p, shared, mod_p, ew_t, 5, seq, n_tok, 0)
    y_s = _combine(dest_flat, ys, x1_s, shared, mod_s, ew_t, 5, 1, n_tok, batch * seq)

    w_keep = min(WINDOW, seq)
    qkv_p3 = qkv_p.reshape(batch, seq, aw + 2 * kvw)
    new_k_p = qkv_p3[:, seq - w_keep:, aw:aw + kvw].reshape(1, batch, w_keep, n_kv, hd)
    new_v_p = qkv_p3[:, seq - w_keep:, aw + kvw:].reshape(1, batch, w_keep, n_kv, hd)
    return (y_p.reshape(batch, seq, d), y_s.reshape(db, 1, d), new_k_p, new_v_p,
            new_k_s.reshape(1, db, win, n_kv, hd), new_v_s.reshape(1, db, win, n_kv, hd),
            vn_s.reshape(1, db, 1, gmw))
```

```python
import functools

import jax
import jax.numpy as jnp
from jax import lax
from jax.experimental import pallas as pl
from jax.experimental.pallas import tpu as pltpu

TOP_K = 6
N_EXPERT_GROUPS = 8
TOPK_GROUPS = 4
ROUTED_SCALE = 2.5
WINDOW = 128
EPS = 1e-6

TILE_HAS_NEXT = 1
TILE_HALF_FULL = 2

V7X_VMEM_BYTES = 64 * 1024 * 1024
V7X_VMEM_REQUEST_CAP = 56 * 1024 * 1024
LANES = 128
SUBLANES = 8

BF16 = jnp.bfloat16
F32 = jnp.float32


def _pick_tile(total, preferred, multiple):
    t = min(preferred, total)
    t -= t % multiple
    while t > multiple and total % t:
        t -= multiple
    assert t > 0 and total % t == 0, (total, preferred, multiple)
    return t


def _params(semantics, vmem_bytes):
    return pltpu.CompilerParams(
        dimension_semantics=semantics,
        vmem_limit_bytes=int(min(V7X_VMEM_REQUEST_CAP, max(vmem_bytes, 16 * 1024 * 1024))))


def _nbytes(shape, dtype):
    n = 1
    for s in shape:
        n *= s
    return n * jnp.dtype(dtype).itemsize


def _mod_spec(mod3, tm, group_rows, width, row_of, col_of):
    if mod3.shape[1] == 1:
        return pl.BlockSpec((1, 1, width), lambda *g: ((row_of(*g) * tm) // group_rows, 0, col_of(*g)))
    return pl.BlockSpec((1, tm, width), lambda *g: (0, row_of(*g), col_of(*g)))


def _adaln_kernel(c_ref, w_ref, b_ref, o_ref, a_scr):
    @pl.when(pl.program_id(0) == 0)
    def _():
        c = c_ref[...]
        a_scr[...] = (c * jax.nn.sigmoid(c)).astype(BF16)

    acc = jnp.dot(a_scr[...], w_ref[...].astype(BF16), preferred_element_type=F32)
    o_ref[...] = acc + b_ref[...]


def _adaln(c_all, w_ada, b_ada):
    rows, d = c_all.shape
    n_out = w_ada.shape[1]
    tn = _pick_tile(n_out, 512, LANES)
    est = 2 * (_nbytes((d, tn), F32) + _nbytes((rows, tn), F32)) + _nbytes((rows, d), F32) * 2 \
        + _nbytes((rows, d), BF16) + _nbytes((d, tn), BF16) + _nbytes((rows, tn), F32)
    return pl.pallas_call(
        _adaln_kernel,
        out_shape=jax.ShapeDtypeStruct((rows, n_out), F32),
        grid=(n_out // tn,),
        in_specs=[pl.BlockSpec((rows, d), lambda n: (0, 0)),
                  pl.BlockSpec((d, tn), lambda n: (0, n)),
                  pl.BlockSpec((1, tn), lambda n: (0, n))],
        out_specs=pl.BlockSpec((rows, tn), lambda n: (0, n)),
        scratch_shapes=[pltpu.VMEM((rows, d), BF16)],
        compiler_params=_params(("arbitrary",), est),
        name="adaln",
    )(c_all, w_ada, b_ada.reshape(1, n_out))


def _rms_mod(x, g, scale, shift):
    y = x * lax.rsqrt(jnp.mean(x * x, axis=-1, keepdims=True) + EPS)
    return (y * g) * (1.0 + scale) + shift


def _modulate_kernel(x_ref, g_ref, sh_ref, sc_ref, o_ref):
    o_ref[...] = _rms_mod(x_ref[...], g_ref[...], sc_ref[0], sh_ref[0]).astype(o_ref.dtype)


def _modulate(x2, g, mod3, shift_chunk, scale_chunk, group_rows):
    m, d = x2.shape
    tm = _pick_tile(min(m, group_rows) if mod3.shape[1] == 1 else m, 256, SUBLANES)
    row_of = lambda i: i
    est = 2 * (_nbytes((tm, d), F32) * 3 + _nbytes((tm, d), BF16)) + 4 * _nbytes((tm, d), F32)
    return pl.pallas_call(
        _modulate_kernel,
        out_shape=jax.ShapeDtypeStruct((m, d), BF16),
        grid=(m // tm,),
        in_specs=[pl.BlockSpec((tm, d), lambda i: (i, 0)),
                  pl.BlockSpec((1, d), lambda i: (0, 0)),
                  _mod_spec(mod3, tm, group_rows, d, row_of, lambda i: shift_chunk),
                  _mod_spec(mod3, tm, group_rows, d, row_of, lambda i: scale_chunk)],
        out_specs=pl.BlockSpec((tm, d), lambda i: (i, 0)),
        compiler_params=_params(("arbitrary",), est),
        name="modulate",
    )(x2, g.reshape(1, d), mod3, mod3)


def _mm_kernel(*refs, n_a, n_w, n_extra, n_out, pairs, epilogue):
    a_refs = refs[:n_a]
    w_refs = refs[n_a:n_a + n_w]
    extra_refs = refs[n_a + n_w:n_a + n_w + n_extra]
    out_refs = refs[n_a + n_w + n_extra:n_a + n_w + n_extra + n_out]
    w_scr = refs[n_a + n_w + n_extra + n_out:]
    n = pl.program_id(0)
    mi = pl.program_id(1)

    @pl.when(mi == 0)
    def _():
        for w_ref, scr in zip(w_refs, w_scr):
            scr[...] = w_ref[...].astype(BF16)

    accs = [jnp.dot(a_refs[ia][...], w_scr[iw][...], preferred_element_type=F32) for ia, iw in pairs]
    outs = epilogue(accs, extra_refs, n, mi)
    for o_ref, val in zip(out_refs, outs):
        o_ref[...] = val.astype(o_ref.dtype)


def _mm(a_list, w_list, extras, epilogue, outs, *, m, tm, tn, n_tiles, pairs, name):
    in_specs, args, est = [], [], 0
    for arr, kw, cb in a_list:
        in_specs.append(pl.BlockSpec((tm, kw), lambda n, mi, cb=cb: (mi, cb)))
        args.append(arr)
        est += 2 * _nbytes((tm, kw), arr.dtype)
    scratch = []
    for arr, off in w_list:
        k = arr.shape[0]
        in_specs.append(pl.BlockSpec((k, tn), lambda n, mi, off=off: (0, n + off)))
        args.append(arr)
        scratch.append(pltpu.VMEM((k, tn), BF16))
        est += 2 * _nbytes((k, tn), F32) + _nbytes((k, tn), BF16)
    for arr, spec in extras:
        in_specs.append(spec)
        args.append(arr)
        est += 2 * _nbytes(spec.block_shape, arr.dtype)
    for sds, spec in outs:
        est += 2 * _nbytes(spec.block_shape, sds.dtype)
    est += (len(pairs) + 2) * _nbytes((tm, tn), F32)
    kern = functools.partial(_mm_kernel, n_a=len(a_list), n_w=len(w_list), n_extra=len(extras),
                             n_out=len(outs), pairs=tuple(pairs), epilogue=epilogue)
    res = pl.pallas_call(
        kern,
        out_shape=[sds for sds, _ in outs],
        grid=(n_tiles, m // tm),
        in_specs=in_specs,
        out_specs=[spec for _, spec in outs],
        scratch_shapes=scratch,
        compiler_params=_params(("arbitrary", "arbitrary"), est),
        name=name,
    )(*args)
    return res


def _gelu(x):
    return x * (lax.erf(x * (2.0 ** -0.5)) + 1.0) * 0.5


def _silu(x):
    return x * jax.nn.sigmoid(x)


def _attn_prompt_kernel(sinks_ref, q_ref, kc_ref, kp_ref, vc_ref, vp_ref, o_ref, *, n_kv, n_grp, hd):
    i = pl.program_id(1)
    w = q_ref.shape[0]
    scale = hd ** -0.5
    rows = n_grp * w
    qi = lax.broadcasted_iota(jnp.int32, (rows, w), 0) % w
    sj = lax.broadcasted_iota(jnp.int32, (rows, w), 1)
    mask_c = sj <= qi
    mask_p = jnp.logical_and(sj >= qi, i > 0)
    ones = jnp.ones((w, w), BF16)
    dn = (((1,), (1,)), ((), ()))
    for kv in range(n_kv):
        ks = slice(kv * hd, (kv + 1) * hd)
        kc = kc_ref[:, ks].astype(BF16)
        kp = kp_ref[:, ks].astype(BF16)
        vc = vc_ref[:, ks].astype(BF16)
        vp = vp_ref[:, ks].astype(BF16)
        cols = [(kv * n_grp + g) * hd for g in range(n_grp)]
        q = jnp.concatenate([q_ref[:, c:c + hd] for c in cols], axis=0).astype(BF16)
        sink = jnp.concatenate([jnp.full((w, 1), sinks_ref[kv, g], F32) for g in range(n_grp)], axis=0)
        lc = jnp.where(mask_c, lax.dot_general(q, kc, dn, preferred_element_type=F32) * scale, -jnp.inf)
        lp = jnp.where(mask_p, lax.dot_general(q, kp, dn, preferred_element_type=F32) * scale, -jnp.inf)
        mx = jnp.maximum(jnp.max(jnp.maximum(lc, lp), axis=-1, keepdims=True), sink)
        pc = jnp.exp(lc - mx).astype(BF16)
        pp = jnp.exp(lp - mx).astype(BF16)
        den = jnp.dot(pc, ones, preferred_element_type=F32) + jnp.dot(pp, ones, preferred_element_type=F32)
        den = den[:, :hd] + jnp.exp(sink - mx)
        o = jnp.dot(pc, vc, preferred_element_type=F32) + jnp.dot(pp, vp, preferred_element_type=F32)
        o = (o / den).astype(o_ref.dtype)
        for g, c in enumerate(cols):
            o_ref[:, c:c + hd] = o[g * w:(g + 1) * w]


def _attn_prompt(qkv, sinks, batch, seq, n_kv, n_grp, hd):
    aw = n_kv * n_grp * hd
    kvw = n_kv * hd
    assert seq % WINDOW == 0 and aw % kvw == 0
    nb = seq // WINDOW
    kcol = aw // kvw
    est = 2 * (_nbytes((WINDOW, aw), F32) + 4 * _nbytes((WINDOW, kvw), F32) + _nbytes((WINDOW, aw), BF16)) \
        + 16 * _nbytes((WINDOW, WINDOW), F32)
    kern = functools.partial(_attn_prompt_kernel, n_kv=n_kv, n_grp=n_grp, hd=hd)
    return pl.pallas_call(
        kern,
        out_shape=jax.ShapeDtypeStruct((batch * seq, aw), BF16),
        grid=(batch, nb),
        in_specs=[pl.BlockSpec(memory_space=pltpu.SMEM),
                  pl.BlockSpec((WINDOW, aw), lambda b, i: (b * nb + i, 0)),
                  pl.BlockSpec((WINDOW, kvw), lambda b, i: (b * nb + i, kcol)),
                  pl.BlockSpec((WINDOW, kvw), lambda b, i: (b * nb + jnp.maximum(i - 1, 0), kcol)),
                  pl.BlockSpec((WINDOW, kvw), lambda b, i: (b * nb + i, kcol + 1)),
                  pl.BlockSpec((WINDOW, kvw), lambda b, i: (b * nb + jnp.maximum(i - 1, 0), kcol + 1))],
        out_specs=pl.BlockSpec((WINDOW, aw), lambda b, i: (b * nb + i, 0)),
        compiler_params=_params(("arbitrary", "arbitrary"), est),
        name="attn_prompt",
    )(sinks, qkv, qkv, qkv, qkv, qkv)


def _attn_sample_kernel(q_ref, kn_ref, vn_ref, ck_ref, cv_ref, sinks_ref, o_ref, ok_ref, ov_ref,
                        *, n_kv, n_grp, hd):
    bs, w, kvw = ck_ref.shape
    nh = n_kv * n_grp
    scale = hd ** -0.5
    head_kv = lax.broadcasted_iota(jnp.int32, (nh, kvw), 0) // n_grp
    lane_kv = lax.broadcasted_iota(jnp.int32, (nh, kvw), 1) // hd
    own = head_kv == lane_kv
    row = lax.broadcasted_iota(jnp.int32, (w, kvw), 0)
    sinks = sinks_ref[...]
    dn = (((1,), (1,)), ((), ()))
    for b in range(bs):
        q = q_ref[b]
        qbd = jnp.where(own, jnp.concatenate([q] * n_kv, axis=-1), 0.0)
        ck = ck_ref[b]
        cv = cv_ref[b]
        kn = kn_ref[pl.ds(b, 1), :]
        vn = vn_ref[pl.ds(b, 1), :]
        logits = lax.dot_general(qbd.astype(BF16), ck.astype(BF16), dn,
                                 preferred_element_type=F32) * scale
        l_new = jnp.sum(qbd.astype(BF16).astype(F32) * kn.astype(BF16).astype(F32),
                        axis=-1, keepdims=True) * scale
        mx = jnp.maximum(jnp.maximum(jnp.max(logits, axis=-1, keepdims=True), l_new), sinks)
        p = jnp.exp(logits - mx)
        p_new = jnp.exp(l_new - mx)
        den = jnp.sum(p, axis=-1, keepdims=True) + p_new + jnp.exp(sinks - mx)
        r = jnp.dot(p.astype(BF16), cv.astype(BF16), preferred_element_type=F32) \
            + p_new.astype(BF16).astype(F32) * vn.astype(BF16).astype(F32)
        r = jnp.where(own, r / den, 0.0)
        o = r[:, 0:hd]
        for kv in range(1, n_kv):
            o = o + r[:, kv * hd:(kv + 1) * hd]
        o_ref[b] = o.astype(o_ref.dtype)
        ok_ref[b] = jnp.where(row == w - 1, kn, pltpu.roll(ck, shift=w - 1, axis=0))
        ov_ref[b] = jnp.where(row == w - 1, vn, pltpu.roll(cv, shift=w - 1, axis=0))


def _attn_sample(q3, qkv_s, cache_k, cache_v, sinks_col, n_kv, n_grp, hd):
    db, w, kvw = cache_k.shape
    nh = n_kv * n_grp
    aw = nh * hd
    kcol = aw // kvw
    bs = _pick_tile(db, 8, SUBLANES)
    est = 2 * (4 * _nbytes((bs, w, kvw), F32) + 2 * _nbytes((bs, kvw), F32) + 2 * _nbytes((bs, nh, LANES), F32)) \
        + 16 * _nbytes((w, kvw), F32)
    kern = functools.partial(_attn_sample_kernel, n_kv=n_kv, n_grp=n_grp, hd=hd)
    return pl.pallas_call(
        kern,
        out_shape=[jax.ShapeDtypeStruct((db, nh, hd), BF16),
                   jax.ShapeDtypeStruct((db, w, kvw), F32),
                   jax.ShapeDtypeStruct((db, w, kvw), F32)],
        grid=(db // bs,),
        in_specs=[pl.BlockSpec((bs, nh, hd), lambda i: (i, 0, 0)),
                  pl.BlockSpec((bs, kvw), lambda i: (i, kcol)),
                  pl.BlockSpec((bs, kvw), lambda i: (i, kcol + 1)),
                  pl.BlockSpec((bs, w, kvw), lambda i: (i, 0, 0)),
                  pl.BlockSpec((bs, w, kvw), lambda i: (i, 0, 0)),
                  pl.BlockSpec((nh, 1), lambda i: (0, 0))],
        out_specs=[pl.BlockSpec((bs, nh, hd), lambda i: (i, 0, 0)),
                   pl.BlockSpec((bs, w, kvw), lambda i: (i, 0, 0)),
                   pl.BlockSpec((bs, w, kvw), lambda i: (i, 0, 0))],
        compiler_params=_params(("arbitrary",), est),
        name="attn_sample",
    )(q3, qkv_s, qkv_s, cache_k, cache_v, sinks_col)


def _layer_norm(v, g, b):
    mu = jnp.mean(v, axis=-1, keepdims=True)
    c = v - mu
    var = jnp.mean(c * c, axis=-1, keepdims=True)
    return c * lax.rsqrt(var + EPS) * g + b


def _gmlp_prompt_kernel(u_ref, v_ref, g_ref, b_ref, ws_ref, bs_ref, o_ref, *, n_groups):
    ch = u_ref.shape[0]
    gd = u_ref.shape[1] // n_groups
    vn = _layer_norm(v_ref[...].astype(F32), g_ref[...], b_ref[...])
    ti = lax.broadcasted_iota(jnp.int32, (ch, ch), 0)
    si = lax.broadcasted_iota(jnp.int32, (ch, ch), 1)
    causal = si <= ti
    for g in range(n_groups):
        cs = slice(g * gd, (g + 1) * gd)
        wc = jnp.where(causal, ws_ref[g], 0.0).astype(BF16)
        mixed = jnp.dot(wc, vn[:, cs].astype(BF16), preferred_element_type=F32) + bs_ref[:, g:g + 1]
        o_ref[:, cs] = (u_ref[:, cs].astype(F32) * mixed).astype(o_ref.dtype)


def _gmlp_prompt(ugv, ln_g, ln_b, w_spatial, b_spatial_t):
    m, two_w = ugv.shape
    gmw = two_w // 2
    n_groups, ch, _ = w_spatial.shape
    assert m % ch == 0
    est = 2 * (2 * _nbytes((ch, gmw), ugv.dtype) + _nbytes((ch, gmw), BF16) + _nbytes(w_spatial.shape, F32)) \
        + 6 * _nbytes((ch, gmw), F32)
    kern = functools.partial(_gmlp_prompt_kernel, n_groups=n_groups)
    return pl.pallas_call(
        kern,
        out_shape=jax.ShapeDtypeStruct((m, gmw), BF16),
        grid=(m // ch,),
        in_specs=[pl.BlockSpec((ch, gmw), lambda i: (i, 0)),
                  pl.BlockSpec((ch, gmw), lambda i: (i, 1)),
                  pl.BlockSpec((1, gmw), lambda i: (0, 0)),
                  pl.BlockSpec((1, gmw), lambda i: (0, 0)),
                  pl.BlockSpec((n_groups, ch, ch), lambda i: (0, 0, 0)),
                  pl.BlockSpec((ch, n_groups), lambda i: (0, 0))],
        out_specs=pl.BlockSpec((ch, gmw), lambda i: (i, 0)),
        compiler_params=_params(("arbitrary",), est),
        name="gmlp_prompt",
    )(ugv, ugv, ln_g.reshape(1, gmw), ln_b.reshape(1, gmw), w_spatial, b_spatial_t)


def _gmlp_sample_kernel(u_ref, v_ref, g_ref, b_ref, w0_ref, b0_ref, o_ref, vn_ref):
    vn = _layer_norm(v_ref[...], g_ref[...], b_ref[...])
    vn_ref[...] = vn
    o_ref[...] = (u_ref[...] * (w0_ref[...] * vn + b0_ref[...])).astype(o_ref.dtype)


def _gmlp_sample(ugv, ln_g, ln_b, w00, b0):
    m, two_w = ugv.shape
    gmw = two_w // 2
    est = 2 * (3 * _nbytes((m, gmw), F32) + _nbytes((m, gmw), BF16)) + 4 * _nbytes((m, gmw), F32)
    row = pl.BlockSpec((1, gmw), lambda i: (0, 0))
    return pl.pallas_call(
        _gmlp_sample_kernel,
        out_shape=[jax.ShapeDtypeStruct((m, gmw), BF16), jax.ShapeDtypeStruct((m, gmw), F32)],
        grid=(1,),
        in_specs=[pl.BlockSpec((m, gmw), lambda i: (0, 0)), pl.BlockSpec((m, gmw), lambda i: (0, 1)),
                  row, row, row, row],
        out_specs=[pl.BlockSpec((m, gmw), lambda i: (0, 0)), pl.BlockSpec((m, gmw), lambda i: (0, 0))],
        compiler_params=_params(("arbitrary",), est),
        name="gmlp_sample",
    )(ugv, ugv, ln_g.reshape(1, gmw), ln_b.reshape(1, gmw), w00, b0)


def _pack_bf16_pair(left, right):
    lb = pltpu.bitcast(left.astype(BF16).astype(F32), jnp.uint32)
    rb = pltpu.bitcast(right.astype(BF16).astype(F32), jnp.uint32)
    return lb | (rb >> 16)


def _unpack_bf16_pair(word):
    left = pltpu.bitcast(word & jnp.uint32(0xFFFF0000), F32)
    right = pltpu.bitcast(word << 16, F32)
    return left, right


def _split_bf16(x):
    hi = x.astype(BF16)
    lo = (x - hi.astype(F32)).astype(BF16)
    return hi, lo


def _ffn_norm_kernel(x_ref, g_ref, sh_ref, sc_ref, wr_ref, h_ref, hp_ref, lg_ref):
    h = _rms_mod(x_ref[...], g_ref[...], sc_ref[0], sh_ref[0])
    half = h.shape[1] // 2
    h_ref[...] = h.astype(BF16)
    hp_ref[...] = _pack_bf16_pair(h[:, :half], h[:, half:])
    h_hi, h_lo = _split_bf16(h)
    w_hi, w_lo = _split_bf16(wr_ref[...])
    dn = (((1,), (1,)), ((), ()))
    lg_ref[...] = lax.dot_general(w_hi, h_hi, dn, preferred_element_type=F32) \
        + lax.dot_general(w_hi, h_lo, dn, preferred_element_type=F32) \
        + lax.dot_general(w_lo, h_hi, dn, preferred_element_type=F32)


def _ffn_norm(x2, g, mod3, shift_chunk, scale_chunk, group_rows, w_router_t):
    m, d = x2.shape
    e = w_router_t.shape[0]
    tm = _pick_tile(min(m, group_rows) if mod3.shape[1] == 1 else m, 256, LANES)
    row_of = lambda i: i
    est = 2 * (3 * _nbytes((tm, d), F32) + 2 * _nbytes((tm, d), BF16) + _nbytes((e, d), F32)) \
        + 6 * _nbytes((tm, d), F32)
    return pl.pallas_call(
        _ffn_norm_kernel,
        out_shape=[jax.ShapeDtypeStruct((m, d), BF16),
                   jax.ShapeDtypeStruct((m, d // 2), jnp.uint32),
                   jax.ShapeDtypeStruct((e, m), F32)],
        grid=(m // tm,),
        in_specs=[pl.BlockSpec((tm, d), lambda i: (i, 0)),
                  pl.BlockSpec((1, d), lambda i: (0, 0)),
                  _mod_spec(mod3, tm, group_rows, d, row_of, lambda i: shift_chunk),
                  _mod_spec(mod3, tm, group_rows, d, row_of, lambda i: scale_chunk),
                  pl.BlockSpec((e, d), lambda i: (0, 0))],
        out_specs=[pl.BlockSpec((tm, d), lambda i: (i, 0)),
                   pl.BlockSpec((tm, d // 2), lambda i: (i, 0)),
                   pl.BlockSpec((e, tm), lambda i: (0, i))],
        compiler_params=_params(("arbitrary",), est),
        name="ffn_norm_router",
    )(x2, g.reshape(1, d), mod3, mod3, w_router_t)


def _first_max(vals, idx):
    mx = jnp.max(vals, axis=0, keepdims=True)
    first = jnp.min(jnp.where(vals == mx, idx, jnp.int32(2 ** 30)), axis=0, keepdims=True)
    return mx, first


def _route_kernel(lg_ref, bias_ref, eidx_ref, ew_ref, rank_ref, cnt_ref, base_scr,
                  *, n_groups, topk_groups, top_k, scale):
    e, tr = lg_ref.shape
    per = e // n_groups

    @pl.when(pl.program_id(0) == 0)
    def _():
        base_scr[...] = jnp.zeros_like(base_scr)

    scores = jax.nn.sigmoid(lg_ref[...])
    biased = scores + bias_ref[...]
    eid = lax.broadcasted_iota(jnp.int32, (e, tr), 0)
    neg = jnp.float32(-jnp.inf)

    grp_rows = []
    bid = lax.broadcasted_iota(jnp.int32, (per, tr), 0)
    for g in range(n_groups):
        blk = biased[g * per:(g + 1) * per]
        m1, i1 = _first_max(blk, bid)
        m2 = jnp.max(jnp.where(bid == i1, neg, blk), axis=0, keepdims=True)
        grp_rows.append(m1 + m2)
    grp = jnp.concatenate(grp_rows, axis=0)
    gid = lax.broadcasted_iota(jnp.int32, (n_groups, tr), 0)
    gsel = jnp.zeros((n_groups, tr), F32)
    work = grp
    for _ in range(topk_groups):
        _, gi = _first_max(work, gid)
        hit = gid == gi
        gsel = jnp.where(hit, 1.0, gsel)
        work = jnp.where(hit, neg, work)
    emask = jnp.concatenate(
        [jnp.broadcast_to(gsel[g:g + 1], (per, tr)) for g in range(n_groups)], axis=0)
    masked = jnp.where(emask > 0.0, biased, neg)

    onehots, idxs, wts = [], [], []
    for _ in range(top_k):
        _, ei = _first_max(masked, eid)
        hit = eid == ei
        onehots.append(hit)
        idxs.append(ei)
        wts.append(jnp.sum(jnp.where(hit, scores, 0.0), axis=0, keepdims=True))
        masked = jnp.where(hit, neg, masked)
    wsum = wts[0]
    for wk in wts[1:]:
        wsum = wsum + wk

    chosen = onehots[0]
    for oh in onehots[1:]:
        chosen = jnp.logical_or(chosen, oh)
    chosen_f = jnp.where(chosen, 1.0, 0.0)
    si = lax.broadcasted_iota(jnp.int32, (tr, tr), 0)
    ti = lax.broadcasted_iota(jnp.int32, (tr, tr), 1)
    upper = jnp.where(si < ti, 1.0, 0.0).astype(BF16)
    prefix = jnp.dot(chosen_f.astype(BF16), upper, preferred_element_type=F32)
    pos = prefix + base_scr[:, 0:1]
    for k in range(top_k):
        eidx_ref[k:k + 1, :] = idxs[k]
        ew_ref[k:k + 1, :] = wts[k] / wsum * scale
        rank_ref[k:k + 1, :] = jnp.sum(jnp.where(onehots[k], pos, 0.0), axis=0, keepdims=True).astype(jnp.int32)
    for k in range(top_k, eidx_ref.shape[0]):
        eidx_ref[k:k + 1, :] = jnp.zeros((1, tr), jnp.int32)
        ew_ref[k:k + 1, :] = jnp.zeros((1, tr), F32)
        rank_ref[k:k + 1, :] = jnp.zeros((1, tr), jnp.int32)
    base_scr[...] = base_scr[...] + jnp.sum(chosen_f, axis=1, keepdims=True)
    cnt_ref[...] = base_scr[...].astype(jnp.int32)


def _route(logits_t, router_bias):
    e, n = logits_t.shape
    tr = _pick_tile(n, 640, LANES)
    rows = SUBLANES
    assert TOP_K <= rows and e % N_EXPERT_GROUPS == 0
    kern = functools.partial(_route_kernel, n_groups=N_EXPERT_GROUPS, topk_groups=TOPK_GROUPS,
                             top_k=TOP_K, scale=ROUTED_SCALE)
    est = 2 * (_nbytes((e, tr), F32) + 3 * _nbytes((rows, tr), F32)) + 24 * _nbytes((e, tr), F32) \
        + 3 * _nbytes((tr, tr), F32)
    out_row = pl.BlockSpec((rows, tr), lambda i: (0, i))
    return pl.pallas_call(
        kern,
        out_shape=[jax.ShapeDtypeStruct((rows, n), jnp.int32),
                   jax.ShapeDtypeStruct((rows, n), F32),
                   jax.ShapeDtypeStruct((rows, n), jnp.int32),
                   jax.ShapeDtypeStruct((e, LANES), jnp.int32)],
        grid=(n // tr,),
        in_specs=[pl.BlockSpec((e, tr), lambda i: (0, i)),
                  pl.BlockSpec((e, 1), lambda i: (0, 0))],
        out_specs=[out_row, out_row, out_row, pl.BlockSpec((e, LANES), lambda i: (0, 0))],
        scratch_shapes=[pltpu.VMEM((e, LANES), F32)],
        compiler_params=_params(("arbitrary",), est),
        name="route",
    )(logits_t, router_bias.reshape(e, 1))


def _dispatch_kernel(dest_ref, fill_ref, nfill_ref, hp_ref, xs_ref, zero_scr, sem, *, top_k, n_tok):
    step = pl.program_id(0)
    tt = hp_ref.shape[0]
    fill_rows = zero_scr.shape[0]

    @pl.when(step == 0)
    def _():
        zero_scr[...] = jnp.zeros_like(zero_scr)
        n_fill = nfill_ref[0]

        def start_fill(i, c):
            row0 = pl.multiple_of(fill_ref[i] * fill_rows, fill_rows)
            pltpu.make_async_copy(zero_scr, xs_ref.at[pl.ds(row0, fill_rows)], sem).start()
            return c

        def wait_fill(i, c):
            pltpu.make_async_copy(zero_scr, xs_ref.at[pl.ds(0, fill_rows)], sem).wait()
            return c

        lax.fori_loop(0, n_fill, start_fill, 0)
        lax.fori_loop(0, n_fill, wait_fill, 0)

    base = step * tt

    def start(r, c):
        for k in range(top_k):
            pltpu.make_async_copy(hp_ref.at[pl.ds(r, 1)],
                                  xs_ref.at[pl.ds(dest_ref[k * n_tok + base + r], 1)], sem).start()
        return c

    def wait(r, c):
        for k in range(top_k):
            pltpu.make_async_copy(hp_ref.at[pl.ds(r, 1)], xs_ref.at[pl.ds(0, 1)], sem).wait()
        return c

    lax.fori_loop(0, tt, start, 0)
    lax.fori_loop(0, tt, wait, 0)


def _dispatch(dest_flat, fill_tiles, n_fill, hp, n_rows, fill_rows):
    n_tok, half = hp.shape
    tt = _pick_tile(n_tok, 128, SUBLANES)
    kern = functools.partial(_dispatch_kernel, top_k=TOP_K, n_tok=n_tok)
    est = 2 * _nbytes((tt, half), hp.dtype) + _nbytes((fill_rows, half), hp.dtype)
    return pl.pallas_call(
        kern,
        out_shape=jax.ShapeDtypeStruct((n_rows, half), hp.dtype),
        grid_spec=pltpu.PrefetchScalarGridSpec(
            num_scalar_prefetch=3,
            grid=(n_tok // tt,),
            in_specs=[pl.BlockSpec((tt, half), lambda i, dest, fill, nfill: (i, 0))],
            out_specs=pl.BlockSpec(memory_space=pl.ANY),
            scratch_shapes=[pltpu.VMEM((fill_rows, half), hp.dtype), pltpu.SemaphoreType.DMA]),
        compiler_params=_params(("arbitrary",), est),
        name="moe_dispatch",
    )(dest_flat, fill_tiles, n_fill, hp)


def _stream_expert_weights(tile_e_ref, next_e_ref, flags_ref, blk_ref, w_refs, col_of, stage, sems):
    j = pl.program_id(0)
    t = pl.program_id(1)
    n_pass = pl.num_programs(0)
    width = stage.shape[-1]

    def copies(e, jj, slot):
        return [pltpu.make_async_copy(
            w_ref.at[e, :, pl.ds(pl.multiple_of(col(jj), width), width)], stage.at[slot, i], sems.at[slot])
            for i, (w_ref, col) in enumerate(zip(w_refs, col_of))]

    @pl.when(jnp.logical_and(j == 0, t == 0))
    def _():
        blk_ref[0] = 0
        for c in copies(tile_e_ref[0], 0, 0):
            c.start()

    slot = blk_ref[0] % 2
    for c in copies(tile_e_ref[t], j, slot):
        c.wait()
    in_pass = (flags_ref[t] & TILE_HAS_NEXT) > 0
    e_next = jnp.where(in_pass, next_e_ref[t], tile_e_ref[0])
    j_next = jnp.where(in_pass, j, j + 1)

    @pl.when(jnp.logical_or(in_pass, j + 1 < n_pass))
    def _():
        for c in copies(e_next, j_next, 1 - slot):
            c.start()

    blk_ref[0] = blk_ref[0] + 1


def _expert_up_kernel(tile_e_ref, used_ref, next_e_ref, flags_ref, xs_ref, wg_ref, wu_ref, o_ref,
                      stage, sems, blk_ref):
    t = pl.program_id(1)
    n_used = used_ref[0]
    fc = o_ref.shape[1]

    @pl.when(t < n_used)
    def _():
        prev = tile_e_ref[jnp.maximum(t - 1, 0)]
        fresh = jnp.logical_or(t == 0, tile_e_ref[t] != prev)

        @pl.when(fresh)
        def _():
            col = lambda jj: jj * fc
            _stream_expert_weights(tile_e_ref, next_e_ref, flags_ref, blk_ref, [wg_ref, wu_ref], [col, col],
                                   stage, sems)

        half = xs_ref.shape[1]
        tm = xs_ref.shape[0]
        slot = (blk_ref[0] + 1) % 2

        def project(rows):
            xl, xr = _unpack_bf16_pair(xs_ref[:rows])
            xl = xl.astype(BF16)
            xr = xr.astype(BF16)
            gate = jnp.dot(xl, stage[slot, 0, :half].astype(BF16), preferred_element_type=F32) \
                + jnp.dot(xr, stage[slot, 0, half:].astype(BF16), preferred_element_type=F32)
            up = jnp.dot(xl, stage[slot, 1, :half].astype(BF16), preferred_element_type=F32) \
                + jnp.dot(xr, stage[slot, 1, half:].astype(BF16), preferred_element_type=F32)
            o_ref[:rows] = (_silu(gate) * up).astype(o_ref.dtype)
            if rows < tm:
                o_ref[rows:] = jnp.zeros((tm - rows, fc), o_ref.dtype)

        half_full = (flags_ref[t] & TILE_HALF_FULL) > 0

        @pl.when(half_full)
        def _():
            project(tm // 2)

        @pl.when(jnp.logical_not(half_full))
        def _():
            project(tm)

    @pl.when(t >= n_used)
    def _():
        o_ref[...] = jnp.zeros_like(o_ref)


def _expert_up(tile_e, n_used, next_e, tile_flags, xs, w_gate, w_up, tm, fc):
    p, half = xs.shape
    e, d, de = w_gate.shape
    n_t = p // tm
    clamp = lambda t, used: jnp.minimum(t, used[0] - 1)
    est = 2 * (_nbytes((tm, half), jnp.uint32) + 2 * _nbytes((d, fc), F32)
               + _nbytes((tm, fc), BF16)) + 2 * _nbytes((d, fc), BF16) + 2 * _nbytes((tm, d), BF16) \
        + 4 * _nbytes((tm, fc), F32) + 2 * _nbytes((tm, half), F32)
    return pl.pallas_call(
        _expert_up_kernel,
        out_shape=jax.ShapeDtypeStruct((p, de), BF16),
        grid_spec=pltpu.PrefetchScalarGridSpec(
            num_scalar_prefetch=4,
            grid=(de // fc, n_t),
            in_specs=[pl.BlockSpec((tm, half), lambda j, t, te, used, ne, hn: (clamp(t, used), 0)),
                      pl.BlockSpec(memory_space=pl.ANY),
                      pl.BlockSpec(memory_space=pl.ANY)],
            out_specs=pl.BlockSpec((tm, fc), lambda j, t, te, used, ne, hn: (t, j)),
            scratch_shapes=[pltpu.VMEM((2, 2, d, fc), F32),
                            pltpu.SemaphoreType.DMA((2,)), pltpu.SMEM((1,), jnp.int32)]),
        compiler_params=_params(("arbitrary", "arbitrary"), est),
        name="expert_up",
    )(tile_e, n_used, next_e, tile_flags, xs, w_gate, w_up)


def _expert_down_kernel(tile_e_ref, used_ref, next_e_ref, flags_ref, a_ref, wd_ref, o_ref,
                        stage, sems, blk_ref):
    t = pl.program_id(1)
    n_used = used_ref[0]
    nc = o_ref.shape[1]
    half = wd_ref.shape[2] // 2

    @pl.when(t < n_used)
    def _():
        prev = tile_e_ref[jnp.maximum(t - 1, 0)]
        fresh = jnp.logical_or(t == 0, tile_e_ref[t] != prev)

        @pl.when(fresh)
        def _():
            _stream_expert_weights(tile_e_ref, next_e_ref, flags_ref, blk_ref, [wd_ref, wd_ref],
                                   [lambda cc: cc * nc, lambda cc: half + cc * nc],
                                   stage, sems)

        tm = a_ref.shape[0]
        slot = (blk_ref[0] + 1) % 2

        def project(rows):
            a = a_ref[:rows]
            yl = jnp.dot(a, stage[slot, 0].astype(BF16), preferred_element_type=F32)
            yr = jnp.dot(a, stage[slot, 1].astype(BF16), preferred_element_type=F32)
            o_ref[:rows] = _pack_bf16_pair(yl, yr)
            if rows < tm:
                o_ref[rows:] = jnp.zeros((tm - rows, nc), o_ref.dtype)

        half_full = (flags_ref[t] & TILE_HALF_FULL) > 0

        @pl.when(half_full)
        def _():
            project(tm // 2)

        @pl.when(jnp.logical_not(half_full))
        def _():
            project(tm)

    @pl.when(t >= n_used)
    def _():
        o_ref[...] = jnp.zeros_like(o_ref)


def _expert_down(tile_e, n_used, next_e, tile_flags, act, w_down, tm, nc):
    p, de = act.shape
    e, _, d = w_down.shape
    half = d // 2
    n_t = p // tm
    n_c = half // nc
    clamp = lambda t, used: jnp.minimum(t, used[0] - 1)
    est = 2 * (_nbytes((tm, de), BF16) + 2 * _nbytes((de, nc), F32) + _nbytes((tm, nc), jnp.uint32)) \
        + 2 * _nbytes((de, nc), BF16) + 3 * _nbytes((tm, nc), F32)
    return pl.pallas_call(
        _expert_down_kernel,
        out_shape=jax.ShapeDtypeStruct((p, half), jnp.uint32),
        grid_spec=pltpu.PrefetchScalarGridSpec(
            num_scalar_prefetch=4,
            grid=(n_c, n_t),
            in_specs=[pl.BlockSpec((tm, de), lambda c, t, te, used, ne, hn: (clamp(t, used), 0)),
                      pl.BlockSpec(memory_space=pl.ANY)],
            out_specs=pl.BlockSpec((tm, nc), lambda c, t, te, used, ne, hn: (t, c)),
            scratch_shapes=[pltpu.VMEM((2, 2, de, nc), F32),
                            pltpu.SemaphoreType.DMA((2,)), pltpu.SMEM((1,), jnp.int32)]),
        compiler_params=_params(("arbitrary", "arbitrary"), est),
        name="expert_down",
    )(tile_e, n_used, next_e, tile_flags, act, w_down)


def _combine_kernel(dest_ref, ys_ref, x_ref, sh_ref, gf_ref, ew_ref, o_ref, buf, sem, *, top_k, n_tok, tok0):
    tt = x_ref.shape[0]
    half = ys_ref.shape[1]
    step = pl.program_id(0)
    slot = step % 2

    def gather(for_step, into, start):
        base = tok0 + for_step * tt

        def body(r, c):
            for k in range(top_k):
                row = dest_ref[k * n_tok + base + r] if start else 0
                cp = pltpu.make_async_copy(ys_ref.at[pl.ds(row, 1)], buf.at[into, k, pl.ds(r, 1)], sem.at[into])
                if start:
                    cp.start()
                else:
                    cp.wait()
            return c

        lax.fori_loop(0, tt, body, 0)

    @pl.when(step == 0)
    def _():
        gather(0, 0, True)

    @pl.when(step + 1 < pl.num_programs(0))
    def _():
        gather(step + 1, 1 - slot, True)

    gather(step, slot, False)
    per_row_gate = gf_ref.shape[1] != 1

    def reduce_rows(c, carry):
        rows = pl.ds(pl.multiple_of(c * SUBLANES, SUBLANES), SUBLANES)
        ew = ew_ref[rows, :]
        gf = gf_ref[0, rows, :] if per_row_gate else gf_ref[0]
        for side, cols in ((0, slice(0, half)), (1, slice(half, 2 * half))):
            acc = None
            for k in range(top_k):
                term = _unpack_bf16_pair(buf[slot, k, rows, :])[side] * ew[:, k:k + 1]
                acc = term if acc is None else acc + term
            o_ref[rows, cols] = x_ref[rows, cols] + gf[:, cols] * (acc + sh_ref[rows, cols])
        return carry

    lax.fori_loop(0, tt // SUBLANES, reduce_rows, 0)


def _combine(dest_flat, ys, x2, shared, mod3, ew_t, gate_chunk, group_rows, n_tok, tok0):
    m, d = x2.shape
    half = d // 2
    ew_rows = ew_t.shape[1]
    tt = _pick_tile(min(m, group_rows) if mod3.shape[1] == 1 else m, 128, SUBLANES)
    sh_blk0 = tok0 // tt
    assert tok0 % tt == 0
    row_of = lambda i, dest: i
    kern = functools.partial(_combine_kernel, top_k=TOP_K, n_tok=n_tok, tok0=tok0)
    est = 2 * (3 * _nbytes((tt, d), F32) + _nbytes((tt, d), F32)) + 2 * _nbytes((TOP_K, tt, half), jnp.uint32) \
        + 6 * _nbytes((tt, d), F32)
    return pl.pallas_call(
        kern,
        out_shape=jax.ShapeDtypeStruct((m, d), F32),
        grid_spec=pltpu.PrefetchScalarGridSpec(
            num_scalar_prefetch=1,
            grid=(m // tt,),
            in_specs=[pl.BlockSpec(memory_space=pl.ANY),
                      pl.BlockSpec((tt, d), lambda i, dest: (i, 0)),
                      pl.BlockSpec((tt, d), lambda i, dest: (i + sh_blk0, 0)),
                      _mod_spec(mod3, tt, group_rows, d, row_of, lambda i, dest: gate_chunk),
                      pl.BlockSpec((tt, ew_rows), lambda i, dest: (i + sh_blk0, 0))],
            out_specs=pl.BlockSpec((tt, d), lambda i, dest: (i, 0)),
            scratch_shapes=[pltpu.VMEM((2, TOP_K, tt, half), jnp.uint32), pltpu.SemaphoreType.DMA((2,))]),
        compiler_params=_params(("arbitrary",), est),
        name="moe_combine",
    )(dest_flat, ys, x2, shared, mod3, ew_t)


def _token_mixing(x2, mod3, group_rows, p, attend):
    m, d = x2.shape
    aw, kvw, gmw, hd = p["aw"], p["kvw"], p["gmw"], p["hd"]
    tn = 512
    assert aw % tn == 0 and (2 * kvw) % tn == 0 and gmw % tn == 0 and d % tn == 0
    tm = _pick_tile(min(m, group_rows) if mod3.shape[1] == 1 else m, 1024, 16)
    h = _modulate(x2, p["norm_mix_g"], mod3, 0, 1, group_rows)
    w_in = p["w_in"]

    qkv_w = aw + 2 * kvw

    def qkv_epilogue(accs, extras, n, mi):
        gain_ref, flag_ref, bd_ref = extras
        z = accs[0]
        sq_hi, sq_lo = _split_bf16(z * z)
        ss = jnp.dot(sq_hi, bd_ref[...], preferred_element_type=F32) \
            + jnp.dot(sq_lo, bd_ref[...], preferred_element_type=F32)
        inv = lax.rsqrt(ss * (1.0 / hd) + EPS)
        return [z * jnp.where(flag_ref[...] > 0.0, inv, 1.0) * gain_ref[...]]

    qkv = _mm([(h, d, 0)], [(w_in, 0)],
              [(p["qkv_gain"], pl.BlockSpec((1, tn), lambda n, mi: (0, n))),
               (p["qkv_flag"], pl.BlockSpec((1, tn), lambda n, mi: (0, n))),
               (p["head_ones"], pl.BlockSpec((tn, tn), lambda n, mi: (0, 0)))],
              qkv_epilogue,
              [(jax.ShapeDtypeStruct((m, qkv_w), F32), pl.BlockSpec((tm, tn), lambda n, mi: (mi, n)))],
              m=m, tm=tm, tn=tn, n_tiles=qkv_w // tn, pairs=[(0, 0)], name="in_proj_qkv")[0]

    ugv_dtype = BF16 if mod3.shape[1] == 1 else F32
    ugv = _mm([(h, d, 0)], [(w_in, qkv_w // tn)], [],
              lambda accs, extras, n, mi: [_gelu(accs[0])],
              [(jax.ShapeDtypeStruct((m, 2 * gmw), ugv_dtype), pl.BlockSpec((tm, tn), lambda n, mi: (mi, n)))],
              m=m, tm=tm, tn=tn, n_tiles=2 * gmw // tn, pairs=[(0, 0)], name="in_proj_gmlp")[0]

    gates = _mm([(h, d, 0)], [(w_in, (qkv_w + 2 * gmw) // tn)], [],
                lambda accs, extras, n, mi: [jax.nn.sigmoid(accs[0])],
                [(jax.ShapeDtypeStruct((m, 2 * d), BF16), pl.BlockSpec((tm, tn), lambda n, mi: (mi, n)))],
                m=m, tm=tm, tn=tn, n_tiles=2 * d // tn, pairs=[(0, 0)], name="in_proj_gates")[0]

    o_attn, o_gmlp, aux = attend(qkv, ugv)

    nd = d // tn
    merged = _mm([(o_attn, aw, 0), (o_gmlp, gmw, 0)], [(p["w_branch_attn"], 0), (p["w_branch_gmlp"], 0)],
                 [(gates, pl.BlockSpec((tm, tn), lambda n, mi: (mi, n))),
                  (gates, pl.BlockSpec((tm, tn), lambda n, mi: (mi, n + nd)))],
                 lambda accs, extras, n, mi: [extras[0][...].astype(F32) * accs[0]
                                              + extras[1][...].astype(F32) * accs[1]],
                 [(jax.ShapeDtypeStruct((m, d), BF16), pl.BlockSpec((tm, tn), lambda n, mi: (mi, n)))],
                 m=m, tm=tm, tn=tn, n_tiles=nd, pairs=[(0, 0), (1, 1)], name="branch_merge")[0]

    x1 = _mm([(merged, d, 0)], [(p["w_out"], 0)],
             [(x2, pl.BlockSpec((tm, tn), lambda n, mi: (mi, n))),
              (mod3, _mod_spec(mod3, tm, group_rows, tn, lambda n, mi: mi, lambda n, mi: 2 * nd + n))],
             lambda accs, extras, n, mi: [extras[0][...] + extras[1][0] * accs[0]],
             [(jax.ShapeDtypeStruct((m, d), F32), pl.BlockSpec((tm, tn), lambda n, mi: (mi, n)))],
             m=m, tm=tm, tn=tn, n_tiles=nd, pairs=[(0, 0)], name="out_proj")[0]
    return x1, qkv, aux


def kernel(x_prompt, x_sample, cache_k_win, cache_v_win, c_prompt, c_sample, norm_mix_g, norm_ffn_g,
           w_ada, b_ada, w_in, q_norm_g, k_norm_g, attn_sinks, gm_ln_g, gm_ln_b, w_spatial, b_spatial,
           w_branch_attn, w_branch_gmlp, w_out, w_router, router_bias, w_gate, w_up, w_down,
           ws_gate, ws_up, ws_down):
    depth = norm_mix_g.shape[0]
    assert depth == 1, "single-layer step"
    batch, seq, d = x_prompt.shape
    db, t_new, _ = x_sample.shape
    assert t_new == 1, "one new token per sequence"
    _, _, win, n_kv, hd = cache_k_win.shape
    n_grp = attn_sinks.shape[-1]
    aw, kvw = n_kv * n_grp * hd, n_kv * hd
    gmw = gm_ln_g.shape[-1]
    n_groups, ch, _ = w_spatial.shape[1:]
    n_exp = w_router.shape[-1]
    de = w_gate.shape[-1]
    ds = ws_gate.shape[-1]
    tn = 512
    l = 0

    n_c = batch + db
    rows = -(-n_c // 16) * 16
    c_all = jnp.concatenate([c_prompt, c_sample, jnp.zeros((rows - n_c, d), F32)], axis=0)
    mods = _adaln(c_all, w_ada[l], b_ada[l])
    mod_p = mods[:batch].reshape(batch, 1, 6 * d)
    mod_s = mods[batch:n_c].reshape(1, db, 6 * d)

    gq = jnp.tile(q_norm_g[l], aw // hd)
    gk = jnp.tile(k_norm_g[l], kvw // hd)
    qkv_gain = jnp.concatenate([gq, gk, jnp.ones((kvw,), F32)]).reshape(1, -1)
    qkv_flag = jnp.concatenate([jnp.ones((aw + kvw,), F32), jnp.zeros((kvw,), F32)]).reshape(1, -1)
    hid = jnp.arange(tn) // hd
    head_ones = (hid[:, None] == hid[None, :]).astype(BF16)
    params = dict(aw=aw, kvw=kvw, gmw=gmw, hd=hd, norm_mix_g=norm_mix_g[l], w_in=w_in[l],
                  qkv_gain=qkv_gain, qkv_flag=qkv_flag, head_ones=head_ones,
                  w_branch_attn=w_branch_attn[l], w_branch_gmlp=w_branch_gmlp[l], w_out=w_out[l])
    sinks = attn_sinks[l]

    def attend_prompt(qkv, ugv):
        o_attn = _attn_prompt(qkv, sinks, batch, seq, n_kv, n_grp, hd)
        o_gmlp = _gmlp_prompt(ugv, gm_ln_g[l], gm_ln_b[l], w_spatial[l], b_spatial[l].T)
        return o_attn, o_gmlp, None

    xp2 = x_prompt.reshape(batch * seq, d)
    x1_p, qkv_p, _ = _token_mixing(xp2, mod_p, seq, params, attend_prompt)

    ck = cache_k_win[l].reshape(db, win, kvw)
    cv = cache_v_win[l].reshape(db, win, kvw)

    def attend_sample(qkv, ugv):
        q3 = qkv[:, :aw].reshape(db, n_kv * n_grp, hd)
        o3, new_k, new_v = _attn_sample(q3, qkv, ck, cv, sinks.reshape(-1, 1), n_kv, n_grp, hd)
        w00 = jnp.repeat(w_spatial[l][:, 0, 0], gmw // n_groups).reshape(1, gmw)
        b0 = jnp.repeat(b_spatial[l][:, 0], gmw // n_groups).reshape(1, gmw)
        o_gmlp, vn = _gmlp_sample(ugv, gm_ln_g[l], gm_ln_b[l], w00, b0)
        return o3.reshape(db, aw), o_gmlp, (new_k, new_v, vn)

    xs2 = x_sample.reshape(db, d)
    x1_s, _, (new_k_s, new_v_s, vn_s) = _token_mixing(xs2, mod_s, 1, params, attend_sample)

    w_router_t = w_router[l].T
    h2_p, hp_p, lg_p = _ffn_norm(x1_p, norm_ffn_g[l], mod_p, 3, 4, seq, w_router_t)
    h2_s, hp_s, lg_s = _ffn_norm(x1_s, norm_ffn_g[l], mod_s, 3, 4, 1, w_router_t)
    h2 = jnp.concatenate([h2_p, h2_s], axis=0)
    hp = jnp.concatenate([hp_p, hp_s], axis=0)
    logits_t = jnp.concatenate([lg_p, lg_s], axis=1)
    n_tok = h2.shape[0]

    eidx8, ew8, rank8, counts = _route(logits_t, router_bias[l])
    eidx, ew, rank = eidx8[:TOP_K], ew8[:TOP_K], rank8[:TOP_K]
    counts = counts[:, 0]

    tm_e = 256
    n_assign = n_tok * TOP_K
    n_tiles = -(-n_assign // tm_e) + n_exp
    n_rows = n_tiles * tm_e
    padded = (counts + tm_e - 1) // tm_e * tm_e
    pad_end = jnp.cumsum(padded)
    pad_start = pad_end - padded
    expert_ids = jnp.arange(n_exp, dtype=jnp.int32)
    dest = rank + jnp.sum(jnp.where(eidx[:, :, None] == expert_ids, pad_start.astype(jnp.int32), 0), axis=-1)
    dest_flat = dest.reshape(-1)
    tile_ids = jnp.arange(n_tiles, dtype=jnp.int32)
    tile_e = jnp.minimum(jnp.sum((pad_end[None, :] <= tile_ids[:, None] * tm_e).astype(jnp.int32), axis=1),
                         n_exp - 1).astype(jnp.int32)
    used_tiles = (pad_end[-1:] // tm_e).astype(jnp.int32)
    pad_fill = jnp.maximum(pad_end // tm_e - 1, 0).astype(jnp.int32)
    tail_fill = jnp.minimum(used_tiles + tile_ids, n_tiles - 1)
    fill_tiles = jnp.concatenate([pad_fill, tail_fill])
    n_fill = (n_exp + n_tiles - used_tiles).astype(jnp.int32)

    seg_end_tile = (pad_end // tm_e).astype(jnp.int32)[tile_e]
    next_e = tile_e[jnp.minimum(seg_end_tile, n_tiles - 1)]
    rows_in_tile = counts[tile_e] - (tile_ids - (pad_start // tm_e).astype(jnp.int32)[tile_e]) * tm_e
    tile_flags = jnp.where(seg_end_tile < used_tiles, TILE_HAS_NEXT, 0) \
        + jnp.where(rows_in_tile <= tm_e // 2, TILE_HALF_FULL, 0)
    tile_flags = tile_flags.astype(jnp.int32)

    xs = _dispatch(dest_flat, fill_tiles, n_fill, hp, n_rows, tm_e)
    act = _expert_up(tile_e, used_tiles, next_e, tile_flags, xs, w_gate[l], w_up[l], tm_e,
                     _pick_tile(de, 512, LANES))
    ys = _expert_down(tile_e, used_tiles, next_e, tile_flags, act, w_down[l], tm_e,
                      _pick_tile(d // 2, 2048, LANES))

    tm_s = _pick_tile(n_tok, 640, 16)
    tn_s = _pick_tile(ds, 256, LANES)
    sh_act = _mm([(h2, d, 0)], [(ws_gate[l], 0), (ws_up[l], 0)], [],
                 lambda accs, extras, n, mi: [_silu(accs[0]) * accs[1]],
                 [(jax.ShapeDtypeStruct((n_tok, ds), BF16), pl.BlockSpec((tm_s, tn_s), lambda n, mi: (mi, n)))],
                 m=n_tok, tm=tm_s, tn=tn_s, n_tiles=ds // tn_s, pairs=[(0, 0), (0, 1)], name="shared_up")[0]
    shared = _mm([(sh_act, ds, 0)], [(ws_down[l], 0)], [],
                 lambda accs, extras, n, mi: [accs[0]],
                 [(jax.ShapeDtypeStruct((n_tok, d), F32), pl.BlockSpec((tm_s, tn), lambda n, mi: (mi, n)))],
                 m=n_tok, tm=tm_s, tn=tn, n_tiles=d // tn, pairs=[(0, 0)], name="shared_down")[0]

    ew_t = ew8.T
    y_p = _combine(dest_flat, ys, x1_p, shared, mod_p, ew_t, 5, seq, n_tok, 0)
    y_s = _combine(dest_flat, ys, x1_s, shared, mod_s, ew_t, 5, 1, n_tok, batch * seq)

    w_keep = min(WINDOW, seq)
    qkv_p3 = qkv_p.reshape(batch, seq, aw + 2 * kvw)
    new_k_p = qkv_p3[:, seq - w_keep:, aw:aw + kvw].reshape(1, batch, w_keep, n_kv, hd)
    new_v_p = qkv_p3[:, seq - w_keep:, aw + kvw:].reshape(1, batch, w_keep, n_kv, hd)
    return (y_p.reshape(batch, seq, d), y_s.reshape(db, 1, d), new_k_p, new_v_p,
            new_k_s.reshape(1, db, win, n_kv, hd), new_v_s.reshape(1, db, win, n_kv, hd),
            vn_s.reshape(1, db, 1, gmw))
```

```python
import functools

import jax
import jax.numpy as jnp
from jax import lax
from jax.experimental import pallas as pl
from jax.experimental.pallas import tpu as pltpu

TOP_K = 6
N_EXPERT_GROUPS = 8
TOPK_GROUPS = 4
ROUTED_SCALE = 2.5
WINDOW = 128
EPS = 1e-6

TILE_HAS_NEXT = 1
TILE_HALF_FULL = 2

V7X_VMEM_BYTES = 64 * 1024 * 1024
V7X_VMEM_REQUEST_CAP = 56 * 1024 * 1024
LANES = 128
SUBLANES = 8
N_DMA_PRIORITIES = 2
WEIGHT_STREAM_DMA_PRIORITY = 1

BF16 = jnp.bfloat16
F32 = jnp.float32


def _pick_tile(total, preferred, multiple):
    t = min(preferred, total)
    t -= t % multiple
    while t > multiple and total % t:
        t -= multiple
    assert t > 0 and total % t == 0, (total, preferred, multiple)
    return t


def _params(semantics, vmem_bytes):
    return pltpu.CompilerParams(
        dimension_semantics=semantics,
        vmem_limit_bytes=int(min(V7X_VMEM_REQUEST_CAP, max(vmem_bytes, 16 * 1024 * 1024))))


def _nbytes(shape, dtype):
    n = 1
    for s in shape:
        n *= s
    return n * jnp.dtype(dtype).itemsize


def _mod_spec(mod3, tm, group_rows, width, row_of, col_of):
    if mod3.shape[1] == 1:
        return pl.BlockSpec((1, 1, width), lambda *g: ((row_of(*g) * tm) // group_rows, 0, col_of(*g)))
    return pl.BlockSpec((1, tm, width), lambda *g: (0, row_of(*g), col_of(*g)))


def _adaln_kernel(c_ref, w_ref, b_ref, o_ref, a_scr):
    @pl.when(pl.program_id(0) == 0)
    def _():
        c = c_ref[...]
        a_scr[...] = (c * jax.nn.sigmoid(c)).astype(BF16)

    acc = jnp.dot(a_scr[...], w_ref[...].astype(BF16), preferred_element_type=F32)
    o_ref[...] = acc + b_ref[...]


def _adaln(c_all, w_ada, b_ada):
    rows, d = c_all.shape
    n_out = w_ada.shape[1]
    tn = _pick_tile(n_out, 512, LANES)
    est = 2 * (_nbytes((d, tn), F32) + _nbytes((rows, tn), F32)) + _nbytes((rows, d), F32) * 2 \
        + _nbytes((rows, d), BF16) + _nbytes((d, tn), BF16) + _nbytes((rows, tn), F32)
    return pl.pallas_call(
        _adaln_kernel,
        out_shape=jax.ShapeDtypeStruct((rows, n_out), F32),
        grid=(n_out // tn,),
        in_specs=[pl.BlockSpec((rows, d), lambda n: (0, 0)),
                  pl.BlockSpec((d, tn), lambda n: (0, n)),
                  pl.BlockSpec((1, tn), lambda n: (0, n))],
        out_specs=pl.BlockSpec((rows, tn), lambda n: (0, n)),
        scratch_shapes=[pltpu.VMEM((rows, d), BF16)],
        compiler_params=_params(("arbitrary",), est),
        name="adaln",
    )(c_all, w_ada, b_ada.reshape(1, n_out))


def _rms_mod(x, g, scale, shift):
    y = x * lax.rsqrt(jnp.mean(x * x, axis=-1, keepdims=True) + EPS)
    return (y * g) * (1.0 + scale) + shift


def _modulate_kernel(x_ref, g_ref, sh_ref, sc_ref, o_ref):
    o_ref[...] = _rms_mod(x_ref[...], g_ref[...], sc_ref[0], sh_ref[0]).astype(o_ref.dtype)


def _modulate(x2, g, mod3, shift_chunk, scale_chunk, group_rows):
    m, d = x2.shape
    tm = _pick_tile(min(m, group_rows) if mod3.shape[1] == 1 else m, 256, SUBLANES)
    row_of = lambda i: i
    est = 2 * (_nbytes((tm, d), F32) * 3 + _nbytes((tm, d), BF16)) + 4 * _nbytes((tm, d), F32)
    return pl.pallas_call(
        _modulate_kernel,
        out_shape=jax.ShapeDtypeStruct((m, d), BF16),
        grid=(m // tm,),
        in_specs=[pl.BlockSpec((tm, d), lambda i: (i, 0)),
                  pl.BlockSpec((1, d), lambda i: (0, 0)),
                  _mod_spec(mod3, tm, group_rows, d, row_of, lambda i: shift_chunk),
                  _mod_spec(mod3, tm, group_rows, d, row_of, lambda i: scale_chunk)],
        out_specs=pl.BlockSpec((tm, d), lambda i: (i, 0)),
        compiler_params=_params(("arbitrary",), est),
        name="modulate",
    )(x2, g.reshape(1, d), mod3, mod3)


def _mm_kernel(*refs, n_a, n_w, n_extra, n_out, pairs, epilogue):
    a_refs = refs[:n_a]
    w_refs = refs[n_a:n_a + n_w]
    extra_refs = refs[n_a + n_w:n_a + n_w + n_extra]
    out_refs = refs[n_a + n_w + n_extra:n_a + n_w + n_extra + n_out]
    w_scr = refs[n_a + n_w + n_extra + n_out:]
    n = pl.program_id(0)
    mi = pl.program_id(1)

    @pl.when(mi == 0)
    def _():
        for w_ref, scr in zip(w_refs, w_scr):
            scr[...] = w_ref[...].astype(BF16)

    accs = [jnp.dot(a_refs[ia][...], w_scr[iw][...], preferred_element_type=F32) for ia, iw in pairs]
    outs = epilogue(accs, extra_refs, n, mi)
    for o_ref, val in zip(out_refs, outs):
        o_ref[...] = val.astype(o_ref.dtype)


def _mm(a_list, w_list, extras, epilogue, outs, *, m, tm, tn, n_tiles, pairs, name):
    in_specs, args, est = [], [], 0
    for arr, kw, cb in a_list:
        in_specs.append(pl.BlockSpec((tm, kw), lambda n, mi, cb=cb: (mi, cb)))
        args.append(arr)
        est += 2 * _nbytes((tm, kw), arr.dtype)
    scratch = []
    for arr, off in w_list:
        k = arr.shape[0]
        in_specs.append(pl.BlockSpec((k, tn), lambda n, mi, off=off: (0, n + off)))
        args.append(arr)
        scratch.append(pltpu.VMEM((k, tn), BF16))
        est += 2 * _nbytes((k, tn), F32) + _nbytes((k, tn), BF16)
    for arr, spec in extras:
        in_specs.append(spec)
        args.append(arr)
        est += 2 * _nbytes(spec.block_shape, arr.dtype)
    for sds, spec in outs:
        est += 2 * _nbytes(spec.block_shape, sds.dtype)
    est += (len(pairs) + 2) * _nbytes((tm, tn), F32)
    kern = functools.partial(_mm_kernel, n_a=len(a_list), n_w=len(w_list), n_extra=len(extras),
                             n_out=len(outs), pairs=tuple(pairs), epilogue=epilogue)
    res = pl.pallas_call(
        kern,
        out_shape=[sds for sds, _ in outs],
        grid=(n_tiles, m // tm),
        in_specs=in_specs,
        out_specs=[spec for _, spec in outs],
        scratch_shapes=scratch,
        compiler_params=_params(("arbitrary", "arbitrary"), est),
        name=name,
    )(*args)
    return res


def _gelu(x):
    return x * (lax.erf(x * (2.0 ** -0.5)) + 1.0) * 0.5


def _silu(x):
    return x * jax.nn.sigmoid(x)


def _attn_prompt_kernel(sinks_ref, q_ref, kc_ref, kp_ref, vc_ref, vp_ref, o_ref, *, n_kv, n_grp, hd):
    i = pl.program_id(1)
    w = q_ref.shape[0]
    scale = hd ** -0.5
    rows = n_grp * w
    qi = lax.broadcasted_iota(jnp.int32, (rows, w), 0) % w
    sj = lax.broadcasted_iota(jnp.int32, (rows, w), 1)
    mask_c = sj <= qi
    mask_p = jnp.logical_and(sj >= qi, i > 0)
    ones = jnp.ones((w, w), BF16)
    dn = (((1,), (1,)), ((), ()))
    for kv in range(n_kv):
        ks = slice(kv * hd, (kv + 1) * hd)
        kc = kc_ref[:, ks].astype(BF16)
        kp = kp_ref[:, ks].astype(BF16)
        vc = vc_ref[:, ks].astype(BF16)
        vp = vp_ref[:, ks].astype(BF16)
        cols = [(kv * n_grp + g) * hd for g in range(n_grp)]
        q = jnp.concatenate([q_ref[:, c:c + hd] for c in cols], axis=0).astype(BF16)
        sink = jnp.concatenate([jnp.full((w, 1), sinks_ref[kv, g], F32) for g in range(n_grp)], axis=0)
        lc = jnp.where(mask_c, lax.dot_general(q, kc, dn, preferred_element_type=F32) * scale, -jnp.inf)
        lp = jnp.where(mask_p, lax.dot_general(q, kp, dn, preferred_element_type=F32) * scale, -jnp.inf)
        mx = jnp.maximum(jnp.max(jnp.maximum(lc, lp), axis=-1, keepdims=True), sink)
        pc = jnp.exp(lc - mx).astype(BF16)
        pp = jnp.exp(lp - mx).astype(BF16)
        den = jnp.dot(pc, ones, preferred_element_type=F32) + jnp.dot(pp, ones, preferred_element_type=F32)
        den = den[:, :hd] + jnp.exp(sink - mx)
        o = jnp.dot(pc, vc, preferred_element_type=F32) + jnp.dot(pp, vp, preferred_element_type=F32)
        o = (o / den).astype(o_ref.dtype)
        for g, c in enumerate(cols):
            o_ref[:, c:c + hd] = o[g * w:(g + 1) * w]


def _attn_prompt(qkv, sinks, batch, seq, n_kv, n_grp, hd):
    aw = n_kv * n_grp * hd
    kvw = n_kv * hd
    assert seq % WINDOW == 0 and aw % kvw == 0
    nb = seq // WINDOW
    kcol = aw // kvw
    est = 2 * (_nbytes((WINDOW, aw), F32) + 4 * _nbytes((WINDOW, kvw), F32) + _nbytes((WINDOW, aw), BF16)) \
        + 16 * _nbytes((WINDOW, WINDOW), F32)
    kern = functools.partial(_attn_prompt_kernel, n_kv=n_kv, n_grp=n_grp, hd=hd)
    return pl.pallas_call(
        kern,
        out_shape=jax.ShapeDtypeStruct((batch * seq, aw), BF16),
        grid=(batch, nb),
        in_specs=[pl.BlockSpec(memory_space=pltpu.SMEM),
                  pl.BlockSpec((WINDOW, aw), lambda b, i: (b * nb + i, 0)),
                  pl.BlockSpec((WINDOW, kvw), lambda b, i: (b * nb + i, kcol)),
                  pl.BlockSpec((WINDOW, kvw), lambda b, i: (b * nb + jnp.maximum(i - 1, 0), kcol)),
                  pl.BlockSpec((WINDOW, kvw), lambda b, i: (b * nb + i, kcol + 1)),
                  pl.BlockSpec((WINDOW, kvw), lambda b, i: (b * nb + jnp.maximum(i - 1, 0), kcol + 1))],
        out_specs=pl.BlockSpec((WINDOW, aw), lambda b, i: (b * nb + i, 0)),
        compiler_params=_params(("arbitrary", "arbitrary"), est),
        name="attn_prompt",
    )(sinks, qkv, qkv, qkv, qkv, qkv)


def _attn_sample_kernel(q_ref, kn_ref, vn_ref, ck_ref, cv_ref, sinks_ref, o_ref, ok_ref, ov_ref,
                        *, n_kv, n_grp, hd):
    bs, w, kvw = ck_ref.shape
    nh = n_kv * n_grp
    scale = hd ** -0.5
    head_kv = lax.broadcasted_iota(jnp.int32, (nh, kvw), 0) // n_grp
    lane_kv = lax.broadcasted_iota(jnp.int32, (nh, kvw), 1) // hd
    own = head_kv == lane_kv
    row = lax.broadcasted_iota(jnp.int32, (w, kvw), 0)
    sinks = sinks_ref[...]
    dn = (((1,), (1,)), ((), ()))
    for b in range(bs):
        q = q_ref[b]
        qbd = jnp.where(own, jnp.concatenate([q] * n_kv, axis=-1), 0.0)
        ck = ck_ref[b]
        cv = cv_ref[b]
        kn = kn_ref[pl.ds(b, 1), :]
        vn = vn_ref[pl.ds(b, 1), :]
        logits = lax.dot_general(qbd.astype(BF16), ck.astype(BF16), dn,
                                 preferred_element_type=F32) * scale
        l_new = jnp.sum(qbd.astype(BF16).astype(F32) * kn.astype(BF16).astype(F32),
                        axis=-1, keepdims=True) * scale
        mx = jnp.maximum(jnp.maximum(jnp.max(logits, axis=-1, keepdims=True), l_new), sinks)
        p = jnp.exp(logits - mx)
        p_new = jnp.exp(l_new - mx)
        den = jnp.sum(p, axis=-1, keepdims=True) + p_new + jnp.exp(sinks - mx)
        r = jnp.dot(p.astype(BF16), cv.astype(BF16), preferred_element_type=F32) \
            + p_new.astype(BF16).astype(F32) * vn.astype(BF16).astype(F32)
        r = jnp.where(own, r / den, 0.0)
        o = r[:, 0:hd]
        for kv in range(1, n_kv):
            o = o + r[:, kv * hd:(kv + 1) * hd]
        o_ref[b] = o.astype(o_ref.dtype)
        ok_ref[b] = jnp.where(row == w - 1, kn, pltpu.roll(ck, shift=w - 1, axis=0))
        ov_ref[b] = jnp.where(row == w - 1, vn, pltpu.roll(cv, shift=w - 1, axis=0))


def _attn_sample(q3, qkv_s, cache_k, cache_v, sinks_col, n_kv, n_grp, hd):
    db, w, kvw = cache_k.shape
    nh = n_kv * n_grp
    aw = nh * hd
    kcol = aw // kvw
    bs = _pick_tile(db, 8, SUBLANES)
    est = 2 * (4 * _nbytes((bs, w, kvw), F32) + 2 * _nbytes((bs, kvw), F32) + 2 * _nbytes((bs, nh, LANES), F32)) \
        + 16 * _nbytes((w, kvw), F32)
    kern = functools.partial(_attn_sample_kernel, n_kv=n_kv, n_grp=n_grp, hd=hd)
    return pl.pallas_call(
        kern,
        out_shape=[jax.ShapeDtypeStruct((db, nh, hd), BF16),
                   jax.ShapeDtypeStruct((db, w, kvw), F32),
                   jax.ShapeDtypeStruct((db, w, kvw), F32)],
        grid=(db // bs,),
        in_specs=[pl.BlockSpec((bs, nh, hd), lambda i: (i, 0, 0)),
                  pl.BlockSpec((bs, kvw), lambda i: (i, kcol)),
                  pl.BlockSpec((bs, kvw), lambda i: (i, kcol + 1)),
                  pl.BlockSpec((bs, w, kvw), lambda i: (i, 0, 0)),
                  pl.BlockSpec((bs, w, kvw), lambda i: (i, 0, 0)),
                  pl.BlockSpec((nh, 1), lambda i: (0, 0))],
        out_specs=[pl.BlockSpec((bs, nh, hd), lambda i: (i, 0, 0)),
                   pl.BlockSpec((bs, w, kvw), lambda i: (i, 0, 0)),
                   pl.BlockSpec((bs, w, kvw), lambda i: (i, 0, 0))],
        compiler_params=_params(("arbitrary",), est),
        name="attn_sample",
    )(q3, qkv_s, qkv_s, cache_k, cache_v, sinks_col)


def _layer_norm(v, g, b):
    mu = jnp.mean(v, axis=-1, keepdims=True)
    c = v - mu
    var = jnp.mean(c * c, axis=-1, keepdims=True)
    return c * lax.rsqrt(var + EPS) * g + b


def _gmlp_prompt_kernel(u_ref, v_ref, g_ref, b_ref, ws_ref, bs_ref, o_ref, *, n_groups):
    ch = u_ref.shape[0]
    gd = u_ref.shape[1] // n_groups
    vn = _layer_norm(v_ref[...].astype(F32), g_ref[...], b_ref[...])
    ti = lax.broadcasted_iota(jnp.int32, (ch, ch), 0)
    si = lax.broadcasted_iota(jnp.int32, (ch, ch), 1)
    causal = si <= ti
    for g in range(n_groups):
        cs = slice(g * gd, (g + 1) * gd)
        wc = jnp.where(causal, ws_ref[g], 0.0).astype(BF16)
        mixed = jnp.dot(wc, vn[:, cs].astype(BF16), preferred_element_type=F32) + bs_ref[:, g:g + 1]
        o_ref[:, cs] = (u_ref[:, cs].astype(F32) * mixed).astype(o_ref.dtype)


def _gmlp_prompt(ugv, ln_g, ln_b, w_spatial, b_spatial_t):
    m, two_w = ugv.shape
    gmw = two_w // 2
    n_groups, ch, _ = w_spatial.shape
    assert m % ch == 0
    est = 2 * (2 * _nbytes((ch, gmw), ugv.dtype) + _nbytes((ch, gmw), BF16) + _nbytes(w_spatial.shape, F32)) \
        + 6 * _nbytes((ch, gmw), F32)
    kern = functools.partial(_gmlp_prompt_kernel, n_groups=n_groups)
    return pl.pallas_call(
        kern,
        out_shape=jax.ShapeDtypeStruct((m, gmw), BF16),
        grid=(m // ch,),
        in_specs=[pl.BlockSpec((ch, gmw), lambda i: (i, 0)),
                  pl.BlockSpec((ch, gmw), lambda i: (i, 1)),
                  pl.BlockSpec((1, gmw), lambda i: (0, 0)),
                  pl.BlockSpec((1, gmw), lambda i: (0, 0)),
                  pl.BlockSpec((n_groups, ch, ch), lambda i: (0, 0, 0)),
                  pl.BlockSpec((ch, n_groups), lambda i: (0, 0))],
        out_specs=pl.BlockSpec((ch, gmw), lambda i: (i, 0)),
        compiler_params=_params(("arbitrary",), est),
        name="gmlp_prompt",
    )(ugv, ugv, ln_g.reshape(1, gmw), ln_b.reshape(1, gmw), w_spatial, b_spatial_t)


def _gmlp_sample_kernel(u_ref, v_ref, g_ref, b_ref, w0_ref, b0_ref, o_ref, vn_ref):
    vn = _layer_norm(v_ref[...], g_ref[...], b_ref[...])
    vn_ref[...] = vn
    o_ref[...] = (u_ref[...] * (w0_ref[...] * vn + b0_ref[...])).astype(o_ref.dtype)


def _gmlp_sample(ugv, ln_g, ln_b, w00, b0):
    m, two_w = ugv.shape
    gmw = two_w // 2
    est = 2 * (3 * _nbytes((m, gmw), F32) + _nbytes((m, gmw), BF16)) + 4 * _nbytes((m, gmw), F32)
    row = pl.BlockSpec((1, gmw), lambda i: (0, 0))
    return pl.pallas_call(
        _gmlp_sample_kernel,
        out_shape=[jax.ShapeDtypeStruct((m, gmw), BF16), jax.ShapeDtypeStruct((m, gmw), F32)],
        grid=(1,),
        in_specs=[pl.BlockSpec((m, gmw), lambda i: (0, 0)), pl.BlockSpec((m, gmw), lambda i: (0, 1)),
                  row, row, row, row],
        out_specs=[pl.BlockSpec((m, gmw), lambda i: (0, 0)), pl.BlockSpec((m, gmw), lambda i: (0, 0))],
        compiler_params=_params(("arbitrary",), est),
        name="gmlp_sample",
    )(ugv, ugv, ln_g.reshape(1, gmw), ln_b.reshape(1, gmw), w00, b0)


def _pack_bf16_pair(left, right):
    lb = pltpu.bitcast(left.astype(BF16).astype(F32), jnp.uint32)
    rb = pltpu.bitcast(right.astype(BF16).astype(F32), jnp.uint32)
    return lb | (rb >> 16)


def _unpack_bf16_pair(word):
    left = pltpu.bitcast(word & jnp.uint32(0xFFFF0000), F32)
    right = pltpu.bitcast(word << 16, F32)
    return left, right


def _split_bf16(x):
    hi = x.astype(BF16)
    lo = (x - hi.astype(F32)).astype(BF16)
    return hi, lo


def _ffn_norm_kernel(x_ref, g_ref, sh_ref, sc_ref, wr_ref, h_ref, hp_ref, lg_ref):
    h = _rms_mod(x_ref[...], g_ref[...], sc_ref[0], sh_ref[0])
    half = h.shape[1] // 2
    h_ref[...] = h.astype(BF16)
    hp_ref[...] = _pack_bf16_pair(h[:, :half], h[:, half:])
    h_hi, h_lo = _split_bf16(h)
    w_hi, w_lo = _split_bf16(wr_ref[...])
    dn = (((1,), (1,)), ((), ()))
    lg_ref[...] = lax.dot_general(w_hi, h_hi, dn, preferred_element_type=F32) \
        + lax.dot_general(w_hi, h_lo, dn, preferred_element_type=F32) \
        + lax.dot_general(w_lo, h_hi, dn, preferred_element_type=F32)


def _ffn_norm(x2, g, mod3, shift_chunk, scale_chunk, group_rows, w_router_t):
    m, d = x2.shape
    e = w_router_t.shape[0]
    tm = _pick_tile(min(m, group_rows) if mod3.shape[1] == 1 else m, 256, LANES)
    row_of = lambda i: i
    est = 2 * (3 * _nbytes((tm, d), F32) + 2 * _nbytes((tm, d), BF16) + _nbytes((e, d), F32)) \
        + 6 * _nbytes((tm, d), F32)
    return pl.pallas_call(
        _ffn_norm_kernel,
        out_shape=[jax.ShapeDtypeStruct((m, d), BF16),
                   jax.ShapeDtypeStruct((m, d // 2), jnp.uint32),
                   jax.ShapeDtypeStruct((e, m), F32)],
        grid=(m // tm,),
        in_specs=[pl.BlockSpec((tm, d), lambda i: (i, 0)),
                  pl.BlockSpec((1, d), lambda i: (0, 0)),
                  _mod_spec(mod3, tm, group_rows, d, row_of, lambda i: shift_chunk),
                  _mod_spec(mod3, tm, group_rows, d, row_of, lambda i: scale_chunk),
                  pl.BlockSpec((e, d), lambda i: (0, 0))],
        out_specs=[pl.BlockSpec((tm, d), lambda i: (i, 0)),
                   pl.BlockSpec((tm, d // 2), lambda i: (i, 0)),
                   pl.BlockSpec((e, tm), lambda i: (0, i))],
        compiler_params=_params(("arbitrary",), est),
        name="ffn_norm_router",
    )(x2, g.reshape(1, d), mod3, mod3, w_router_t)


def _first_max(vals, idx):
    mx = jnp.max(vals, axis=0, keepdims=True)
    first = jnp.min(jnp.where(vals == mx, idx, jnp.int32(2 ** 30)), axis=0, keepdims=True)
    return mx, first


def _route_kernel(lg_ref, bias_ref, eidx_ref, ew_ref, rank_ref, cnt_ref, base_scr,
                  *, n_groups, topk_groups, top_k, scale):
    e, tr = lg_ref.shape
    per = e // n_groups

    @pl.when(pl.program_id(0) == 0)
    def _():
        base_scr[...] = jnp.zeros_like(base_scr)

    scores = jax.nn.sigmoid(lg_ref[...])
    biased = scores + bias_ref[...]
    eid = lax.broadcasted_iota(jnp.int32, (e, tr), 0)
    neg = jnp.float32(-jnp.inf)

    grp_rows = []
    bid = lax.broadcasted_iota(jnp.int32, (per, tr), 0)
    for g in range(n_groups):
        blk = biased[g * per:(g + 1) * per]
        m1, i1 = _first_max(blk, bid)
        m2 = jnp.max(jnp.where(bid == i1, neg, blk), axis=0, keepdims=True)
        grp_rows.append(m1 + m2)
    grp = jnp.concatenate(grp_rows, axis=0)
    gid = lax.broadcasted_iota(jnp.int32, (n_groups, tr), 0)
    gsel = jnp.zeros((n_groups, tr), F32)
    work = grp
    for _ in range(topk_groups):
        _, gi = _first_max(work, gid)
        hit = gid == gi
        gsel = jnp.where(hit, 1.0, gsel)
        work = jnp.where(hit, neg, work)
    emask = jnp.concatenate(
        [jnp.broadcast_to(gsel[g:g + 1], (per, tr)) for g in range(n_groups)], axis=0)
    masked = jnp.where(emask > 0.0, biased, neg)

    onehots, idxs, wts = [], [], []
    for _ in range(top_k):
        _, ei = _first_max(masked, eid)
        hit = eid == ei
        onehots.append(hit)
        idxs.append(ei)
        wts.append(jnp.sum(jnp.where(hit, scores, 0.0), axis=0, keepdims=True))
        masked = jnp.where(hit, neg, masked)
    wsum = wts[0]
    for wk in wts[1:]:
        wsum = wsum + wk

    chosen = onehots[0]
    for oh in onehots[1:]:
        chosen = jnp.logical_or(chosen, oh)
    chosen_f = jnp.where(chosen, 1.0, 0.0)
    si = lax.broadcasted_iota(jnp.int32, (tr, tr), 0)
    ti = lax.broadcasted_iota(jnp.int32, (tr, tr), 1)
    upper = jnp.where(si < ti, 1.0, 0.0).astype(BF16)
    prefix = jnp.dot(chosen_f.astype(BF16), upper, preferred_element_type=F32)
    pos = prefix + base_scr[:, 0:1]
    for k in range(top_k):
        eidx_ref[k:k + 1, :] = idxs[k]
        ew_ref[k:k + 1, :] = wts[k] / wsum * scale
        rank_ref[k:k + 1, :] = jnp.sum(jnp.where(onehots[k], pos, 0.0), axis=0, keepdims=True).astype(jnp.int32)
    for k in range(top_k, eidx_ref.shape[0]):
        eidx_ref[k:k + 1, :] = jnp.zeros((1, tr), jnp.int32)
        ew_ref[k:k + 1, :] = jnp.zeros((1, tr), F32)
        rank_ref[k:k + 1, :] = jnp.zeros((1, tr), jnp.int32)
    base_scr[...] = base_scr[...] + jnp.sum(chosen_f, axis=1, keepdims=True)
    cnt_ref[...] = base_scr[...].astype(jnp.int32)


def _route(logits_t, router_bias):
    e, n = logits_t.shape
    tr = _pick_tile(n, 640, LANES)
    rows = SUBLANES
    assert TOP_K <= rows and e % N_EXPERT_GROUPS == 0
    kern = functools.partial(_route_kernel, n_groups=N_EXPERT_GROUPS, topk_groups=TOPK_GROUPS,
                             top_k=TOP_K, scale=ROUTED_SCALE)
    est = 2 * (_nbytes((e, tr), F32) + 3 * _nbytes((rows, tr), F32)) + 24 * _nbytes((e, tr), F32) \
        + 3 * _nbytes((tr, tr), F32)
    out_row = pl.BlockSpec((rows, tr), lambda i: (0, i))
    return pl.pallas_call(
        kern,
        out_shape=[jax.ShapeDtypeStruct((rows, n), jnp.int32),
                   jax.ShapeDtypeStruct((rows, n), F32),
                   jax.ShapeDtypeStruct((rows, n), jnp.int32),
                   jax.ShapeDtypeStruct((e, LANES), jnp.int32)],
        grid=(n // tr,),
        in_specs=[pl.BlockSpec((e, tr), lambda i: (0, i)),
                  pl.BlockSpec((e, 1), lambda i: (0, 0))],
        out_specs=[out_row, out_row, out_row, pl.BlockSpec((e, LANES), lambda i: (0, 0))],
        scratch_shapes=[pltpu.VMEM((e, LANES), F32)],
        compiler_params=_params(("arbitrary",), est),
        name="route",
    )(logits_t, router_bias.reshape(e, 1))


def _dispatch_kernel(dest_ref, fill_ref, nfill_ref, hp_ref, xs_ref, zero_scr, sem, *, top_k, n_tok):
    step = pl.program_id(0)
    tt = hp_ref.shape[0]
    fill_rows = zero_scr.shape[0]

    @pl.when(step == 0)
    def _():
        zero_scr[...] = jnp.zeros_like(zero_scr)
        n_fill = nfill_ref[0]

        def start_fill(i, c):
            row0 = pl.multiple_of(fill_ref[i] * fill_rows, fill_rows)
            pltpu.make_async_copy(zero_scr, xs_ref.at[pl.ds(row0, fill_rows)], sem).start()
            return c

        def wait_fill(i, c):
            pltpu.make_async_copy(zero_scr, xs_ref.at[pl.ds(0, fill_rows)], sem).wait()
            return c

        lax.fori_loop(0, n_fill, start_fill, 0)
        lax.fori_loop(0, n_fill, wait_fill, 0)

    base = step * tt

    def start(r, c):
        for k in range(top_k):
            pltpu.make_async_copy(hp_ref.at[pl.ds(r, 1)],
                                  xs_ref.at[pl.ds(dest_ref[k * n_tok + base + r], 1)], sem
                                  ).start(priority=k % N_DMA_PRIORITIES)
        return c

    def wait(r, c):
        for k in range(top_k):
            pltpu.make_async_copy(hp_ref.at[pl.ds(r, 1)], xs_ref.at[pl.ds(0, 1)], sem).wait()
        return c

    lax.fori_loop(0, tt, start, 0)
    lax.fori_loop(0, tt, wait, 0)


def _dispatch(dest_flat, fill_tiles, n_fill, hp, n_rows, fill_rows):
    n_tok, half = hp.shape
    tt = _pick_tile(n_tok, 128, SUBLANES)
    kern = functools.partial(_dispatch_kernel, top_k=TOP_K, n_tok=n_tok)
    est = 2 * _nbytes((tt, half), hp.dtype) + _nbytes((fill_rows, half), hp.dtype)
    return pl.pallas_call(
        kern,
        out_shape=jax.ShapeDtypeStruct((n_rows, half), hp.dtype),
        grid_spec=pltpu.PrefetchScalarGridSpec(
            num_scalar_prefetch=3,
            grid=(n_tok // tt,),
            in_specs=[pl.BlockSpec((tt, half), lambda i, dest, fill, nfill: (i, 0))],
            out_specs=pl.BlockSpec(memory_space=pl.ANY),
            scratch_shapes=[pltpu.VMEM((fill_rows, half), hp.dtype), pltpu.SemaphoreType.DMA]),
        compiler_params=_params(("arbitrary",), est),
        name="moe_dispatch",
    )(dest_flat, fill_tiles, n_fill, hp)


def _stream_expert_weights(tile_e_ref, next_e_ref, flags_ref, blk_ref, w_refs, col_of, stage, sems):
    j = pl.program_id(0)
    t = pl.program_id(1)
    n_pass = pl.num_programs(0)
    width = stage.shape[-1]

    def copies(e, jj, slot):
        return [pltpu.make_async_copy(
            w_ref.at[e, :, pl.ds(pl.multiple_of(col(jj), width), width)], stage.at[slot, i], sems.at[slot])
            for i, (w_ref, col) in enumerate(zip(w_refs, col_of))]

    @pl.when(jnp.logical_and(j == 0, t == 0))
    def _():
        blk_ref[0] = 0
        for c in copies(tile_e_ref[0], 0, 0):
            c.start(priority=WEIGHT_STREAM_DMA_PRIORITY)

    slot = blk_ref[0] % 2
    for c in copies(tile_e_ref[t], j, slot):
        c.wait()
    in_pass = (flags_ref[t] & TILE_HAS_NEXT) > 0
    e_next = jnp.where(in_pass, next_e_ref[t], tile_e_ref[0])
    j_next = jnp.where(in_pass, j, j + 1)

    @pl.when(jnp.logical_or(in_pass, j + 1 < n_pass))
    def _():
        for c in copies(e_next, j_next, 1 - slot):
            c.start(priority=WEIGHT_STREAM_DMA_PRIORITY)

    blk_ref[0] = blk_ref[0] + 1


def _expert_up_kernel(tile_e_ref, used_ref, next_e_ref, flags_ref, xs_ref, wg_ref, wu_ref, o_ref,
                      stage, sems, blk_ref):
    t = pl.program_id(1)
    n_used = used_ref[0]
    fc = o_ref.shape[1]

    @pl.when(t < n_used)
    def _():
        prev = tile_e_ref[jnp.maximum(t - 1, 0)]
        fresh = jnp.logical_or(t == 0, tile_e_ref[t] != prev)

        @pl.when(fresh)
        def _():
            col = lambda jj: jj * fc
            _stream_expert_weights(tile_e_ref, next_e_ref, flags_ref, blk_ref, [wg_ref, wu_ref], [col, col],
                                   stage, sems)

        half = xs_ref.shape[1]
        tm = xs_ref.shape[0]
        slot = (blk_ref[0] + 1) % 2

        def project(rows):
            xl, xr = _unpack_bf16_pair(xs_ref[:rows])
            xl = xl.astype(BF16)
            xr = xr.astype(BF16)
            gate = jnp.dot(xl, stage[slot, 0, :half].astype(BF16), preferred_element_type=F32) \
                + jnp.dot(xr, stage[slot, 0, half:].astype(BF16), preferred_element_type=F32)
            up = jnp.dot(xl, stage[slot, 1, :half].astype(BF16), preferred_element_type=F32) \
                + jnp.dot(xr, stage[slot, 1, half:].astype(BF16), preferred_element_type=F32)
            o_ref[:rows] = (_silu(gate) * up).astype(o_ref.dtype)
            if rows < tm:
                o_ref[rows:] = jnp.zeros((tm - rows, fc), o_ref.dtype)

        half_full = (flags_ref[t] & TILE_HALF_FULL) > 0

        @pl.when(half_full)
        def _():
            project(tm // 2)

        @pl.when(jnp.logical_not(half_full))
        def _():
            project(tm)

    @pl.when(t >= n_used)
    def _():
        o_ref[...] = jnp.zeros_like(o_ref)


def _expert_up(tile_e, n_used, next_e, tile_flags, xs, w_gate, w_up, tm, fc):
    p, half = xs.shape
    e, d, de = w_gate.shape
    n_t = p // tm
    clamp = lambda t, used: jnp.minimum(t, used[0] - 1)
    est = 2 * (_nbytes((tm, half), jnp.uint32) + 2 * _nbytes((d, fc), F32)
               + _nbytes((tm, fc), BF16)) + 2 * _nbytes((d, fc), BF16) + 2 * _nbytes((tm, d), BF16) \
        + 4 * _nbytes((tm, fc), F32) + 2 * _nbytes((tm, half), F32)
    return pl.pallas_call(
        _expert_up_kernel,
        out_shape=jax.ShapeDtypeStruct((p, de), BF16),
        grid_spec=pltpu.PrefetchScalarGridSpec(
            num_scalar_prefetch=4,
            grid=(de // fc, n_t),
            in_specs=[pl.BlockSpec((tm, half), lambda j, t, te, used, ne, hn: (clamp(t, used), 0)),
                      pl.BlockSpec(memory_space=pl.ANY),
                      pl.BlockSpec(memory_space=pl.ANY)],
            out_specs=pl.BlockSpec((tm, fc), lambda j, t, te, used, ne, hn: (t, j)),
            scratch_shapes=[pltpu.VMEM((2, 2, d, fc), F32),
                            pltpu.SemaphoreType.DMA((2,)), pltpu.SMEM((1,), jnp.int32)]),
        compiler_params=_params(("arbitrary", "arbitrary"), est),
        name="expert_up",
    )(tile_e, n_used, next_e, tile_flags, xs, w_gate, w_up)


def _expert_down_kernel(tile_e_ref, used_ref, next_e_ref, flags_ref, a_ref, wd_ref, o_ref,
                        stage, sems, blk_ref):
    t = pl.program_id(1)
    n_used = used_ref[0]
    nc = o_ref.shape[1]
    half = wd_ref.shape[2] // 2

    @pl.when(t < n_used)
    def _():
        prev = tile_e_ref[jnp.maximum(t - 1, 0)]
        fresh = jnp.logical_or(t == 0, tile_e_ref[t] != prev)

        @pl.when(fresh)
        def _():
            _stream_expert_weights(tile_e_ref, next_e_ref, flags_ref, blk_ref, [wd_ref, wd_ref],
                                   [lambda cc: cc * nc, lambda cc: half + cc * nc],
                                   stage, sems)

        tm = a_ref.shape[0]
        slot = (blk_ref[0] + 1) % 2

        def project(rows):
            a = a_ref[:rows]
            yl = jnp.dot(a, stage[slot, 0].astype(BF16), preferred_element_type=F32)
            yr = jnp.dot(a, stage[slot, 1].astype(BF16), preferred_element_type=F32)
            o_ref[:rows] = _pack_bf16_pair(yl, yr)
            if rows < tm:
                o_ref[rows:] = jnp.zeros((tm - rows, nc), o_ref.dtype)

        half_full = (flags_ref[t] & TILE_HALF_FULL) > 0

        @pl.when(half_full)
        def _():
            project(tm // 2)

        @pl.when(jnp.logical_not(half_full))
        def _():
            project(tm)

    @pl.when(t >= n_used)
    def _():
        o_ref[...] = jnp.zeros_like(o_ref)


def _expert_down(tile_e, n_used, next_e, tile_flags, act, w_down, tm, nc):
    p, de = act.shape
    e, _, d = w_down.shape
    half = d // 2
    n_t = p // tm
    n_c = half // nc
    clamp = lambda t, used: jnp.minimum(t, used[0] - 1)
    est = 2 * (_nbytes((tm, de), BF16) + 2 * _nbytes((de, nc), F32) + _nbytes((tm, nc), jnp.uint32)) \
        + 2 * _nbytes((de, nc), BF16) + 3 * _nbytes((tm, nc), F32)
    return pl.pallas_call(
        _expert_down_kernel,
        out_shape=jax.ShapeDtypeStruct((p, half), jnp.uint32),
        grid_spec=pltpu.PrefetchScalarGridSpec(
            num_scalar_prefetch=4,
            grid=(n_c, n_t),
            in_specs=[pl.BlockSpec((tm, de), lambda c, t, te, used, ne, hn: (clamp(t, used), 0)),
                      pl.BlockSpec(memory_space=pl.ANY)],
            out_specs=pl.BlockSpec((tm, nc), lambda c, t, te, used, ne, hn: (t, c)),
            scratch_shapes=[pltpu.VMEM((2, 2, de, nc), F32),
                            pltpu.SemaphoreType.DMA((2,)), pltpu.SMEM((1,), jnp.int32)]),
        compiler_params=_params(("arbitrary", "arbitrary"), est),
        name="expert_down",
    )(tile_e, n_used, next_e, tile_flags, act, w_down)


def _combine_kernel(dest_ref, ys_ref, x_ref, sh_ref, gf_ref, ew_ref, o_ref, buf, sem, *, top_k, n_tok, tok0):
    tt = x_ref.shape[0]
    half = ys_ref.shape[1]
    step = pl.program_id(0)
    slot = step % 2

    def gather(for_step, into, start):
        base = tok0 + for_step * tt

        def body(r, c):
            for k in range(top_k):
                row = dest_ref[k * n_tok + base + r] if start else 0
                cp = pltpu.make_async_copy(ys_ref.at[pl.ds(row, 1)], buf.at[into, k, pl.ds(r, 1)], sem.at[into])
                if start:
                    cp.start(priority=k % N_DMA_PRIORITIES)
                else:
                    cp.wait()
            return c

        lax.fori_loop(0, tt, body, 0)

    @pl.when(step == 0)
    def _():
        gather(0, 0, True)

    @pl.when(step + 1 < pl.num_programs(0))
    def _():
        gather(step + 1, 1 - slot, True)

    gather(step, slot, False)
    per_row_gate = gf_ref.shape[1] != 1

    def reduce_rows(c, carry):
        rows = pl.ds(pl.multiple_of(c * SUBLANES, SUBLANES), SUBLANES)
        ew = ew_ref[rows, :]
        gf = gf_ref[0, rows, :] if per_row_gate else gf_ref[0]
        for side, cols in ((0, slice(0, half)), (1, slice(half, 2 * half))):
            acc = None
            for k in range(top_k):
                term = _unpack_bf16_pair(buf[slot, k, rows, :])[side] * ew[:, k:k + 1]
                acc = term if acc is None else acc + term
            o_ref[rows, cols] = x_ref[rows, cols] + gf[:, cols] * (acc + sh_ref[rows, cols])
        return carry

    lax.fori_loop(0, tt // SUBLANES, reduce_rows, 0)


def _combine(dest_flat, ys, x2, shared, mod3, ew_t, gate_chunk, group_rows, n_tok, tok0):
    m, d = x2.shape
    half = d // 2
    ew_rows = ew_t.shape[1]
    tt = _pick_tile(min(m, group_rows) if mod3.shape[1] == 1 else m, 128, SUBLANES)
    sh_blk0 = tok0 // tt
    assert tok0 % tt == 0
    row_of = lambda i, dest: i
    kern = functools.partial(_combine_kernel, top_k=TOP_K, n_tok=n_tok, tok0=tok0)
    est = 2 * (3 * _nbytes((tt, d), F32) + _nbytes((tt, d), F32)) + 2 * _nbytes((TOP_K, tt, half), jnp.uint32) \
        + 6 * _nbytes((tt, d), F32)
    return pl.pallas_call(
        kern,
        out_shape=jax.ShapeDtypeStruct((m, d), F32),
        grid_spec=pltpu.PrefetchScalarGridSpec(
            num_scalar_prefetch=1,
            grid=(m // tt,),
            in_specs=[pl.BlockSpec(memory_space=pl.ANY),
                      pl.BlockSpec((tt, d), lambda i, dest: (i, 0)),
                      pl.BlockSpec((tt, d), lambda i, dest: (i + sh_blk0, 0)),
                      _mod_spec(mod3, tt, group_rows, d, row_of, lambda i, dest: gate_chunk),
                      pl.BlockSpec((tt, ew_rows), lambda i, dest: (i + sh_blk0, 0))],
            out_specs=pl.BlockSpec((tt, d), lambda i, dest: (i, 0)),
            scratch_shapes=[pltpu.VMEM((2, TOP_K, tt, half), jnp.uint32), pltpu.SemaphoreType.DMA((2,))]),
        compiler_params=_params(("arbitrary",), est),
        name="moe_combine",
    )(dest_flat, ys, x2, shared, mod3, ew_t)


def _token_mixing(x2, mod3, group_rows, p, attend):
    m, d = x2.shape
    aw, kvw, gmw, hd = p["aw"], p["kvw"], p["gmw"], p["hd"]
    tn = 512
    assert aw % tn == 0 and (2 * kvw) % tn == 0 and gmw % tn == 0 and d % tn == 0
    tm = _pick_tile(min(m, group_rows) if mod3.shape[1] == 1 else m, 1024, 16)
    h = _modulate(x2, p["norm_mix_g"], mod3, 0, 1, group_rows)
    w_in = p["w_in"]

    qkv_w = aw + 2 * kvw

    def qkv_epilogue(accs, extras, n, mi):
        gain_ref, flag_ref, bd_ref = extras
        z = accs[0]
        sq_hi, sq_lo = _split_bf16(z * z)
        ss = jnp.dot(sq_hi, bd_ref[...], preferred_element_type=F32) \
            + jnp.dot(sq_lo, bd_ref[...], preferred_element_type=F32)
        inv = lax.rsqrt(ss * (1.0 / hd) + EPS)
        return [z * jnp.where(flag_ref[...] > 0.0, inv, 1.0) * gain_ref[...]]

    qkv = _mm([(h, d, 0)], [(w_in, 0)],
              [(p["qkv_gain"], pl.BlockSpec((1, tn), lambda n, mi: (0, n))),
               (p["qkv_flag"], pl.BlockSpec((1, tn), lambda n, mi: (0, n))),
               (p["head_ones"], pl.BlockSpec((tn, tn), lambda n, mi: (0, 0)))],
              qkv_epilogue,
              [(jax.ShapeDtypeStruct((m, qkv_w), F32), pl.BlockSpec((tm, tn), lambda n, mi: (mi, n)))],
              m=m, tm=tm, tn=tn, n_tiles=qkv_w // tn, pairs=[(0, 0)], name="in_proj_qkv")[0]

    ugv_dtype = BF16 if mod3.shape[1] == 1 else F32
    ugv = _mm([(h, d, 0)], [(w_in, qkv_w // tn)], [],
              lambda accs, extras, n, mi: [_gelu(accs[0])],
              [(jax.ShapeDtypeStruct((m, 2 * gmw), ugv_dtype), pl.BlockSpec((tm, tn), lambda n, mi: (mi, n)))],
              m=m, tm=tm, tn=tn, n_tiles=2 * gmw // tn, pairs=[(0, 0)], name="in_proj_gmlp")[0]

    gates = _mm([(h, d, 0)], [(w_in, (qkv_w + 2 * gmw) // tn)], [],
                lambda accs, extras, n, mi: [jax.nn.sigmoid(accs[0])],
                [(jax.ShapeDtypeStruct((m, 2 * d), BF16), pl.BlockSpec((tm, tn), lambda n, mi: (mi, n)))],
                m=m, tm=tm, tn=tn, n_tiles=2 * d // tn, pairs=[(0, 0)], name="in_proj_gates")[0]

    o_attn, o_gmlp, aux = attend(qkv, ugv)

    nd = d // tn
    merged = _mm([(o_attn, aw, 0), (o_gmlp, gmw, 0)], [(p["w_branch_attn"], 0), (p["w_branch_gmlp"], 0)],
                 [(gates, pl.BlockSpec((tm, tn), lambda n, mi: (mi, n))),
                  (gates, pl.BlockSpec((tm, tn), lambda n, mi: (mi, n + nd)))],
                 lambda accs, extras, n, mi: [extras[0][...].astype(F32) * accs[0]
                                              + extras[1][...].astype(F32) * accs[1]],
                 [(jax.ShapeDtypeStruct((m, d), BF16), pl.BlockSpec((tm, tn), lambda n, mi: (mi, n)))],
                 m=m, tm=tm, tn=tn, n_tiles=nd, pairs=[(0, 0), (1, 1)], name="branch_merge")[0]

    x1 = _mm([(merged, d, 0)], [(p["w_out"], 0)],
             [(x2, pl.BlockSpec((tm, tn), lambda n, mi: (mi, n))),
              (mod3, _mod_spec(mod3, tm, group_rows, tn, lambda n, mi: mi, lambda n, mi: 2 * nd + n))],
             lambda accs, extras, n, mi: [extras[0][...] + extras[1][0] * accs[0]],
             [(jax.ShapeDtypeStruct((m, d), F32), pl.BlockSpec((tm, tn), lambda n, mi: (mi, n)))],
             m=m, tm=tm, tn=tn, n_tiles=nd, pairs=[(0, 0)], name="out_proj")[0]
    return x1, qkv, aux


def kernel(x_prompt, x_sample, cache_k_win, cache_v_win, c_prompt, c_sample, norm_mix_g, norm_ffn_g,
           w_ada, b_ada, w_in, q_norm_g, k_norm_g, attn_sinks, gm_ln_g, gm_ln_b, w_spatial, b_spatial,
           w_branch_attn, w_branch_gmlp, w_out, w_router, router_bias, w_gate, w_up, w_down,
           ws_gate, ws_up, ws_down):
    depth = norm_mix_g.shape[0]
    assert depth == 1, "single-layer step"
    batch, seq, d = x_prompt.shape
    db, t_new, _ = x_sample.shape
    assert t_new == 1, "one new token per sequence"
    _, _, win, n_kv, hd = cache_k_win.shape
    n_grp = attn_sinks.shape[-1]
    aw, kvw = n_kv * n_grp * hd, n_kv * hd
    gmw = gm_ln_g.shape[-1]
    n_groups, ch, _ = w_spatial.shape[1:]
    n_exp = w_router.shape[-1]
    de = w_gate.shape[-1]
    ds = ws_gate.shape[-1]
    tn = 512
    l = 0

    n_c = batch + db
    rows = -(-n_c // 16) * 16
    c_all = jnp.concatenate([c_prompt, c_sample, jnp.zeros((rows - n_c, d), F32)], axis=0)
    mods = _adaln(c_all, w_ada[l], b_ada[l])
    mod_p = mods[:batch].reshape(batch, 1, 6 * d)
    mod_s = mods[batch:n_c].reshape(1, db, 6 * d)

    gq = jnp.tile(q_norm_g[l], aw // hd)
    gk = jnp.tile(k_norm_g[l], kvw // hd)
    qkv_gain = jnp.concatenate([gq, gk, jnp.ones((kvw,), F32)]).reshape(1, -1)
    qkv_flag = jnp.concatenate([jnp.ones((aw + kvw,), F32), jnp.zeros((kvw,), F32)]).reshape(1, -1)
    hid = jnp.arange(tn) // hd
    head_ones = (hid[:, None] == hid[None, :]).astype(BF16)
    params = dict(aw=aw, kvw=kvw, gmw=gmw, hd=hd, norm_mix_g=norm_mix_g[l], w_in=w_in[l],
                  qkv_gain=qkv_gain, qkv_flag=qkv_flag, head_ones=head_ones,
                  w_branch_attn=w_branch_attn[l], w_branch_gmlp=w_branch_gmlp[l], w_out=w_out[l])
    sinks = attn_sinks[l]

    def attend_prompt(qkv, ugv):
        o_attn = _attn_prompt(qkv, sinks, batch, seq, n_kv, n_grp, hd)
        o_gmlp = _gmlp_prompt(ugv, gm_ln_g[l], gm_ln_b[l], w_spatial[l], b_spatial[l].T)
        return o_attn, o_gmlp, None

    xp2 = x_prompt.reshape(batch * seq, d)
    x1_p, qkv_p, _ = _token_mixing(xp2, mod_p, seq, params, attend_prompt)

    ck = cache_k_win[l].reshape(db, win, kvw)
    cv = cache_v_win[l].reshape(db, win, kvw)

    def attend_sample(qkv, ugv):
        q3 = qkv[:, :aw].reshape(db, n_kv * n_grp, hd)
        o3, new_k, new_v = _attn_sample(q3, qkv, ck, cv, sinks.reshape(-1, 1), n_kv, n_grp, hd)
        w00 = jnp.repeat(w_spatial[l][:, 0, 0], gmw // n_groups).reshape(1, gmw)
        b0 = jnp.repeat(b_spatial[l][:, 0], gmw // n_groups).reshape(1, gmw)
        o_gmlp, vn = _gmlp_sample(ugv, gm_ln_g[l], gm_ln_b[l], w00, b0)
        return o3.reshape(db, aw), o_gmlp, (new_k, new_v, vn)

    xs2 = x_sample.reshape(db, d)
    x1_s, _, (new_k_s, new_v_s, vn_s) = _token_mixing(xs2, mod_s, 1, params, attend_sample)

    w_router_t = w_router[l].T
    h2_p, hp_p, lg_p = _ffn_norm(x1_p, norm_ffn_g[l], mod_p, 3, 4, seq, w_router_t)
    h2_s, hp_s, lg_s = _ffn_norm(x1_s, norm_ffn_g[l], mod_s, 3, 4, 1, w_router_t)
    h2 = jnp.concatenate([h2_p, h2_s], axis=0)
    hp = jnp.concatenate([hp_p, hp_s], axis=0)
    logits_t = jnp.concatenate([lg_p, lg_s], axis=1)
    n_tok = h2.shape[0]

    eidx8, ew8, rank8, counts = _route(logits_t, router_bias[l])
    eidx, ew, rank = eidx8[:TOP_K], ew8[:TOP_K], rank8[:TOP_K]
    counts = counts[:, 0]

    tm_e = 256
    n_assign = n_tok * TOP_K
    n_tiles = -(-n_assign // tm_e) + n_exp
    n_rows = n_tiles * tm_e
    padded = (counts + tm_e - 1) // tm_e * tm_e
    pad_end = jnp.cumsum(padded)
    pad_start = pad_end - padded
    expert_ids = jnp.arange(n_exp, dtype=jnp.int32)
    dest = rank + jnp.sum(jnp.where(eidx[:, :, None] == expert_ids, pad_start.astype(jnp.int32), 0), axis=-1)
    dest_flat = dest.reshape(-1)
    tile_ids = jnp.arange(n_tiles, dtype=jnp.int32)
    tile_e = jnp.minimum(jnp.sum((pad_end[None, :] <= tile_ids[:, None] * tm_e).astype(jnp.int32), axis=1),
                         n_exp - 1).astype(jnp.int32)
    used_tiles = (pad_end[-1:] // tm_e).astype(jnp.int32)
    pad_fill = jnp.maximum(pad_end // tm_e - 1, 0).astype(jnp.int32)
    tail_fill = jnp.minimum(used_tiles + tile_ids, n_tiles - 1)
    fill_tiles = jnp.concatenate([pad_fill, tail_fill])
    n_fill = (n_exp + n_tiles - used_tiles).astype(jnp.int32)

    seg_end_tile = (pad_end // tm_e).astype(jnp.int32)[tile_e]
    next_e = tile_e[jnp.minimum(seg_end_tile, n_tiles - 1)]
    rows_in_tile = counts[tile_e] - (tile_ids - (pad_start // tm_e).astype(jnp.int32)[tile_e]) * tm_e
    tile_flags = jnp.where(seg_end_tile < used_tiles, TILE_HAS_NEXT, 0) \
        + jnp.where(rows_in_tile <= tm_e // 2, TILE_HALF_FULL, 0)
    tile_flags = tile_flags.astype(jnp.int32)

    xs = _dispatch(dest_flat, fill_tiles, n_fill, hp, n_rows, tm_e)
    act = _expert_up(tile_e, used_tiles, next_e, tile_flags, xs, w_gate[l], w_up[l], tm_e,
                     _pick_tile(de, 512, LANES))
    ys = _expert_down(tile_e, used_tiles, next_e, tile_flags, act, w_down[l], tm_e,
                      _pick_tile(d // 2, 2048, LANES))

    tm_s = _pick_tile(n_tok, 640, 16)
    tn_s = _pick_tile(ds, 256, LANES)
    sh_act = _mm([(h2, d, 0)], [(ws_gate[l], 0), (ws_up[l], 0)], [],
                 lambda accs, extras, n, mi: [_silu(accs[0]) * accs[1]],
                 [(jax.ShapeDtypeStruct((n_tok, ds), BF16), pl.BlockSpec((tm_s, tn_s), lambda n, mi: (mi, n)))],
                 m=n_tok, tm=tm_s, tn=tn_s, n_tiles=ds // tn_s, pairs=[(0, 0), (0, 1)], name="shared_up")[0]
    shared = _mm([(sh_act, ds, 0)], [(ws_down[l], 0)], [],
                 lambda accs, extras, n, mi: [accs[0]],
                 [(jax.ShapeDtypeStruct((n_tok, d), F32), pl.BlockSpec((tm_s, tn), lambda n, mi: (mi, n)))],
                 m=n_tok, tm=tm_s, tn=tn, n_tiles=d // tn, pairs=[(0, 0)], name="shared_down")[0]

    ew_t = ew8.T
    y_p = _combine(dest_flat, ys, x1_p, shared, mod_p, ew_t, 5, seq, n_tok, 0)
    y_s = _combine(dest_flat, ys, x1_s, shared, mod_s, ew_t, 5, 1, n_tok, batch * seq)

    w_keep = min(WINDOW, seq)
    qkv_p3 = qkv_p.reshape(batch, seq, aw + 2 * kvw)
    new_k_p = qkv_p3[:, seq - w_keep:, aw:aw + kvw].reshape(1, batch, w_keep, n_kv, hd)
    new_v_p = qkv_p3[:, seq - w_keep:, aw + kvw:].reshape(1, batch, w_keep, n_kv, hd)
    return (y_p.reshape(batch, seq, d), y_s.reshape(db, 1, d), new_k_p, new_v_p,
            new_k_s.reshape(1, db, win, n_kv, hd), new_v_s.reshape(1, db, win, n_kv, hd),
            vn_s.reshape(1, db, 1, gmw))
```

```python
import functools

import jax
import jax.numpy as jnp
from jax import lax
from jax.experimental import pallas as pl
from jax.experimental.pallas import tpu as pltpu

TOP_K = 6
N_EXPERT_GROUPS = 8
TOPK_GROUPS = 4
ROUTED_SCALE = 2.5
WINDOW = 128
EPS = 1e-6

TILE_HAS_NEXT = 1
TILE_HALF_FULL = 2

V7X_VMEM_BYTES = 64 * 1024 * 1024
V7X_VMEM_REQUEST_CAP = 56 * 1024 * 1024
LANES = 128
SUBLANES = 8
N_DMA_PRIORITIES = 2
WEIGHT_STREAM_DMA_PRIORITY = 1

BF16 = jnp.bfloat16
F32 = jnp.float32


def _pick_tile(total, preferred, multiple):
    t = min(preferred, total)
    t -= t % multiple
    while t > multiple and total % t:
        t -= multiple
    assert t > 0 and total % t == 0, (total, preferred, multiple)
    return t


def _params(semantics, vmem_bytes):
    return pltpu.CompilerParams(
        dimension_semantics=semantics,
        vmem_limit_bytes=int(min(V7X_VMEM_REQUEST_CAP, max(vmem_bytes, 16 * 1024 * 1024))))


def _nbytes(shape, dtype):
    n = 1
    for s in shape:
        n *= s
    return n * jnp.dtype(dtype).itemsize


def _mod_spec(mod3, tm, group_rows, width, row_of, col_of):
    if mod3.shape[1] == 1:
        return pl.BlockSpec((1, 1, width), lambda *g: ((row_of(*g) * tm) // group_rows, 0, col_of(*g)))
    return pl.BlockSpec((1, tm, width), lambda *g: (0, row_of(*g), col_of(*g)))


def _adaln_kernel(c_ref, w_ref, b_ref, o_ref, a_scr):
    @pl.when(pl.program_id(0) == 0)
    def _():
        c = c_ref[...]
        a_scr[...] = (c * jax.nn.sigmoid(c)).astype(BF16)

    acc = jnp.dot(a_scr[...], w_ref[...].astype(BF16), preferred_element_type=F32)
    o_ref[...] = acc + b_ref[...]


def _adaln(c_all, w_ada, b_ada):
    rows, d = c_all.shape
    n_out = w_ada.shape[1]
    tn = _pick_tile(n_out, 512, LANES)
    est = 2 * (_nbytes((d, tn), F32) + _nbytes((rows, tn), F32)) + _nbytes((rows, d), F32) * 2 \
        + _nbytes((rows, d), BF16) + _nbytes((d, tn), BF16) + _nbytes((rows, tn), F32)
    return pl.pallas_call(
        _adaln_kernel,
        out_shape=jax.ShapeDtypeStruct((rows, n_out), F32),
        grid=(n_out // tn,),
        in_specs=[pl.BlockSpec((rows, d), lambda n: (0, 0)),
                  pl.BlockSpec((d, tn), lambda n: (0, n)),
                  pl.BlockSpec((1, tn), lambda n: (0, n))],
        out_specs=pl.BlockSpec((rows, tn), lambda n: (0, n)),
        scratch_shapes=[pltpu.VMEM((rows, d), BF16)],
        compiler_params=_params(("arbitrary",), est),
        name="adaln",
    )(c_all, w_ada, b_ada.reshape(1, n_out))


def _rms_mod(x, g, scale, shift):
    y = x * lax.rsqrt(jnp.mean(x * x, axis=-1, keepdims=True) + EPS)
    return (y * g) * (1.0 + scale) + shift


def _modulate_kernel(x_ref, g_ref, sh_ref, sc_ref, o_ref):
    o_ref[...] = _rms_mod(x_ref[...], g_ref[...], sc_ref[0], sh_ref[0]).astype(o_ref.dtype)


def _modulate(x2, g, mod3, shift_chunk, scale_chunk, group_rows):
    m, d = x2.shape
    tm = _pick_tile(min(m, group_rows) if mod3.shape[1] == 1 else m, 256, SUBLANES)
    row_of = lambda i: i
    est = 2 * (_nbytes((tm, d), F32) * 3 + _nbytes((tm, d), BF16)) + 4 * _nbytes((tm, d), F32)
    return pl.pallas_call(
        _modulate_kernel,
        out_shape=jax.ShapeDtypeStruct((m, d), BF16),
        grid=(m // tm,),
        in_specs=[pl.BlockSpec((tm, d), lambda i: (i, 0)),
                  pl.BlockSpec((1, d), lambda i: (0, 0)),
                  _mod_spec(mod3, tm, group_rows, d, row_of, lambda i: shift_chunk),
                  _mod_spec(mod3, tm, group_rows, d, row_of, lambda i: scale_chunk)],
        out_specs=pl.BlockSpec((tm, d), lambda i: (i, 0)),
        compiler_params=_params(("arbitrary",), est),
        name="modulate",
    )(x2, g.reshape(1, d), mod3, mod3)


def _mm_kernel(*refs, n_a, n_w, n_extra, n_out, pairs, epilogue):
    a_refs = refs[:n_a]
    w_refs = refs[n_a:n_a + n_w]
    extra_refs = refs[n_a + n_w:n_a + n_w + n_extra]
    out_refs = refs[n_a + n_w + n_extra:n_a + n_w + n_extra + n_out]
    w_scr = refs[n_a + n_w + n_extra + n_out:]
    n = pl.program_id(0)
    mi = pl.program_id(1)

    @pl.when(mi == 0)
    def _():
        for w_ref, scr in zip(w_refs, w_scr):
            scr[...] = w_ref[...].astype(BF16)

    accs = [jnp.dot(a_refs[ia][...], w_scr[iw][...], preferred_element_type=F32) for ia, iw in pairs]
    outs = epilogue(accs, extra_refs, n, mi)
    for o_ref, val in zip(out_refs, outs):
        o_ref[...] = val.astype(o_ref.dtype)


def _mm(a_list, w_list, extras, epilogue, outs, *, m, tm, tn, n_tiles, pairs, name):
    in_specs, args, est = [], [], 0
    for arr, kw, cb in a_list:
        in_specs.append(pl.BlockSpec((tm, kw), lambda n, mi, cb=cb: (mi, cb)))
        args.append(arr)
        est += 2 * _nbytes((tm, kw), arr.dtype)
    scratch = []
    for arr, off in w_list:
        k = arr.shape[0]
        in_specs.append(pl.BlockSpec((k, tn), lambda n, mi, off=off: (0, n + off)))
        args.append(arr)
        scratch.append(pltpu.VMEM((k, tn), BF16))
        est += 2 * _nbytes((k, tn), F32) + _nbytes((k, tn), BF16)
    for arr, spec in extras:
        in_specs.append(spec)
        args.append(arr)
        est += 2 * _nbytes(spec.block_shape, arr.dtype)
    for sds, spec in outs:
        est += 2 * _nbytes(spec.block_shape, sds.dtype)
    est += (len(pairs) + 2) * _nbytes((tm, tn), F32)
    kern = functools.partial(_mm_kernel, n_a=len(a_list), n_w=len(w_list), n_extra=len(extras),
                             n_out=len(outs), pairs=tuple(pairs), epilogue=epilogue)
    res = pl.pallas_call(
        kern,
        out_shape=[sds for sds, _ in outs],
        grid=(n_tiles, m // tm),
        in_specs=in_specs,
        out_specs=[spec for _, spec in outs],
        scratch_shapes=scratch,
        compiler_params=_params(("arbitrary", "arbitrary"), est),
        name=name,
    )(*args)
    return res


def _gelu(x):
    return x * (lax.erf(x * (2.0 ** -0.5)) + 1.0) * 0.5


def _silu(x):
    return x * jax.nn.sigmoid(x)


def _attn_prompt_kernel(sinks_ref, q_ref, kc_ref, kp_ref, vc_ref, vp_ref, o_ref, *, n_kv, n_grp, hd):
    i = pl.program_id(1)
    w = q_ref.shape[0]
    scale = hd ** -0.5
    rows = n_grp * w
    qi = lax.broadcasted_iota(jnp.int32, (rows, w), 0) % w
    sj = lax.broadcasted_iota(jnp.int32, (rows, w), 1)
    mask_c = sj <= qi
    mask_p = jnp.logical_and(sj >= qi, i > 0)
    ones = jnp.ones((w, w), BF16)
    dn = (((1,), (1,)), ((), ()))
    for kv in range(n_kv):
        ks = slice(kv * hd, (kv + 1) * hd)
        kc = kc_ref[:, ks].astype(BF16)
        kp = kp_ref[:, ks].astype(BF16)
        vc = vc_ref[:, ks].astype(BF16)
        vp = vp_ref[:, ks].astype(BF16)
        cols = [(kv * n_grp + g) * hd for g in range(n_grp)]
        q = jnp.concatenate([q_ref[:, c:c + hd] for c in cols], axis=0).astype(BF16)
        sink = jnp.concatenate([jnp.full((w, 1), sinks_ref[kv, g], F32) for g in range(n_grp)], axis=0)
        lc = jnp.where(mask_c, lax.dot_general(q, kc, dn, preferred_element_type=F32) * scale, -jnp.inf)
        lp = jnp.where(mask_p, lax.dot_general(q, kp, dn, preferred_element_type=F32) * scale, -jnp.inf)
        mx = jnp.maximum(jnp.max(jnp.maximum(lc, lp), axis=-1, keepdims=True), sink)
        pc = jnp.exp(lc - mx).astype(BF16)
        pp = jnp.exp(lp - mx).astype(BF16)
        den = jnp.dot(pc, ones, preferred_element_type=F32) + jnp.dot(pp, ones, preferred_element_type=F32)
        den = den[:, :hd] + jnp.exp(sink - mx)
        o = jnp.dot(pc, vc, preferred_element_type=F32) + jnp.dot(pp, vp, preferred_element_type=F32)
        o = (o / den).astype(o_ref.dtype)
        for g, c in enumerate(cols):
            o_ref[:, c:c + hd] = o[g * w:(g + 1) * w]


def _attn_prompt(qkv, sinks, batch, seq, n_kv, n_grp, hd):
    aw = n_kv * n_grp * hd
    kvw = n_kv * hd
    assert seq % WINDOW == 0 and aw % kvw == 0
    nb = seq // WINDOW
    kcol = aw // kvw
    est = 2 * (_nbytes((WINDOW, aw), F32) + 4 * _nbytes((WINDOW, kvw), F32) + _nbytes((WINDOW, aw), BF16)) \
        + 16 * _nbytes((WINDOW, WINDOW), F32)
    kern = functools.partial(_attn_prompt_kernel, n_kv=n_kv, n_grp=n_grp, hd=hd)
    return pl.pallas_call(
        kern,
        out_shape=jax.ShapeDtypeStruct((batch * seq, aw), BF16),
        grid=(batch, nb),
        in_specs=[pl.BlockSpec(memory_space=pltpu.SMEM),
                  pl.BlockSpec((WINDOW, aw), lambda b, i: (b * nb + i, 0)),
                  pl.BlockSpec((WINDOW, kvw), lambda b, i: (b * nb + i, kcol)),
                  pl.BlockSpec((WINDOW, kvw), lambda b, i: (b * nb + jnp.maximum(i - 1, 0), kcol)),
                  pl.BlockSpec((WINDOW, kvw), lambda b, i: (b * nb + i, kcol + 1)),
                  pl.BlockSpec((WINDOW, kvw), lambda b, i: (b * nb + jnp.maximum(i - 1, 0), kcol + 1))],
        out_specs=pl.BlockSpec((WINDOW, aw), lambda b, i: (b * nb + i, 0)),
        compiler_params=_params(("arbitrary", "arbitrary"), est),
        name="attn_prompt",
    )(sinks, qkv, qkv, qkv, qkv, qkv)


def _attn_sample_kernel(q_ref, kn_ref, vn_ref, ck_ref, cv_ref, sinks_ref, o_ref, ok_ref, ov_ref,
                        *, n_kv, n_grp, hd):
    bs, w, kvw = ck_ref.shape
    nh = n_kv * n_grp
    scale = hd ** -0.5
    head_kv = lax.broadcasted_iota(jnp.int32, (nh, kvw), 0) // n_grp
    lane_kv = lax.broadcasted_iota(jnp.int32, (nh, kvw), 1) // hd
    own = head_kv == lane_kv
    row = lax.broadcasted_iota(jnp.int32, (w, kvw), 0)
    sinks = sinks_ref[...]
    dn = (((1,), (1,)), ((), ()))
    for b in range(bs):
        q = q_ref[b]
        qbd = jnp.where(own, jnp.concatenate([q] * n_kv, axis=-1), 0.0)
        ck = ck_ref[b]
        cv = cv_ref[b]
        kn = kn_ref[pl.ds(b, 1), :]
        vn = vn_ref[pl.ds(b, 1), :]
        logits = lax.dot_general(qbd.astype(BF16), ck.astype(BF16), dn,
                                 preferred_element_type=F32) * scale
        l_new = jnp.sum(qbd.astype(BF16).astype(F32) * kn.astype(BF16).astype(F32),
                        axis=-1, keepdims=True) * scale
        mx = jnp.maximum(jnp.maximum(jnp.max(logits, axis=-1, keepdims=True), l_new), sinks)
        p = jnp.exp(logits - mx)
        p_new = jnp.exp(l_new - mx)
        den = jnp.sum(p, axis=-1, keepdims=True) + p_new + jnp.exp(sinks - mx)
        r = jnp.dot(p.astype(BF16), cv.astype(BF16), preferred_element_type=F32) \
            + p_new.astype(BF16).astype(F32) * vn.astype(BF16).astype(F32)
        r = jnp.where(own, r / den, 0.0)
        o = r[:, 0:hd]
        for kv in range(1, n_kv):
            o = o + r[:, kv * hd:(kv + 1) * hd]
        o_ref[b] = o.astype(o_ref.dtype)
        ok_ref[b] = jnp.where(row == w - 1, kn, pltpu.roll(ck, shift=w - 1, axis=0))
        ov_ref[b] = jnp.where(row == w - 1, vn, pltpu.roll(cv, shift=w - 1, axis=0))


def _attn_sample(q3, qkv_s, cache_k, cache_v, sinks_col, n_kv, n_grp, hd):
    db, w, kvw = cache_k.shape
    nh = n_kv * n_grp
    aw = nh * hd
    kcol = aw // kvw
    bs = _pick_tile(db, 8, SUBLANES)
    est = 2 * (4 * _nbytes((bs, w, kvw), F32) + 2 * _nbytes((bs, kvw), F32) + 2 * _nbytes((bs, nh, LANES), F32)) \
        + 16 * _nbytes((w, kvw), F32)
    kern = functools.partial(_attn_sample_kernel, n_kv=n_kv, n_grp=n_grp, hd=hd)
    return pl.pallas_call(
        kern,
        out_shape=[jax.ShapeDtypeStruct((db, nh, hd), BF16),
                   jax.ShapeDtypeStruct((db, w, kvw), F32),
                   jax.ShapeDtypeStruct((db, w, kvw), F32)],
        grid=(db // bs,),
        in_specs=[pl.BlockSpec((bs, nh, hd), lambda i: (i, 0, 0)),
                  pl.BlockSpec((bs, kvw), lambda i: (i, kcol)),
                  pl.BlockSpec((bs, kvw), lambda i: (i, kcol + 1)),
                  pl.BlockSpec((bs, w, kvw), lambda i: (i, 0, 0)),
                  pl.BlockSpec((bs, w, kvw), lambda i: (i, 0, 0)),
                  pl.BlockSpec((nh, 1), lambda i: (0, 0))],
        out_specs=[pl.BlockSpec((bs, nh, hd), lambda i: (i, 0, 0)),
                   pl.BlockSpec((bs, w, kvw), lambda i: (i, 0, 0)),
                   pl.BlockSpec((bs, w, kvw), lambda i: (i, 0, 0))],
        compiler_params=_params(("arbitrary",), est),
        name="attn_sample",
    )(q3, qkv_s, qkv_s, cache_k, cache_v, sinks_col)


def _layer_norm(v, g, b):
    mu = jnp.mean(v, axis=-1, keepdims=True)
    c = v - mu
    var = jnp.mean(c * c, axis=-1, keepdims=True)
    return c * lax.rsqrt(var + EPS) * g + b


def _gmlp_prompt_kernel(u_ref, v_ref, g_ref, b_ref, ws_ref, bs_ref, o_ref, *, n_groups):
    ch = u_ref.shape[0]
    gd = u_ref.shape[1] // n_groups
    vn = _layer_norm(v_ref[...].astype(F32), g_ref[...], b_ref[...])
    ti = lax.broadcasted_iota(jnp.int32, (ch, ch), 0)
    si = lax.broadcasted_iota(jnp.int32, (ch, ch), 1)
    causal = si <= ti
    for g in range(n_groups):
        cs = slice(g * gd, (g + 1) * gd)
        wc = jnp.where(causal, ws_ref[g], 0.0).astype(BF16)
        mixed = jnp.dot(wc, vn[:, cs].astype(BF16), preferred_element_type=F32) + bs_ref[:, g:g + 1]
        o_ref[:, cs] = (u_ref[:, cs].astype(F32) * mixed).astype(o_ref.dtype)


def _gmlp_prompt(ugv, ln_g, ln_b, w_spatial, b_spatial_t):
    m, two_w = ugv.shape
    gmw = two_w // 2
    n_groups, ch, _ = w_spatial.shape
    assert m % ch == 0
    est = 2 * (2 * _nbytes((ch, gmw), ugv.dtype) + _nbytes((ch, gmw), BF16) + _nbytes(w_spatial.shape, F32)) \
        + 6 * _nbytes((ch, gmw), F32)
    kern = functools.partial(_gmlp_prompt_kernel, n_groups=n_groups)
    return pl.pallas_call(
        kern,
        out_shape=jax.ShapeDtypeStruct((m, gmw), BF16),
        grid=(m // ch,),
        in_specs=[pl.BlockSpec((ch, gmw), lambda i: (i, 0)),
                  pl.BlockSpec((ch, gmw), lambda i: (i, 1)),
                  pl.BlockSpec((1, gmw), lambda i: (0, 0)),
                  pl.BlockSpec((1, gmw), lambda i: (0, 0)),
                  pl.BlockSpec((n_groups, ch, ch), lambda i: (0, 0, 0)),
                  pl.BlockSpec((ch, n_groups), lambda i: (0, 0))],
        out_specs=pl.BlockSpec((ch, gmw), lambda i: (i, 0)),
        compiler_params=_params(("arbitrary",), est),
        name="gmlp_prompt",
    )(ugv, ugv, ln_g.reshape(1, gmw), ln_b.reshape(1, gmw), w_spatial, b_spatial_t)


def _gmlp_sample_kernel(u_ref, v_ref, g_ref, b_ref, w0_ref, b0_ref, o_ref, vn_ref):
    vn = _layer_norm(v_ref[...], g_ref[...], b_ref[...])
    vn_ref[...] = vn
    o_ref[...] = (u_ref[...] * (w0_ref[...] * vn + b0_ref[...])).astype(o_ref.dtype)


def _gmlp_sample(ugv, ln_g, ln_b, w00, b0):
    m, two_w = ugv.shape
    gmw = two_w // 2
    est = 2 * (3 * _nbytes((m, gmw), F32) + _nbytes((m, gmw), BF16)) + 4 * _nbytes((m, gmw), F32)
    row = pl.BlockSpec((1, gmw), lambda i: (0, 0))
    return pl.pallas_call(
        _gmlp_sample_kernel,
        out_shape=[jax.ShapeDtypeStruct((m, gmw), BF16), jax.ShapeDtypeStruct((m, gmw), F32)],
        grid=(1,),
        in_specs=[pl.BlockSpec((m, gmw), lambda i: (0, 0)), pl.BlockSpec((m, gmw), lambda i: (0, 1)),
                  row, row, row, row],
        out_specs=[pl.BlockSpec((m, gmw), lambda i: (0, 0)), pl.BlockSpec((m, gmw), lambda i: (0, 0))],
        compiler_params=_params(("arbitrary",), est),
        name="gmlp_sample",
    )(ugv, ugv, ln_g.reshape(1, gmw), ln_b.reshape(1, gmw), w00, b0)


def _pack_bf16_pair(left, right):
    lb = pltpu.bitcast(left.astype(BF16).astype(F32), jnp.uint32)
    rb = pltpu.bitcast(right.astype(BF16).astype(F32), jnp.uint32)
    return lb | (rb >> 16)


def _unpack_bf16_pair(word):
    left = pltpu.bitcast(word & jnp.uint32(0xFFFF0000), F32)
    right = pltpu.bitcast(word << 16, F32)
    return left, right


def _split_bf16(x):
    hi = x.astype(BF16)
    lo = (x - hi.astype(F32)).astype(BF16)
    return hi, lo


def _ffn_norm_kernel(x_ref, g_ref, sh_ref, sc_ref, wr_ref, h_ref, hp_ref, lg_ref):
    h = _rms_mod(x_ref[...], g_ref[...], sc_ref[0], sh_ref[0])
    half = h.shape[1] // 2
    h_ref[...] = h.astype(BF16)
    hp_ref[...] = _pack_bf16_pair(h[:, :half], h[:, half:])
    h_hi, h_lo = _split_bf16(h)
    w_hi, w_lo = _split_bf16(wr_ref[...])
    dn = (((1,), (1,)), ((), ()))
    lg_ref[...] = lax.dot_general(w_hi, h_hi, dn, preferred_element_type=F32) \
        + lax.dot_general(w_hi, h_lo, dn, preferred_element_type=F32) \
        + lax.dot_general(w_lo, h_hi, dn, preferred_element_type=F32)


def _ffn_norm(x2, g, mod3, shift_chunk, scale_chunk, group_rows, w_router_t):
    m, d = x2.shape
    e = w_router_t.shape[0]
    tm = _pick_tile(min(m, group_rows) if mod3.shape[1] == 1 else m, 256, LANES)
    row_of = lambda i: i
    est = 2 * (3 * _nbytes((tm, d), F32) + 2 * _nbytes((tm, d), BF16) + _nbytes((e, d), F32)) \
        + 6 * _nbytes((tm, d), F32)
    return pl.pallas_call(
        _ffn_norm_kernel,
        out_shape=[jax.ShapeDtypeStruct((m, d), BF16),
                   jax.ShapeDtypeStruct((m, d // 2), jnp.uint32),
                   jax.ShapeDtypeStruct((e, m), F32)],
        grid=(m // tm,),
        in_specs=[pl.BlockSpec((tm, d), lambda i: (i, 0)),
                  pl.BlockSpec((1, d), lambda i: (0, 0)),
                  _mod_spec(mod3, tm, group_rows, d, row_of, lambda i: shift_chunk),
                  _mod_spec(mod3, tm, group_rows, d, row_of, lambda i: scale_chunk),
                  pl.BlockSpec((e, d), lambda i: (0, 0))],
        out_specs=[pl.BlockSpec((tm, d), lambda i: (i, 0)),
                   pl.BlockSpec((tm, d // 2), lambda i: (i, 0)),
                   pl.BlockSpec((e, tm), lambda i: (0, i))],
        compiler_params=_params(("arbitrary",), est),
        name="ffn_norm_router",
    )(x2, g.reshape(1, d), mod3, mod3, w_router_t)


def _first_max(vals, idx):
    mx = jnp.max(vals, axis=0, keepdims=True)
    first = jnp.min(jnp.where(vals == mx, idx, jnp.int32(2 ** 30)), axis=0, keepdims=True)
    return mx, first


def _route_kernel(lg_ref, bias_ref, eidx_ref, ew_ref, rank_ref, cnt_ref, base_scr,
                  *, n_groups, topk_groups, top_k, scale):
    e, tr = lg_ref.shape
    per = e // n_groups

    @pl.when(pl.program_id(0) == 0)
    def _():
        base_scr[...] = jnp.zeros_like(base_scr)

    scores = jax.nn.sigmoid(lg_ref[...])
    biased = scores + bias_ref[...]
    eid = lax.broadcasted_iota(jnp.int32, (e, tr), 0)
    neg = jnp.float32(-jnp.inf)

    grp_rows = []
    bid = lax.broadcasted_iota(jnp.int32, (per, tr), 0)
    for g in range(n_groups):
        blk = biased[g * per:(g + 1) * per]
        m1, i1 = _first_max(blk, bid)
        m2 = jnp.max(jnp.where(bid == i1, neg, blk), axis=0, keepdims=True)
        grp_rows.append(m1 + m2)
    grp = jnp.concatenate(grp_rows, axis=0)
    gid = lax.broadcasted_iota(jnp.int32, (n_groups, tr), 0)
    gsel = jnp.zeros((n_groups, tr), F32)
    work = grp
    for _ in range(topk_groups):
        _, gi = _first_max(work, gid)
        hit = gid == gi
        gsel = jnp.where(hit, 1.0, gsel)
        work = jnp.where(hit, neg, work)
    emask = jnp.concatenate(
        [jnp.broadcast_to(gsel[g:g + 1], (per, tr)) for g in range(n_groups)], axis=0)
    masked = jnp.where(emask > 0.0, biased, neg)

    onehots, idxs, wts = [], [], []
    for _ in range(top_k):
        _, ei = _first_max(masked, eid)
        hit = eid == ei
        onehots.append(hit)
        idxs.append(ei)
        wts.append(jnp.sum(jnp.where(hit, scores, 0.0), axis=0, keepdims=True))
        masked = jnp.where(hit, neg, masked)
    wsum = wts[0]
    for wk in wts[1:]:
        wsum = wsum + wk

    chosen = onehots[0]
    for oh in onehots[1:]:
        chosen = jnp.logical_or(chosen, oh)
    chosen_f = jnp.where(chosen, 1.0, 0.0)
    si = lax.broadcasted_iota(jnp.int32, (tr, tr), 0)
    ti = lax.broadcasted_iota(jnp.int32, (tr, tr), 1)
    upper = jnp.where(si < ti, 1.0, 0.0).astype(BF16)
    prefix = jnp.dot(chosen_f.astype(BF16), upper, preferred_element_type=F32)
    pos = prefix + base_scr[:, 0:1]
    for k in range(top_k):
        eidx_ref[k:k + 1, :] = idxs[k]
        ew_ref[k:k + 1, :] = wts[k] / wsum * scale
        rank_ref[k:k + 1, :] = jnp.sum(jnp.where(onehots[k], pos, 0.0), axis=0, keepdims=True).astype(jnp.int32)
    for k in range(top_k, eidx_ref.shape[0]):
        eidx_ref[k:k + 1, :] = jnp.zeros((1, tr), jnp.int32)
        ew_ref[k:k + 1, :] = jnp.zeros((1, tr), F32)
        rank_ref[k:k + 1, :] = jnp.zeros((1, tr), jnp.int32)
    base_scr[...] = base_scr[...] + jnp.sum(chosen_f, axis=1, keepdims=True)
    cnt_ref[...] = base_scr[...].astype(jnp.int32)


def _route(logits_t, router_bias):
    e, n = logits_t.shape
    tr = _pick_tile(n, 640, LANES)
    rows = SUBLANES
    assert TOP_K <= rows and e % N_EXPERT_GROUPS == 0
    kern = functools.partial(_route_kernel, n_groups=N_EXPERT_GROUPS, topk_groups=TOPK_GROUPS,
                             top_k=TOP_K, scale=ROUTED_SCALE)
    est = 2 * (_nbytes((e, tr), F32) + 3 * _nbytes((rows, tr), F32)) + 24 * _nbytes((e, tr), F32) \
        + 3 * _nbytes((tr, tr), F32)
    out_row = pl.BlockSpec((rows, tr), lambda i: (0, i))
    return pl.pallas_call(
        kern,
        out_shape=[jax.ShapeDtypeStruct((rows, n), jnp.int32),
                   jax.ShapeDtypeStruct((rows, n), F32),
                   jax.ShapeDtypeStruct((rows, n), jnp.int32),
                   jax.ShapeDtypeStruct((e, LANES), jnp.int32)],
        grid=(n // tr,),
        in_specs=[pl.BlockSpec((e, tr), lambda i: (0, i)),
                  pl.BlockSpec((e, 1), lambda i: (0, 0))],
        out_specs=[out_row, out_row, out_row, pl.BlockSpec((e, LANES), lambda i: (0, 0))],
        scratch_shapes=[pltpu.VMEM((e, LANES), F32)],
        compiler_params=_params(("arbitrary",), est),
        name="route",
    )(logits_t, router_bias.reshape(e, 1))


def _dispatch_kernel(dest_ref, fill_ref, nfill_ref, hp_ref, xs_ref, zero_scr, sem, *, top_k, n_tok):
    step = pl.program_id(0)
    tt = hp_ref.shape[0]
    fill_rows = zero_scr.shape[0]

    @pl.when(step == 0)
    def _():
        zero_scr[...] = jnp.zeros_like(zero_scr)
        n_fill = nfill_ref[0]

        def start_fill(i, c):
            row0 = pl.multiple_of(fill_ref[i] * fill_rows, fill_rows)
            pltpu.make_async_copy(zero_scr, xs_ref.at[pl.ds(row0, fill_rows)], sem).start()
            return c

        def wait_fill(i, c):
            pltpu.make_async_copy(zero_scr, xs_ref.at[pl.ds(0, fill_rows)], sem).wait()
            return c

        lax.fori_loop(0, n_fill, start_fill, 0)
        lax.fori_loop(0, n_fill, wait_fill, 0)

    base = step * tt

    def start(r, c):
        for k in range(top_k):
            pltpu.make_async_copy(hp_ref.at[pl.ds(r, 1)],
                                  xs_ref.at[pl.ds(dest_ref[k * n_tok + base + r], 1)], sem
                                  ).start(priority=k % N_DMA_PRIORITIES)
        return c

    def wait(r, c):
        for k in range(top_k):
            pltpu.make_async_copy(hp_ref.at[pl.ds(r, 1)], xs_ref.at[pl.ds(0, 1)], sem).wait()
        return c

    lax.fori_loop(0, tt, start, 0)
    lax.fori_loop(0, tt, wait, 0)


def _dispatch(dest_flat, fill_tiles, n_fill, hp, n_rows, fill_rows):
    n_tok, half = hp.shape
    tt = _pick_tile(n_tok, 128, SUBLANES)
    kern = functools.partial(_dispatch_kernel, top_k=TOP_K, n_tok=n_tok)
    est = 2 * _nbytes((tt, half), hp.dtype) + _nbytes((fill_rows, half), hp.dtype)
    return pl.pallas_call(
        kern,
        out_shape=jax.ShapeDtypeStruct((n_rows, half), hp.dtype),
        grid_spec=pltpu.PrefetchScalarGridSpec(
            num_scalar_prefetch=3,
            grid=(n_tok // tt,),
            in_specs=[pl.BlockSpec((tt, half), lambda i, dest, fill, nfill: (i, 0))],
            out_specs=pl.BlockSpec(memory_space=pl.ANY),
            scratch_shapes=[pltpu.VMEM((fill_rows, half), hp.dtype), pltpu.SemaphoreType.DMA]),
        compiler_params=_params(("arbitrary",), est),
        name="moe_dispatch",
    )(dest_flat, fill_tiles, n_fill, hp)


def _stream_expert_weights(tile_e_ref, next_e_ref, flags_ref, blk_ref, w_refs, col_of, stage, sems):
    j = pl.program_id(0)
    t = pl.program_id(1)
    n_pass = pl.num_programs(0)
    width = stage.shape[-1]

    def copies(e, jj, slot):
        return [pltpu.make_async_copy(
            w_ref.at[e, :, pl.ds(pl.multiple_of(col(jj), width), width)], stage.at[slot, i], sems.at[slot])
            for i, (w_ref, col) in enumerate(zip(w_refs, col_of))]

    @pl.when(jnp.logical_and(j == 0, t == 0))
    def _():
        blk_ref[0] = 0
        for c in copies(tile_e_ref[0], 0, 0):
            c.start(priority=WEIGHT_STREAM_DMA_PRIORITY)

    slot = blk_ref[0] % 2
    for c in copies(tile_e_ref[t], j, slot):
        c.wait()
    in_pass = (flags_ref[t] & TILE_HAS_NEXT) > 0
    e_next = jnp.where(in_pass, next_e_ref[t], tile_e_ref[0])
    j_next = jnp.where(in_pass, j, j + 1)

    @pl.when(jnp.logical_or(in_pass, j + 1 < n_pass))
    def _():
        for c in copies(e_next, j_next, 1 - slot):
            c.start(priority=WEIGHT_STREAM_DMA_PRIORITY)

    blk_ref[0] = blk_ref[0] + 1


def _expert_up_kernel(tile_e_ref, used_ref, next_e_ref, flags_ref, xs_ref, wg_ref, wu_ref, o_ref,
                      stage, sems, blk_ref):
    t = pl.program_id(1)
    n_used = used_ref[0]
    fc = o_ref.shape[1]

    @pl.when(t < n_used)
    def _():
        prev = tile_e_ref[jnp.maximum(t - 1, 0)]
        fresh = jnp.logical_or(t == 0, tile_e_ref[t] != prev)

        @pl.when(fresh)
        def _():
            col = lambda jj: jj * fc
            _stream_expert_weights(tile_e_ref, next_e_ref, flags_ref, blk_ref, [wg_ref, wu_ref], [col, col],
                                   stage, sems)

        half = xs_ref.shape[1]
        tm = xs_ref.shape[0]
        slot = (blk_ref[0] + 1) % 2

        def project(rows):
            xl, xr = _unpack_bf16_pair(xs_ref[:rows])
            xl = xl.astype(BF16)
            xr = xr.astype(BF16)
            gate = jnp.dot(xl, stage[slot, 0, :half].astype(BF16), preferred_element_type=F32) \
                + jnp.dot(xr, stage[slot, 0, half:].astype(BF16), preferred_element_type=F32)
            up = jnp.dot(xl, stage[slot, 1, :half].astype(BF16), preferred_element_type=F32) \
                + jnp.dot(xr, stage[slot, 1, half:].astype(BF16), preferred_element_type=F32)
            o_ref[:rows] = (_silu(gate) * up).astype(o_ref.dtype)
            if rows < tm:
                o_ref[rows:] = jnp.zeros((tm - rows, fc), o_ref.dtype)

        half_full = (flags_ref[t] & TILE_HALF_FULL) > 0

        @pl.when(half_full)
        def _():
            project(tm // 2)

        @pl.when(jnp.logical_not(half_full))
        def _():
            project(tm)

    @pl.when(t >= n_used)
    def _():
        o_ref[...] = jnp.zeros_like(o_ref)


def _expert_up(tile_e, n_used, next_e, tile_flags, xs, w_gate, w_up, tm, fc):
    p, half = xs.shape
    e, d, de = w_gate.shape
    n_t = p // tm
    clamp = lambda t, used: jnp.minimum(t, used[0] - 1)
    est = 2 * (_nbytes((tm, half), jnp.uint32) + 2 * _nbytes((d, fc), F32)
               + _nbytes((tm, fc), BF16)) + 2 * _nbytes((d, fc), BF16) + 2 * _nbytes((tm, d), BF16) \
        + 4 * _nbytes((tm, fc), F32) + 2 * _nbytes((tm, half), F32)
    return pl.pallas_call(
        _expert_up_kernel,
        out_shape=jax.ShapeDtypeStruct((p, de), BF16),
        grid_spec=pltpu.PrefetchScalarGridSpec(
            num_scalar_prefetch=4,
            grid=(de // fc, n_t),
            in_specs=[pl.BlockSpec((tm, half), lambda j, t, te, used, ne, hn: (clamp(t, used), 0)),
                      pl.BlockSpec(memory_space=pl.ANY),
                      pl.BlockSpec(memory_space=pl.ANY)],
            out_specs=pl.BlockSpec((tm, fc), lambda j, t, te, used, ne, hn: (t, j)),
            scratch_shapes=[pltpu.VMEM((2, 2, d, fc), F32),
                            pltpu.SemaphoreType.DMA((2,)), pltpu.SMEM((1,), jnp.int32)]),
        compiler_params=_params(("arbitrary", "arbitrary"), est),
        name="expert_up",
    )(tile_e, n_used, next_e, tile_flags, xs, w_gate, w_up)


def _expert_down_kernel(tile_e_ref, used_ref, next_e_ref, flags_ref, a_ref, wd_ref, o_ref,
                        stage, sems, blk_ref):
    t = pl.program_id(1)
    n_used = used_ref[0]
    nc = o_ref.shape[1]
    half = wd_ref.shape[2] // 2

    @pl.when(t < n_used)
    def _():
        prev = tile_e_ref[jnp.maximum(t - 1, 0)]
        fresh = jnp.logical_or(t == 0, tile_e_ref[t] != prev)

        @pl.when(fresh)
        def _():
            _stream_expert_weights(tile_e_ref, next_e_ref, flags_ref, blk_ref, [wd_ref, wd_ref],
                                   [lambda cc: cc * nc, lambda cc: half + cc * nc],
                                   stage, sems)

        tm = a_ref.shape[0]
        slot = (blk_ref[0] + 1) % 2

        def project(rows):
            a = a_ref[:rows]
            yl = jnp.dot(a, stage[slot, 0].astype(BF16), preferred_element_type=F32)
            yr = jnp.dot(a, stage[slot, 1].astype(BF16), preferred_element_type=F32)
            o_ref[:rows] = _pack_bf16_pair(yl, yr)
            if rows < tm:
                o_ref[rows:] = jnp.zeros((tm - rows, nc), o_ref.dtype)

        half_full = (flags_ref[t] & TILE_HALF_FULL) > 0

        @pl.when(half_full)
        def _():
            project(tm // 2)

        @pl.when(jnp.logical_not(half_full))
        def _():
            project(tm)

    @pl.when(t >= n_used)
    def _():
        o_ref[...] = jnp.zeros_like(o_ref)


def _expert_down(tile_e, n_used, next_e, tile_flags, act, w_down, tm, nc):
    p, de = act.shape
    e, _, d = w_down.shape
    half = d // 2
    n_t = p // tm
    n_c = half // nc
    clamp = lambda t, used: jnp.minimum(t, used[0] - 1)
    est = 2 * (_nbytes((tm, de), BF16) + 2 * _nbytes((de, nc), F32) + _nbytes((tm, nc), jnp.uint32)) \
        + 2 * _nbytes((de, nc), BF16) + 3 * _nbytes((tm, nc), F32)
    return pl.pallas_call(
        _expert_down_kernel,
        out_shape=jax.ShapeDtypeStruct((p, half), jnp.uint32),
        grid_spec=pltpu.PrefetchScalarGridSpec(
            num_scalar_prefetch=4,
            grid=(n_c, n_t),
            in_specs=[pl.BlockSpec((tm, de), lambda c, t, te, used, ne, hn: (clamp(t, used), 0)),
                      pl.BlockSpec(memory_space=pl.ANY)],
            out_specs=pl.BlockSpec((tm, nc), lambda c, t, te, used, ne, hn: (t, c)),
            scratch_shapes=[pltpu.VMEM((2, 2, de, nc), F32),
                            pltpu.SemaphoreType.DMA((2,)), pltpu.SMEM((1,), jnp.int32)]),
        compiler_params=_params(("arbitrary", "arbitrary"), est),
        name="expert_down",
    )(tile_e, n_used, next_e, tile_flags, act, w_down)


def _combine_kernel(dest_ref, ys_ref, x_ref, sh_ref, gf_ref, ew_ref, o_ref, buf, sem, *, top_k, n_tok, tok0):
    tt = x_ref.shape[0]
    half = ys_ref.shape[1]
    step = pl.program_id(0)
    slot = step % 2

    def token_copies(for_step, into, r, start):
        base = tok0 + for_step * tt
        for k in range(top_k):
            row = dest_ref[k * n_tok + base + r] if start else 0
            cp = pltpu.make_async_copy(ys_ref.at[pl.ds(row, 1)], buf.at[into, k, pl.ds(r, 1)], sem.at[into])
            if start:
                cp.start(priority=k % N_DMA_PRIORITIES)
            else:
                cp.wait()

    def gather(for_step, into, start):
        def body(r, c):
            token_copies(for_step, into, r, start)
            return c

        lax.fori_loop(0, tt, body, 0)

    per_row_gate = gf_ref.shape[1] != 1

    def reduce_rows(c, fetch_next):
        row0 = pl.multiple_of(c * SUBLANES, SUBLANES)
        if fetch_next:
            for r in range(SUBLANES):
                token_copies(step + 1, 1 - slot, row0 + r, True)
        rows = pl.ds(row0, SUBLANES)
        ew = ew_ref[rows, :]
        gf = gf_ref[0, rows, :] if per_row_gate else gf_ref[0]
        for side, cols in ((0, slice(0, half)), (1, slice(half, 2 * half))):
            acc = None
            for k in range(top_k):
                term = _unpack_bf16_pair(buf[slot, k, rows, :])[side] * ew[:, k:k + 1]
                acc = term if acc is None else acc + term
            o_ref[rows, cols] = x_ref[rows, cols] + gf[:, cols] * (acc + sh_ref[rows, cols])

    @pl.when(step == 0)
    def _():
        gather(0, 0, True)

    gather(step, slot, False)
    more = step + 1 < pl.num_programs(0)

    @pl.when(more)
    def _():
        lax.fori_loop(0, tt // SUBLANES, lambda c, carry: (reduce_rows(c, True), carry)[1], 0)

    @pl.when(jnp.logical_not(more))
    def _():
        lax.fori_loop(0, tt // SUBLANES, lambda c, carry: (reduce_rows(c, False), carry)[1], 0)


def _combine(dest_flat, ys, x2, shared, mod3, ew_t, gate_chunk, group_rows, n_tok, tok0):
    m, d = x2.shape
    half = d // 2
    ew_rows = ew_t.shape[1]
    tt = _pick_tile(min(m, group_rows) if mod3.shape[1] == 1 else m, 128, SUBLANES)
    sh_blk0 = tok0 // tt
    assert tok0 % tt == 0
    row_of = lambda i, dest: i
    kern = functools.partial(_combine_kernel, top_k=TOP_K, n_tok=n_tok, tok0=tok0)
    est = 2 * (3 * _nbytes((tt, d), F32) + _nbytes((tt, d), F32)) + 2 * _nbytes((TOP_K, tt, half), jnp.uint32) \
        + 6 * _nbytes((tt, d), F32)
    return pl.pallas_call(
        kern,
        out_shape=jax.ShapeDtypeStruct((m, d), F32),
        grid_spec=pltpu.PrefetchScalarGridSpec(
            num_scalar_prefetch=1,
            grid=(m // tt,),
            in_specs=[pl.BlockSpec(memory_space=pl.ANY),
                      pl.BlockSpec((tt, d), lambda i, dest: (i, 0)),
                      pl.BlockSpec((tt, d), lambda i, dest: (i + sh_blk0, 0)),
                      _mod_spec(mod3, tt, group_rows, d, row_of, lambda i, dest: gate_chunk),
                      pl.BlockSpec((tt, ew_rows), lambda i, dest: (i + sh_blk0, 0))],
            out_specs=pl.BlockSpec((tt, d), lambda i, dest: (i, 0)),
            scratch_shapes=[pltpu.VMEM((2, TOP_K, tt, half), jnp.uint32), pltpu.SemaphoreType.DMA((2,))]),
        compiler_params=_params(("arbitrary",), est),
        name="moe_combine",
    )(dest_flat, ys, x2, shared, mod3, ew_t)


def _token_mixing(x2, mod3, group_rows, p, attend):
    m, d = x2.shape
    aw, kvw, gmw, hd = p["aw"], p["kvw"], p["gmw"], p["hd"]
    tn = 512
    assert aw % tn == 0 and (2 * kvw) % tn == 0 and gmw % tn == 0 and d % tn == 0
    tm = _pick_tile(min(m, group_rows) if mod3.shape[1] == 1 else m, 1024, 16)
    h = _modulate(x2, p["norm_mix_g"], mod3, 0, 1, group_rows)
    w_in = p["w_in"]

    qkv_w = aw + 2 * kvw

    def qkv_epilogue(accs, extras, n, mi):
        gain_ref, flag_ref, bd_ref = extras
        z = accs[0]
        sq_hi, sq_lo = _split_bf16(z * z)
        ss = jnp.dot(sq_hi, bd_ref[...], preferred_element_type=F32) \
            + jnp.dot(sq_lo, bd_ref[...], preferred_element_type=F32)
        inv = lax.rsqrt(ss * (1.0 / hd) + EPS)
        return [z * jnp.where(flag_ref[...] > 0.0, inv, 1.0) * gain_ref[...]]

    qkv = _mm([(h, d, 0)], [(w_in, 0)],
              [(p["qkv_gain"], pl.BlockSpec((1, tn), lambda n, mi: (0, n))),
               (p["qkv_flag"], pl.BlockSpec((1, tn), lambda n, mi: (0, n))),
               (p["head_ones"], pl.BlockSpec((tn, tn), lambda n, mi: (0, 0)))],
              qkv_epilogue,
              [(jax.ShapeDtypeStruct((m, qkv_w), F32), pl.BlockSpec((tm, tn), lambda n, mi: (mi, n)))],
              m=m, tm=tm, tn=tn, n_tiles=qkv_w // tn, pairs=[(0, 0)], name="in_proj_qkv")[0]

    ugv_dtype = BF16 if mod3.shape[1] == 1 else F32
    ugv = _mm([(h, d, 0)], [(w_in, qkv_w // tn)], [],
              lambda accs, extras, n, mi: [_gelu(accs[0])],
              [(jax.ShapeDtypeStruct((m, 2 * gmw), ugv_dtype), pl.BlockSpec((tm, tn), lambda n, mi: (mi, n)))],
              m=m, tm=tm, tn=tn, n_tiles=2 * gmw // tn, pairs=[(0, 0)], name="in_proj_gmlp")[0]

    gates = _mm([(h, d, 0)], [(w_in, (qkv_w + 2 * gmw) // tn)], [],
                lambda accs, extras, n, mi: [jax.nn.sigmoid(accs[0])],
                [(jax.ShapeDtypeStruct((m, 2 * d), BF16), pl.BlockSpec((tm, tn), lambda n, mi: (mi, n)))],
                m=m, tm=tm, tn=tn, n_tiles=2 * d // tn, pairs=[(0, 0)], name="in_proj_gates")[0]

    o_attn, o_gmlp, aux = attend(qkv, ugv)

    nd = d // tn
    merged = _mm([(o_attn, aw, 0), (o_gmlp, gmw, 0)], [(p["w_branch_attn"], 0), (p["w_branch_gmlp"], 0)],
                 [(gates, pl.BlockSpec((tm, tn), lambda n, mi: (mi, n))),
                  (gates, pl.BlockSpec((tm, tn), lambda n, mi: (mi, n + nd)))],
                 lambda accs, extras, n, mi: [extras[0][...].astype(F32) * accs[0]
                                              + extras[1][...].astype(F32) * accs[1]],
                 [(jax.ShapeDtypeStruct((m, d), BF16), pl.BlockSpec((tm, tn), lambda n, mi: (mi, n)))],
                 m=m, tm=tm, tn=tn, n_tiles=nd, pairs=[(0, 0), (1, 1)], name="branch_merge")[0]

    x1 = _mm([(merged, d, 0)], [(p["w_out"], 0)],
             [(x2, pl.BlockSpec((tm, tn), lambda n, mi: (mi, n))),
              (mod3, _mod_spec(mod3, tm, group_rows, tn, lambda n, mi: mi, lambda n, mi: 2 * nd + n))],
             lambda accs, extras, n, mi: [extras[0][...] + extras[1][0] * accs[0]],
             [(jax.ShapeDtypeStruct((m, d), F32), pl.BlockSpec((tm, tn), lambda n, mi: (mi, n)))],
             m=m, tm=tm, tn=tn, n_tiles=nd, pairs=[(0, 0)], name="out_proj")[0]
    return x1, qkv, aux


def kernel(x_prompt, x_sample, cache_k_win, cache_v_win, c_prompt, c_sample, norm_mix_g, norm_ffn_g,
           w_ada, b_ada, w_in, q_norm_g, k_norm_g, attn_sinks, gm_ln_g, gm_ln_b, w_spatial, b_spatial,
           w_branch_attn, w_branch_gmlp, w_out, w_router, router_bias, w_gate, w_up, w_down,
           ws_gate, ws_up, ws_down):
    depth = norm_mix_g.shape[0]
    assert depth == 1, "single-layer step"
    batch, seq, d = x_prompt.shape
    db, t_new, _ = x_sample.shape
    assert t_new == 1, "one new token per sequence"
    _, _, win, n_kv, hd = cache_k_win.shape
    n_grp = attn_sinks.shape[-1]
    aw, kvw = n_kv * n_grp * hd, n_kv * hd
    gmw = gm_ln_g.shape[-1]
    n_groups, ch, _ = w_spatial.shape[1:]
    n_exp = w_router.shape[-1]
    de = w_gate.shape[-1]
    ds = ws_gate.shape[-1]
    tn = 512
    l = 0

    n_c = batch + db
    rows = -(-n_c // 16) * 16
    c_all = jnp.concatenate([c_prompt, c_sample, jnp.zeros((rows - n_c, d), F32)], axis=0)
    mods = _adaln(c_all, w_ada[l], b_ada[l])
    mod_p = mods[:batch].reshape(batch, 1, 6 * d)
    mod_s = mods[batch:n_c].reshape(1, db, 6 * d)

    gq = jnp.tile(q_norm_g[l], aw // hd)
    gk = jnp.tile(k_norm_g[l], kvw // hd)
    qkv_gain = jnp.concatenate([gq, gk, jnp.ones((kvw,), F32)]).reshape(1, -1)
    qkv_flag = jnp.concatenate([jnp.ones((aw + kvw,), F32), jnp.zeros((kvw,), F32)]).reshape(1, -1)
    hid = jnp.arange(tn) // hd
    head_ones = (hid[:, None] == hid[None, :]).astype(BF16)
    params = dict(aw=aw, kvw=kvw, gmw=gmw, hd=hd, norm_mix_g=norm_mix_g[l], w_in=w_in[l],
                  qkv_gain=qkv_gain, qkv_flag=qkv_flag, head_ones=head_ones,
                  w_branch_attn=w_branch_attn[l], w_branch_gmlp=w_branch_gmlp[l], w_out=w_out[l])
    sinks = attn_sinks[l]

    def attend_prompt(qkv, ugv):
        o_attn = _attn_prompt(qkv, sinks, batch, seq, n_kv, n_grp, hd)
        o_gmlp = _gmlp_prompt(ugv, gm_ln_g[l], gm_ln_b[l], w_spatial[l], b_spatial[l].T)
        return o_attn, o_gmlp, None

    xp2 = x_prompt.reshape(batch * seq, d)
    x1_p, qkv_p, _ = _token_mixing(xp2, mod_p, seq, params, attend_prompt)

    ck = cache_k_win[l].reshape(db, win, kvw)
    cv = cache_v_win[l].reshape(db, win, kvw)

    def attend_sample(qkv, ugv):
        q3 = qkv[:, :aw].reshape(db, n_kv * n_grp, hd)
        o3, new_k, new_v = _attn_sample(q3, qkv, ck, cv, sinks.reshape(-1, 1), n_kv, n_grp, hd)
        w00 = jnp.repeat(w_spatial[l][:, 0, 0], gmw // n_groups).reshape(1, gmw)
        b0 = jnp.repeat(b_spatial[l][:, 0], gmw // n_groups).reshape(1, gmw)
        o_gmlp, vn = _gmlp_sample(ugv, gm_ln_g[l], gm_ln_b[l], w00, b0)
        return o3.reshape(db, aw), o_gmlp, (new_k, new_v, vn)

    xs2 = x_sample.reshape(db, d)
    x1_s, _, (new_k_s, new_v_s, vn_s) = _token_mixing(xs2, mod_s, 1, params, attend_sample)

    w_router_t = w_router[l].T
    h2_p, hp_p, lg_p = _ffn_norm(x1_p, norm_ffn_g[l], mod_p, 3, 4, seq, w_router_t)
    h2_s, hp_s, lg_s = _ffn_norm(x1_s, norm_ffn_g[l], mod_s, 3, 4, 1, w_router_t)
    h2 = jnp.concatenate([h2_p, h2_s], axis=0)
    hp = jnp.concatenate([hp_p, hp_s], axis=0)
    logits_t = jnp.concatenate([lg_p, lg_s], axis=1)
    n_tok = h2.shape[0]

    eidx8, ew8, rank8, counts = _route(logits_t, router_bias[l])
    eidx, ew, rank = eidx8[:TOP_K], ew8[:TOP_K], rank8[:TOP_K]
    counts = counts[:, 0]

    tm_e = 256
    n_assign = n_tok * TOP_K
    n_tiles = -(-n_assign // tm_e) + n_exp
    n_rows = n_tiles * tm_e
    padded = (counts + tm_e - 1) // tm_e * tm_e
    pad_end = jnp.cumsum(padded)
    pad_start = pad_end - padded
    expert_ids = jnp.arange(n_exp, dtype=jnp.int32)
    dest = rank + jnp.sum(jnp.where(eidx[:, :, None] == expert_ids, pad_start.astype(jnp.int32), 0), axis=-1)
    dest_flat = dest.reshape(-1)
    tile_ids = jnp.arange(n_tiles, dtype=jnp.int32)
    tile_e = jnp.minimum(jnp.sum((pad_end[None, :] <= tile_ids[:, None] * tm_e).astype(jnp.int32), axis=1),
                         n_exp - 1).astype(jnp.int32)
    used_tiles = (pad_end[-1:] // tm_e).astype(jnp.int32)
    pad_fill = jnp.maximum(pad_end // tm_e - 1, 0).astype(jnp.int32)
    tail_fill = jnp.minimum(used_tiles + tile_ids, n_tiles - 1)
    fill_tiles = jnp.concatenate([pad_fill, tail_fill])
    n_fill = (n_exp + n_tiles - used_tiles).astype(jnp.int32)

    seg_end_tile = (pad_end // tm_e).astype(jnp.int32)[tile_e]
    next_e = tile_e[jnp.minimum(seg_end_tile, n_tiles - 1)]
    rows_in_tile = counts[tile_e] - (tile_ids - (pad_start // tm_e).astype(jnp.int32)[tile_e]) * tm_e
    tile_flags = jnp.where(seg_end_tile < used_tiles, TILE_HAS_NEXT, 0) \
        + jnp.where(rows_in_tile <= tm_e // 2, TILE_HALF_FULL, 0)
    tile_flags = tile_flags.astype(jnp.int32)

    xs = _dispatch(dest_flat, fill_tiles, n_fill, hp, n_rows, tm_e)
    act = _expert_up(tile_e, used_tiles, next_e, tile_flags, xs, w_gate[l], w_up[l], tm_e,
                     _pick_tile(de, 512, LANES))
    ys = _expert_down(tile_e, used_tiles, next_e, tile_flags, act, w_down[l], tm_e,
                      _pick_tile(d // 2, 2048, LANES))

    tm_s = _pick_tile(n_tok, 640, 16)
    tn_s = _pick_tile(ds, 256, LANES)
    sh_act = _mm([(h2, d, 0)], [(ws_gate[l], 0), (ws_up[l], 0)], [],
                 lambda accs, extras, n, mi: [_silu(accs[0]) * accs[1]],
                 [(jax.ShapeDtypeStruct((n_tok, ds), BF16), pl.BlockSpec((tm_s, tn_s), lambda n, mi: (mi, n)))],
                 m=n_tok, tm=tm_s, tn=tn_s, n_tiles=ds // tn_s, pairs=[(0, 0), (0, 1)], name="shared_up")[0]
    shared = _mm([(sh_act, ds, 0)], [(ws_down[l], 0)], [],
                 lambda accs, extras, n, mi: [accs[0]],
                 [(jax.ShapeDtypeStruct((n_tok, d), F32), pl.BlockSpec((tm_s, tn), lambda n, mi: (mi, n)))],
                 m=n_tok, tm=tm_s, tn=tn, n_tiles=d // tn, pairs=[(0, 0)], name="shared_down")[0]

    ew_t = ew8.T
    y_p = _combine(dest_flat, ys, x1_p, shared, mod_p, ew_t, 5, seq, n_tok, 0)
    y_s = _combine(dest_flat, ys, x1_s, shared, mod_s, ew_t, 5, 1, n_tok, batch * seq)

    w_keep = min(WINDOW, seq)
    qkv_p3 = qkv_p.reshape(batch, seq, aw + 2 * kvw)
    new_k_p = qkv_p3[:, seq - w_keep:, aw:aw + kvw].reshape(1, batch, w_keep, n_kv, hd)
    new_v_p = qkv_p3[:, seq - w_keep:, aw + kvw:].reshape(1, batch, w_keep, n_kv, hd)
    return (y_p.reshape(batch, seq, d), y_s.reshape(db, 1, d), new_k_p, new_v_p,
            new_k_s.reshape(1, db, win, n_kv, hd), new_v_s.reshape(1, db, win, n_kv, hd),
            vn_s.reshape(1, db, 1, gmw))
```

```python
import functools

import jax
import jax.numpy as jnp
from jax import lax
from jax.experimental import pallas as pl
from jax.experimental.pallas import tpu as pltpu

TOP_K = 6
N_EXPERT_GROUPS = 8
TOPK_GROUPS = 4
ROUTED_SCALE = 2.5
WINDOW = 128
EPS = 1e-6

TILE_HAS_NEXT = 1
TILE_HALF_FULL = 2

V7X_VMEM_BYTES = 64 * 1024 * 1024
V7X_VMEM_REQUEST_CAP = 56 * 1024 * 1024
LANES = 128
SUBLANES = 8
N_DMA_PRIORITIES = 2
WEIGHT_STREAM_DMA_PRIORITY = 1

BF16 = jnp.bfloat16
F32 = jnp.float32


def _pick_tile(total, preferred, multiple):
    t = min(preferred, total)
    t -= t % multiple
    while t > multiple and total % t:
        t -= multiple
    assert t > 0 and total % t == 0, (total, preferred, multiple)
    return t


def _params(semantics, vmem_bytes):
    return pltpu.CompilerParams(
        dimension_semantics=semantics,
        vmem_limit_bytes=int(min(V7X_VMEM_REQUEST_CAP, max(vmem_bytes, 16 * 1024 * 1024))))


def _nbytes(shape, dtype):
    n = 1
    for s in shape:
        n *= s
    return n * jnp.dtype(dtype).itemsize


def _mod_spec(mod3, tm, group_rows, width, row_of, col_of):
    if mod3.shape[1] == 1:
        return pl.BlockSpec((1, 1, width), lambda *g: ((row_of(*g) * tm) // group_rows, 0, col_of(*g)))
    return pl.BlockSpec((1, tm, width), lambda *g: (0, row_of(*g), col_of(*g)))


def _adaln_kernel(c_ref, w_ref, b_ref, o_ref, a_scr):
    @pl.when(pl.program_id(0) == 0)
    def _():
        c = c_ref[...]
        a_scr[...] = (c * jax.nn.sigmoid(c)).astype(BF16)

    acc = jnp.dot(a_scr[...], w_ref[...].astype(BF16), preferred_element_type=F32)
    o_ref[...] = acc + b_ref[...]


def _adaln(c_all, w_ada, b_ada):
    rows, d = c_all.shape
    n_out = w_ada.shape[1]
    tn = _pick_tile(n_out, 512, LANES)
    est = 2 * (_nbytes((d, tn), F32) + _nbytes((rows, tn), F32)) + _nbytes((rows, d), F32) * 2 \
        + _nbytes((rows, d), BF16) + _nbytes((d, tn), BF16) + _nbytes((rows, tn), F32)
    return pl.pallas_call(
        _adaln_kernel,
        out_shape=jax.ShapeDtypeStruct((rows, n_out), F32),
        grid=(n_out // tn,),
        in_specs=[pl.BlockSpec((rows, d), lambda n: (0, 0)),
                  pl.BlockSpec((d, tn), lambda n: (0, n)),
                  pl.BlockSpec((1, tn), lambda n: (0, n))],
        out_specs=pl.BlockSpec((rows, tn), lambda n: (0, n)),
        scratch_shapes=[pltpu.VMEM((rows, d), BF16)],
        compiler_params=_params(("arbitrary",), est),
        name="adaln",
    )(c_all, w_ada, b_ada.reshape(1, n_out))


def _rms_mod(x, g, scale, shift):
    y = x * lax.rsqrt(jnp.mean(x * x, axis=-1, keepdims=True) + EPS)
    return (y * g) * (1.0 + scale) + shift


def _modulate_kernel(x_ref, g_ref, sh_ref, sc_ref, o_ref):
    o_ref[...] = _rms_mod(x_ref[...], g_ref[...], sc_ref[0], sh_ref[0]).astype(o_ref.dtype)


def _modulate(x2, g, mod3, shift_chunk, scale_chunk, group_rows):
    m, d = x2.shape
    tm = _pick_tile(min(m, group_rows) if mod3.shape[1] == 1 else m, 256, SUBLANES)
    row_of = lambda i: i
    est = 2 * (_nbytes((tm, d), F32) * 3 + _nbytes((tm, d), BF16)) + 4 * _nbytes((tm, d), F32)
    return pl.pallas_call(
        _modulate_kernel,
        out_shape=jax.ShapeDtypeStruct((m, d), BF16),
        grid=(m // tm,),
        in_specs=[pl.BlockSpec((tm, d), lambda i: (i, 0)),
                  pl.BlockSpec((1, d), lambda i: (0, 0)),
                  _mod_spec(mod3, tm, group_rows, d, row_of, lambda i: shift_chunk),
                  _mod_spec(mod3, tm, group_rows, d, row_of, lambda i: scale_chunk)],
        out_specs=pl.BlockSpec((tm, d), lambda i: (i, 0)),
        compiler_params=_params(("arbitrary",), est),
        name="modulate",
    )(x2, g.reshape(1, d), mod3, mod3)


def _mm_kernel(*refs, n_a, n_w, n_extra, n_out, pairs, epilogue):
    a_refs = refs[:n_a]
    w_refs = refs[n_a:n_a + n_w]
    extra_refs = refs[n_a + n_w:n_a + n_w + n_extra]
    out_refs = refs[n_a + n_w + n_extra:n_a + n_w + n_extra + n_out]
    w_scr = refs[n_a + n_w + n_extra + n_out:]
    n = pl.program_id(0)
    mi = pl.program_id(1)

    @pl.when(mi == 0)
    def _():
        for w_ref, scr in zip(w_refs, w_scr):
            scr[...] = w_ref[...].astype(BF16)

    accs = [jnp.dot(a_refs[ia][...], w_scr[iw][...], preferred_element_type=F32) for ia, iw in pairs]
    outs = epilogue(accs, extra_refs, n, mi)
    for o_ref, val in zip(out_refs, outs):
        o_ref[...] = val.astype(o_ref.dtype)


def _mm(a_list, w_list, extras, epilogue, outs, *, m, tm, tn, n_tiles, pairs, name):
    in_specs, args, est = [], [], 0
    for arr, kw, cb in a_list:
        in_specs.append(pl.BlockSpec((tm, kw), lambda n, mi, cb=cb: (mi, cb)))
        args.append(arr)
        est += 2 * _nbytes((tm, kw), arr.dtype)
    scratch = []
    for arr, off in w_list:
        k = arr.shape[0]
        in_specs.append(pl.BlockSpec((k, tn), lambda n, mi, off=off: (0, n + off)))
        args.append(arr)
        scratch.append(pltpu.VMEM((k, tn), BF16))
        est += 2 * _nbytes((k, tn), F32) + _nbytes((k, tn), BF16)
    for arr, spec in extras:
        in_specs.append(spec)
        args.append(arr)
        est += 2 * _nbytes(spec.block_shape, arr.dtype)
    for sds, spec in outs:
        est += 2 * _nbytes(spec.block_shape, sds.dtype)
    est += (len(pairs) + 2) * _nbytes((tm, tn), F32)
    kern = functools.partial(_mm_kernel, n_a=len(a_list), n_w=len(w_list), n_extra=len(extras),
                             n_out=len(outs), pairs=tuple(pairs), epilogue=epilogue)
    res = pl.pallas_call(
        kern,
        out_shape=[sds for sds, _ in outs],
        grid=(n_tiles, m // tm),
        in_specs=in_specs,
        out_specs=[spec for _, spec in outs],
        scratch_shapes=scratch,
        compiler_params=_params(("arbitrary", "arbitrary"), est),
        name=name,
    )(*args)
    return res


def _gelu(x):
    return x * (lax.erf(x * (2.0 ** -0.5)) + 1.0) * 0.5


def _silu(x):
    return x * jax.nn.sigmoid(x)


def _attn_prompt_kernel(sinks_ref, q_ref, kc_ref, kp_ref, vc_ref, vp_ref, o_ref, *, n_kv, n_grp, hd):
    i = pl.program_id(1)
    w = q_ref.shape[0]
    scale = hd ** -0.5
    rows = n_grp * w
    qi = lax.broadcasted_iota(jnp.int32, (rows, w), 0) % w
    sj = lax.broadcasted_iota(jnp.int32, (rows, w), 1)
    mask_c = sj <= qi
    mask_p = jnp.logical_and(sj >= qi, i > 0)
    ones = jnp.ones((w, w), BF16)
    dn = (((1,), (1,)), ((), ()))
    for kv in range(n_kv):
        ks = slice(kv * hd, (kv + 1) * hd)
        kc = kc_ref[:, ks].astype(BF16)
        kp = kp_ref[:, ks].astype(BF16)
        vc = vc_ref[:, ks].astype(BF16)
        vp = vp_ref[:, ks].astype(BF16)
        cols = [(kv * n_grp + g) * hd for g in range(n_grp)]
        q = jnp.concatenate([q_ref[:, c:c + hd] for c in cols], axis=0).astype(BF16)
        sink = jnp.concatenate([jnp.full((w, 1), sinks_ref[kv, g], F32) for g in range(n_grp)], axis=0)
        lc = jnp.where(mask_c, lax.dot_general(q, kc, dn, preferred_element_type=F32) * scale, -jnp.inf)
        lp = jnp.where(mask_p, lax.dot_general(q, kp, dn, preferred_element_type=F32) * scale, -jnp.inf)
        mx = jnp.maximum(jnp.max(jnp.maximum(lc, lp), axis=-1, keepdims=True), sink)
        pc = jnp.exp(lc - mx).astype(BF16)
        pp = jnp.exp(lp - mx).astype(BF16)
        den = jnp.dot(pc, ones, preferred_element_type=F32) + jnp.dot(pp, ones, preferred_element_type=F32)
        den = den[:, :hd] + jnp.exp(sink - mx)
        o = jnp.dot(pc, vc, preferred_element_type=F32) + jnp.dot(pp, vp, preferred_element_type=F32)
        o = (o / den).astype(o_ref.dtype)
        for g, c in enumerate(cols):
            o_ref[:, c:c + hd] = o[g * w:(g + 1) * w]


def _attn_prompt(qkv, sinks, batch, seq, n_kv, n_grp, hd):
    aw = n_kv * n_grp * hd
    kvw = n_kv * hd
    assert seq % WINDOW == 0 and aw % kvw == 0
    nb = seq // WINDOW
    kcol = aw // kvw
    est = 2 * (_nbytes((WINDOW, aw), F32) + 4 * _nbytes((WINDOW, kvw), F32) + _nbytes((WINDOW, aw), BF16)) \
        + 16 * _nbytes((WINDOW, WINDOW), F32)
    kern = functools.partial(_attn_prompt_kernel, n_kv=n_kv, n_grp=n_grp, hd=hd)
    return pl.pallas_call(
        kern,
        out_shape=jax.ShapeDtypeStruct((batch * seq, aw), BF16),
        grid=(batch, nb),
        in_specs=[pl.BlockSpec(memory_space=pltpu.SMEM),
                  pl.BlockSpec((WINDOW, aw), lambda b, i: (b * nb + i, 0)),
                  pl.BlockSpec((WINDOW, kvw), lambda b, i: (b * nb + i, kcol)),
                  pl.BlockSpec((WINDOW, kvw), lambda b, i: (b * nb + jnp.maximum(i - 1, 0), kcol)),
                  pl.BlockSpec((WINDOW, kvw), lambda b, i: (b * nb + i, kcol + 1)),
                  pl.BlockSpec((WINDOW, kvw), lambda b, i: (b * nb + jnp.maximum(i - 1, 0), kcol + 1))],
        out_specs=pl.BlockSpec((WINDOW, aw), lambda b, i: (b * nb + i, 0)),
        compiler_params=_params(("arbitrary", "arbitrary"), est),
        name="attn_prompt",
    )(sinks, qkv, qkv, qkv, qkv, qkv)


def _attn_sample_kernel(q_ref, kn_ref, vn_ref, ck_ref, cv_ref, sinks_ref, o_ref, ok_ref, ov_ref,
                        *, n_kv, n_grp, hd):
    bs, w, kvw = ck_ref.shape
    nh = n_kv * n_grp
    scale = hd ** -0.5
    head_kv = lax.broadcasted_iota(jnp.int32, (nh, kvw), 0) // n_grp
    lane_kv = lax.broadcasted_iota(jnp.int32, (nh, kvw), 1) // hd
    own = head_kv == lane_kv
    row = lax.broadcasted_iota(jnp.int32, (w, kvw), 0)
    sinks = sinks_ref[...]
    dn = (((1,), (1,)), ((), ()))
    for b in range(bs):
        q = q_ref[b]
        qbd = jnp.where(own, jnp.concatenate([q] * n_kv, axis=-1), 0.0)
        ck = ck_ref[b]
        cv = cv_ref[b]
        kn = kn_ref[pl.ds(b, 1), :]
        vn = vn_ref[pl.ds(b, 1), :]
        logits = lax.dot_general(qbd.astype(BF16), ck.astype(BF16), dn,
                                 preferred_element_type=F32) * scale
        l_new = jnp.sum(qbd.astype(BF16).astype(F32) * kn.astype(BF16).astype(F32),
                        axis=-1, keepdims=True) * scale
        mx = jnp.maximum(jnp.maximum(jnp.max(logits, axis=-1, keepdims=True), l_new), sinks)
        p = jnp.exp(logits - mx)
        p_new = jnp.exp(l_new - mx)
        den = jnp.sum(p, axis=-1, keepdims=True) + p_new + jnp.exp(sinks - mx)
        r = jnp.dot(p.astype(BF16), cv.astype(BF16), preferred_element_type=F32) \
            + p_new.astype(BF16).astype(F32) * vn.astype(BF16).astype(F32)
        r = jnp.where(own, r / den, 0.0)
        o = r[:, 0:hd]
        for kv in range(1, n_kv):
            o = o + r[:, kv * hd:(kv + 1) * hd]
        o_ref[b] = o.astype(o_ref.dtype)
        ok_ref[b] = jnp.where(row == w - 1, kn, pltpu.roll(ck, shift=w - 1, axis=0))
        ov_ref[b] = jnp.where(row == w - 1, vn, pltpu.roll(cv, shift=w - 1, axis=0))


def _attn_sample(q3, qkv_s, cache_k, cache_v, sinks_col, n_kv, n_grp, hd):
    db, w, kvw = cache_k.shape
    nh = n_kv * n_grp
    aw = nh * hd
    kcol = aw // kvw
    bs = _pick_tile(db, 8, SUBLANES)
    est = 2 * (4 * _nbytes((bs, w, kvw), F32) + 2 * _nbytes((bs, kvw), F32) + 2 * _nbytes((bs, nh, LANES), F32)) \
        + 16 * _nbytes((w, kvw), F32)
    kern = functools.partial(_attn_sample_kernel, n_kv=n_kv, n_grp=n_grp, hd=hd)
    return pl.pallas_call(
        kern,
        out_shape=[jax.ShapeDtypeStruct((db, nh, hd), BF16),
                   jax.ShapeDtypeStruct((db, w, kvw), F32),
                   jax.ShapeDtypeStruct((db, w, kvw), F32)],
        grid=(db // bs,),
        in_specs=[pl.BlockSpec((bs, nh, hd), lambda i: (i, 0, 0)),
                  pl.BlockSpec((bs, kvw), lambda i: (i, kcol)),
                  pl.BlockSpec((bs, kvw), lambda i: (i, kcol + 1)),
                  pl.BlockSpec((bs, w, kvw), lambda i: (i, 0, 0)),
                  pl.BlockSpec((bs, w, kvw), lambda i: (i, 0, 0)),
                  pl.BlockSpec((nh, 1), lambda i: (0, 0))],
        out_specs=[pl.BlockSpec((bs, nh, hd), lambda i: (i, 0, 0)),
                   pl.BlockSpec((bs, w, kvw), lambda i: (i, 0, 0)),
                   pl.BlockSpec((bs, w, kvw), lambda i: (i, 0, 0))],
        compiler_params=_params(("arbitrary",), est),
        name="attn_sample",
    )(q3, qkv_s, qkv_s, cache_k, cache_v, sinks_col)


def _layer_norm(v, g, b):
    mu = jnp.mean(v, axis=-1, keepdims=True)
    c = v - mu
    var = jnp.mean(c * c, axis=-1, keepdims=True)
    return c * lax.rsqrt(var + EPS) * g + b


def _gmlp_prompt_kernel(u_ref, v_ref, g_ref, b_ref, ws_ref, bs_ref, o_ref, *, n_groups):
    ch = u_ref.shape[0]
    gd = u_ref.shape[1] // n_groups
    vn = _layer_norm(v_ref[...].astype(F32), g_ref[...], b_ref[...])
    ti = lax.broadcasted_iota(jnp.int32, (ch, ch), 0)
    si = lax.broadcasted_iota(jnp.int32, (ch, ch), 1)
    causal = si <= ti
    for g in range(n_groups):
        cs = slice(g * gd, (g + 1) * gd)
        wc = jnp.where(causal, ws_ref[g], 0.0).astype(BF16)
        mixed = jnp.dot(wc, vn[:, cs].astype(BF16), preferred_element_type=F32) + bs_ref[:, g:g + 1]
        o_ref[:, cs] = (u_ref[:, cs].astype(F32) * mixed).astype(o_ref.dtype)


def _gmlp_prompt(ugv, ln_g, ln_b, w_spatial, b_spatial_t):
    m, two_w = ugv.shape
    gmw = two_w // 2
    n_groups, ch, _ = w_spatial.shape
    assert m % ch == 0
    est = 2 * (2 * _nbytes((ch, gmw), ugv.dtype) + _nbytes((ch, gmw), BF16) + _nbytes(w_spatial.shape, F32)) \
        + 6 * _nbytes((ch, gmw), F32)
    kern = functools.partial(_gmlp_prompt_kernel, n_groups=n_groups)
    return pl.pallas_call(
        kern,
        out_shape=jax.ShapeDtypeStruct((m, gmw), BF16),
        grid=(m // ch,),
        in_specs=[pl.BlockSpec((ch, gmw), lambda i: (i, 0)),
                  pl.BlockSpec((ch, gmw), lambda i: (i, 1)),
                  pl.BlockSpec((1, gmw), lambda i: (0, 0)),
                  pl.BlockSpec((1, gmw), lambda i: (0, 0)),
                  pl.BlockSpec((n_groups, ch, ch), lambda i: (0, 0, 0)),
                  pl.BlockSpec((ch, n_groups), lambda i: (0, 0))],
        out_specs=pl.BlockSpec((ch, gmw), lambda i: (i, 0)),
        compiler_params=_params(("arbitrary",), est),
        name="gmlp_prompt",
    )(ugv, ugv, ln_g.reshape(1, gmw), ln_b.reshape(1, gmw), w_spatial, b_spatial_t)


def _gmlp_sample_kernel(u_ref, v_ref, g_ref, b_ref, w0_ref, b0_ref, o_ref, vn_ref):
    vn = _layer_norm(v_ref[...], g_ref[...], b_ref[...])
    vn_ref[...] = vn
    o_ref[...] = (u_ref[...] * (w0_ref[...] * vn + b0_ref[...])).astype(o_ref.dtype)


def _gmlp_sample(ugv, ln_g, ln_b, w00, b0):
    m, two_w = ugv.shape
    gmw = two_w // 2
    est = 2 * (3 * _nbytes((m, gmw), F32) + _nbytes((m, gmw), BF16)) + 4 * _nbytes((m, gmw), F32)
    row = pl.BlockSpec((1, gmw), lambda i: (0, 0))
    return pl.pallas_call(
        _gmlp_sample_kernel,
        out_shape=[jax.ShapeDtypeStruct((m, gmw), BF16), jax.ShapeDtypeStruct((m, gmw), F32)],
        grid=(1,),
        in_specs=[pl.BlockSpec((m, gmw), lambda i: (0, 0)), pl.BlockSpec((m, gmw), lambda i: (0, 1)),
                  row, row, row, row],
        out_specs=[pl.BlockSpec((m, gmw), lambda i: (0, 0)), pl.BlockSpec((m, gmw), lambda i: (0, 0))],
        compiler_params=_params(("arbitrary",), est),
        name="gmlp_sample",
    )(ugv, ugv, ln_g.reshape(1, gmw), ln_b.reshape(1, gmw), w00, b0)


def _pack_bf16_pair(left, right):
    lb = pltpu.bitcast(left.astype(BF16).astype(F32), jnp.uint32)
    rb = pltpu.bitcast(right.astype(BF16).astype(F32), jnp.uint32)
    return lb | (rb >> 16)


def _unpack_bf16_pair(word):
    left = pltpu.bitcast(word & jnp.uint32(0xFFFF0000), F32)
    right = pltpu.bitcast(word << 16, F32)
    return left, right


def _split_bf16(x):
    hi = x.astype(BF16)
    lo = (x - hi.astype(F32)).astype(BF16)
    return hi, lo


def _ffn_norm_kernel(x_ref, g_ref, sh_ref, sc_ref, wr_ref, h_ref, hp_ref, lg_ref):
    h = _rms_mod(x_ref[...], g_ref[...], sc_ref[0], sh_ref[0])
    half = h.shape[1] // 2
    h_ref[...] = h.astype(BF16)
    hp_ref[...] = _pack_bf16_pair(h[:, :half], h[:, half:])
    h_hi, h_lo = _split_bf16(h)
    w_hi, w_lo = _split_bf16(wr_ref[...])
    dn = (((1,), (1,)), ((), ()))
    lg_ref[...] = lax.dot_general(w_hi, h_hi, dn, preferred_element_type=F32) \
        + lax.dot_general(w_hi, h_lo, dn, preferred_element_type=F32) \
        + lax.dot_general(w_lo, h_hi, dn, preferred_element_type=F32)


def _ffn_norm(x2, g, mod3, shift_chunk, scale_chunk, group_rows, w_router_t):
    m, d = x2.shape
    e = w_router_t.shape[0]
    tm = _pick_tile(min(m, group_rows) if mod3.shape[1] == 1 else m, 256, LANES)
    row_of = lambda i: i
    est = 2 * (3 * _nbytes((tm, d), F32) + 2 * _nbytes((tm, d), BF16) + _nbytes((e, d), F32)) \
        + 6 * _nbytes((tm, d), F32)
    return pl.pallas_call(
        _ffn_norm_kernel,
        out_shape=[jax.ShapeDtypeStruct((m, d), BF16),
                   jax.ShapeDtypeStruct((m, d // 2), jnp.uint32),
                   jax.ShapeDtypeStruct((e, m), F32)],
        grid=(m // tm,),
        in_specs=[pl.BlockSpec((tm, d), lambda i: (i, 0)),
                  pl.BlockSpec((1, d), lambda i: (0, 0)),
                  _mod_spec(mod3, tm, group_rows, d, row_of, lambda i: shift_chunk),
                  _mod_spec(mod3, tm, group_rows, d, row_of, lambda i: scale_chunk),
                  pl.BlockSpec((e, d), lambda i: (0, 0))],
        out_specs=[pl.BlockSpec((tm, d), lambda i: (i, 0)),
                   pl.BlockSpec((tm, d // 2), lambda i: (i, 0)),
                   pl.BlockSpec((e, tm), lambda i: (0, i))],
        compiler_params=_params(("arbitrary",), est),
        name="ffn_norm_router",
    )(x2, g.reshape(1, d), mod3, mod3, w_router_t)


def _first_max(vals, idx):
    mx = jnp.max(vals, axis=0, keepdims=True)
    first = jnp.min(jnp.where(vals == mx, idx, jnp.int32(2 ** 30)), axis=0, keepdims=True)
    return mx, first


def _route_kernel(lg_ref, bias_ref, eidx_ref, ew_ref, rank_ref, cnt_ref, base_scr,
                  *, n_groups, topk_groups, top_k, scale):
    e, tr = lg_ref.shape
    per = e // n_groups

    @pl.when(pl.program_id(0) == 0)
    def _():
        base_scr[...] = jnp.zeros_like(base_scr)

    scores = jax.nn.sigmoid(lg_ref[...])
    biased = scores + bias_ref[...]
    eid = lax.broadcasted_iota(jnp.int32, (e, tr), 0)
    neg = jnp.float32(-jnp.inf)

    grp_rows = []
    bid = lax.broadcasted_iota(jnp.int32, (per, tr), 0)
    for g in range(n_groups):
        blk = biased[g * per:(g + 1) * per]
        m1, i1 = _first_max(blk, bid)
        m2 = jnp.max(jnp.where(bid == i1, neg, blk), axis=0, keepdims=True)
        grp_rows.append(m1 + m2)
    grp = jnp.concatenate(grp_rows, axis=0)
    gid = lax.broadcasted_iota(jnp.int32, (n_groups, tr), 0)
    gsel = jnp.zeros((n_groups, tr), F32)
    work = grp
    for _ in range(topk_groups):
        _, gi = _first_max(work, gid)
        hit = gid == gi
        gsel = jnp.where(hit, 1.0, gsel)
        work = jnp.where(hit, neg, work)
    emask = jnp.concatenate(
        [jnp.broadcast_to(gsel[g:g + 1], (per, tr)) for g in range(n_groups)], axis=0)
    masked = jnp.where(emask > 0.0, biased, neg)

    onehots, idxs, wts = [], [], []
    for _ in range(top_k):
        _, ei = _first_max(masked, eid)
        hit = eid == ei
        onehots.append(hit)
        idxs.append(ei)
        wts.append(jnp.sum(jnp.where(hit, scores, 0.0), axis=0, keepdims=True))
        masked = jnp.where(hit, neg, masked)
    wsum = wts[0]
    for wk in wts[1:]:
        wsum = wsum + wk

    chosen = onehots[0]
    for oh in onehots[1:]:
        chosen = jnp.logical_or(chosen, oh)
    chosen_f = jnp.where(chosen, 1.0, 0.0)
    si = lax.broadcasted_iota(jnp.int32, (tr, tr), 0)
    ti = lax.broadcasted_iota(jnp.int32, (tr, tr), 1)
    upper = jnp.where(si < ti, 1.0, 0.0).astype(BF16)
    prefix = jnp.dot(chosen_f.astype(BF16), upper, preferred_element_type=F32)
    pos = prefix + base_scr[:, 0:1]
    for k in range(top_k):
        eidx_ref[k:k + 1, :] = idxs[k]
        ew_ref[k:k + 1, :] = wts[k] / wsum * scale
        rank_ref[k:k + 1, :] = jnp.sum(jnp.where(onehots[k], pos, 0.0), axis=0, keepdims=True).astype(jnp.int32)
    for k in range(top_k, eidx_ref.shape[0]):
        eidx_ref[k:k + 1, :] = jnp.zeros((1, tr), jnp.int32)
        ew_ref[k:k + 1, :] = jnp.zeros((1, tr), F32)
        rank_ref[k:k + 1, :] = jnp.zeros((1, tr), jnp.int32)
    base_scr[...] = base_scr[...] + jnp.sum(chosen_f, axis=1, keepdims=True)
    cnt_ref[...] = base_scr[...].astype(jnp.int32)


def _route(logits_t, router_bias):
    e, n = logits_t.shape
    tr = _pick_tile(n, 640, LANES)
    rows = SUBLANES
    assert TOP_K <= rows and e % N_EXPERT_GROUPS == 0
    kern = functools.partial(_route_kernel, n_groups=N_EXPERT_GROUPS, topk_groups=TOPK_GROUPS,
                             top_k=TOP_K, scale=ROUTED_SCALE)
    est = 2 * (_nbytes((e, tr), F32) + 3 * _nbytes((rows, tr), F32)) + 24 * _nbytes((e, tr), F32) \
        + 3 * _nbytes((tr, tr), F32)
    out_row = pl.BlockSpec((rows, tr), lambda i: (0, i))
    return pl.pallas_call(
        kern,
        out_shape=[jax.ShapeDtypeStruct((rows, n), jnp.int32),
                   jax.ShapeDtypeStruct((rows, n), F32),
                   jax.ShapeDtypeStruct((rows, n), jnp.int32),
                   jax.ShapeDtypeStruct((e, LANES), jnp.int32)],
        grid=(n // tr,),
        in_specs=[pl.BlockSpec((e, tr), lambda i: (0, i)),
                  pl.BlockSpec((e, 1), lambda i: (0, 0))],
        out_specs=[out_row, out_row, out_row, pl.BlockSpec((e, LANES), lambda i: (0, 0))],
        scratch_shapes=[pltpu.VMEM((e, LANES), F32)],
        compiler_params=_params(("arbitrary",), est),
        name="route",
    )(logits_t, router_bias.reshape(e, 1))


def _dispatch_kernel(dest_ref, fill_ref, nfill_ref, hp_ref, xs_ref, zero_scr, sem, *, top_k, n_tok):
    step = pl.program_id(0)
    tt = hp_ref.shape[0]
    fill_rows = zero_scr.shape[0]

    @pl.when(step == 0)
    def _():
        zero_scr[...] = jnp.zeros_like(zero_scr)
        n_fill = nfill_ref[0]

        def start_fill(i, c):
            row0 = pl.multiple_of(fill_ref[i] * fill_rows, fill_rows)
            pltpu.make_async_copy(zero_scr, xs_ref.at[pl.ds(row0, fill_rows)], sem).start()
            return c

        def wait_fill(i, c):
            pltpu.make_async_copy(zero_scr, xs_ref.at[pl.ds(0, fill_rows)], sem).wait()
            return c

        lax.fori_loop(0, n_fill, start_fill, 0)
        lax.fori_loop(0, n_fill, wait_fill, 0)

    base = step * tt

    def start(r, c):
        for k in range(top_k):
            pltpu.make_async_copy(hp_ref.at[pl.ds(r, 1)],
                                  xs_ref.at[pl.ds(dest_ref[k * n_tok + base + r], 1)], sem
                                  ).start(priority=k % N_DMA_PRIORITIES)
        return c

    def wait(r, c):
        for k in range(top_k):
            pltpu.make_async_copy(hp_ref.at[pl.ds(r, 1)], xs_ref.at[pl.ds(0, 1)], sem).wait()
        return c

    lax.fori_loop(0, tt, start, 0)
    lax.fori_loop(0, tt, wait, 0)


def _dispatch(dest_flat, fill_tiles, n_fill, hp, n_rows, fill_rows):
    n_tok, half = hp.shape
    tt = _pick_tile(n_tok, 640, SUBLANES)
    kern = functools.partial(_dispatch_kernel, top_k=TOP_K, n_tok=n_tok)
    est = 2 * _nbytes((tt, half), hp.dtype) + _nbytes((fill_rows, half), hp.dtype)
    return pl.pallas_call(
        kern,
        out_shape=jax.ShapeDtypeStruct((n_rows, half), hp.dtype),
        grid_spec=pltpu.PrefetchScalarGridSpec(
            num_scalar_prefetch=3,
            grid=(n_tok // tt,),
            in_specs=[pl.BlockSpec((tt, half), lambda i, dest, fill, nfill: (i, 0))],
            out_specs=pl.BlockSpec(memory_space=pl.ANY),
            scratch_shapes=[pltpu.VMEM((fill_rows, half), hp.dtype), pltpu.SemaphoreType.DMA]),
        compiler_params=_params(("arbitrary",), est),
        name="moe_dispatch",
    )(dest_flat, fill_tiles, n_fill, hp)


def _stream_expert_weights(tile_e_ref, next_e_ref, flags_ref, blk_ref, w_refs, col_of, stage, sems):
    j = pl.program_id(0)
    t = pl.program_id(1)
    n_pass = pl.num_programs(0)
    width = stage.shape[-1]

    def copies(e, jj, slot):
        return [pltpu.make_async_copy(
            w_ref.at[e, :, pl.ds(pl.multiple_of(col(jj), width), width)], stage.at[slot, i], sems.at[slot])
            for i, (w_ref, col) in enumerate(zip(w_refs, col_of))]

    @pl.when(jnp.logical_and(j == 0, t == 0))
    def _():
        blk_ref[0] = 0
        for c in copies(tile_e_ref[0], 0, 0):
            c.start(priority=WEIGHT_STREAM_DMA_PRIORITY)

    slot = blk_ref[0] % 2
    for c in copies(tile_e_ref[t], j, slot):
        c.wait()
    in_pass = (flags_ref[t] & TILE_HAS_NEXT) > 0
    e_next = jnp.where(in_pass, next_e_ref[t], tile_e_ref[0])
    j_next = jnp.where(in_pass, j, j + 1)

    @pl.when(jnp.logical_or(in_pass, j + 1 < n_pass))
    def _():
        for c in copies(e_next, j_next, 1 - slot):
            c.start(priority=WEIGHT_STREAM_DMA_PRIORITY)

    blk_ref[0] = blk_ref[0] + 1


def _expert_up_kernel(tile_e_ref, used_ref, next_e_ref, flags_ref, xs_ref, wg_ref, wu_ref, o_ref,
                      stage, sems, blk_ref):
    t = pl.program_id(1)
    n_used = used_ref[0]
    fc = o_ref.shape[1]

    @pl.when(t < n_used)
    def _():
        prev = tile_e_ref[jnp.maximum(t - 1, 0)]
        fresh = jnp.logical_or(t == 0, tile_e_ref[t] != prev)

        @pl.when(fresh)
        def _():
            col = lambda jj: jj * fc
            _stream_expert_weights(tile_e_ref, next_e_ref, flags_ref, blk_ref, [wg_ref, wu_ref], [col, col],
                                   stage, sems)

        half = xs_ref.shape[1]
        tm = xs_ref.shape[0]
        slot = (blk_ref[0] + 1) % 2

        def project(rows):
            xl, xr = _unpack_bf16_pair(xs_ref[:rows])
            xl = xl.astype(BF16)
            xr = xr.astype(BF16)
            gate = jnp.dot(xl, stage[slot, 0, :half].astype(BF16), preferred_element_type=F32) \
                + jnp.dot(xr, stage[slot, 0, half:].astype(BF16), preferred_element_type=F32)
            up = jnp.dot(xl, stage[slot, 1, :half].astype(BF16), preferred_element_type=F32) \
                + jnp.dot(xr, stage[slot, 1, half:].astype(BF16), preferred_element_type=F32)
            o_ref[:rows] = (_silu(gate) * up).astype(o_ref.dtype)
            if rows < tm:
                o_ref[rows:] = jnp.zeros((tm - rows, fc), o_ref.dtype)

        half_full = (flags_ref[t] & TILE_HALF_FULL) > 0

        @pl.when(half_full)
        def _():
            project(tm // 2)

        @pl.when(jnp.logical_not(half_full))
        def _():
            project(tm)

    @pl.when(t >= n_used)
    def _():
        o_ref[...] = jnp.zeros_like(o_ref)


def _expert_up(tile_e, n_used, next_e, tile_flags, xs, w_gate, w_up, tm, fc):
    p, half = xs.shape
    e, d, de = w_gate.shape
    n_t = p // tm
    clamp = lambda t, used: jnp.minimum(t, used[0] - 1)
    est = 2 * (_nbytes((tm, half), jnp.uint32) + 2 * _nbytes((d, fc), F32)
               + _nbytes((tm, fc), BF16)) + 2 * _nbytes((d, fc), BF16) + 2 * _nbytes((tm, d), BF16) \
        + 4 * _nbytes((tm, fc), F32) + 2 * _nbytes((tm, half), F32)
    return pl.pallas_call(
        _expert_up_kernel,
        out_shape=jax.ShapeDtypeStruct((p, de), BF16),
        grid_spec=pltpu.PrefetchScalarGridSpec(
            num_scalar_prefetch=4,
            grid=(de // fc, n_t),
            in_specs=[pl.BlockSpec((tm, half), lambda j, t, te, used, ne, hn: (clamp(t, used), 0)),
                      pl.BlockSpec(memory_space=pl.ANY),
                      pl.BlockSpec(memory_space=pl.ANY)],
            out_specs=pl.BlockSpec((tm, fc), lambda j, t, te, used, ne, hn: (t, j)),
            scratch_shapes=[pltpu.VMEM((2, 2, d, fc), F32),
                            pltpu.SemaphoreType.DMA((2,)), pltpu.SMEM((1,), jnp.int32)]),
        compiler_params=_params(("arbitrary", "arbitrary"), est),
        name="expert_up",
    )(tile_e, n_used, next_e, tile_flags, xs, w_gate, w_up)


def _expert_down_kernel(tile_e_ref, used_ref, next_e_ref, flags_ref, a_ref, wd_ref, o_ref,
                        stage, sems, blk_ref):
    t = pl.program_id(1)
    n_used = used_ref[0]
    nc = o_ref.shape[1]
    half = wd_ref.shape[2] // 2

    @pl.when(t < n_used)
    def _():
        prev = tile_e_ref[jnp.maximum(t - 1, 0)]
        fresh = jnp.logical_or(t == 0, tile_e_ref[t] != prev)

        @pl.when(fresh)
        def _():
            _stream_expert_weights(tile_e_ref, next_e_ref, flags_ref, blk_ref, [wd_ref, wd_ref],
                                   [lambda cc: cc * nc, lambda cc: half + cc * nc],
                                   stage, sems)

        tm = a_ref.shape[0]
        slot = (blk_ref[0] + 1) % 2

        def project(rows):
            a = a_ref[:rows]
            yl = jnp.dot(a, stage[slot, 0].astype(BF16), preferred_element_type=F32)
            yr = jnp.dot(a, stage[slot, 1].astype(BF16), preferred_element_type=F32)
            o_ref[:rows] = _pack_bf16_pair(yl, yr)
            if rows < tm:
                o_ref[rows:] = jnp.zeros((tm - rows, nc), o_ref.dtype)

        half_full = (flags_ref[t] & TILE_HALF_FULL) > 0

        @pl.when(half_full)
        def _():
            project(tm // 2)

        @pl.when(jnp.logical_not(half_full))
        def _():
            project(tm)

    @pl.when(t >= n_used)
    def _():
        o_ref[...] = jnp.zeros_like(o_ref)


def _expert_down(tile_e, n_used, next_e, tile_flags, act, w_down, tm, nc):
    p, de = act.shape
    e, _, d = w_down.shape
    half = d // 2
    n_t = p // tm
    n_c = half // nc
    clamp = lambda t, used: jnp.minimum(t, used[0] - 1)
    est = 2 * (_nbytes((tm, de), BF16) + 2 * _nbytes((de, nc), F32) + _nbytes((tm, nc), jnp.uint32)) \
        + 2 * _nbytes((de, nc), BF16) + 3 * _nbytes((tm, nc), F32)
    return pl.pallas_call(
        _expert_down_kernel,
        out_shape=jax.ShapeDtypeStruct((p, half), jnp.uint32),
        grid_spec=pltpu.PrefetchScalarGridSpec(
            num_scalar_prefetch=4,
            grid=(n_c, n_t),
            in_specs=[pl.BlockSpec((tm, de), lambda c, t, te, used, ne, hn: (clamp(t, used), 0)),
                      pl.BlockSpec(memory_space=pl.ANY)],
            out_specs=pl.BlockSpec((tm, nc), lambda c, t, te, used, ne, hn: (t, c)),
            scratch_shapes=[pltpu.VMEM((2, 2, de, nc), F32),
                            pltpu.SemaphoreType.DMA((2,)), pltpu.SMEM((1,), jnp.int32)]),
        compiler_params=_params(("arbitrary", "arbitrary"), est),
        name="expert_down",
    )(tile_e, n_used, next_e, tile_flags, act, w_down)


def _combine_kernel(dest_ref, ys_ref, x_ref, sh_ref, gf_ref, ew_ref, o_ref, buf, sem, *, top_k, n_tok, tok0):
    tt = x_ref.shape[0]
    half = ys_ref.shape[1]
    step = pl.program_id(0)
    slot = step % 2

    def token_copies(for_step, into, r, start):
        base = tok0 + for_step * tt
        for k in range(top_k):
            row = dest_ref[k * n_tok + base + r] if start else 0
            cp = pltpu.make_async_copy(ys_ref.at[pl.ds(row, 1)], buf.at[into, k, pl.ds(r, 1)], sem.at[into])
            if start:
                cp.start(priority=k % N_DMA_PRIORITIES)
            else:
                cp.wait()

    def gather(for_step, into, start):
        def body(r, c):
            token_copies(for_step, into, r, start)
            return c

        lax.fori_loop(0, tt, body, 0)

    per_row_gate = gf_ref.shape[1] != 1

    def reduce_rows(c, fetch_next):
        row0 = pl.multiple_of(c * SUBLANES, SUBLANES)
        if fetch_next:
            for r in range(SUBLANES):
                token_copies(step + 1, 1 - slot, row0 + r, True)
        rows = pl.ds(row0, SUBLANES)
        ew = ew_ref[rows, :]
        gf = gf_ref[0, rows, :] if per_row_gate else gf_ref[0]
        for side, cols in ((0, slice(0, half)), (1, slice(half, 2 * half))):
            acc = None
            for k in range(top_k):
                term = _unpack_bf16_pair(buf[slot, k, rows, :])[side] * ew[:, k:k + 1]
                acc = term if acc is None else acc + term
            o_ref[rows, cols] = x_ref[rows, cols] + gf[:, cols] * (acc + sh_ref[rows, cols])

    @pl.when(step == 0)
    def _():
        gather(0, 0, True)

    gather(step, slot, False)
    more = step + 1 < pl.num_programs(0)

    @pl.when(more)
    def _():
        lax.fori_loop(0, tt // SUBLANES, lambda c, carry: (reduce_rows(c, True), carry)[1], 0)

    @pl.when(jnp.logical_not(more))
    def _():
        lax.fori_loop(0, tt // SUBLANES, lambda c, carry: (reduce_rows(c, False), carry)[1], 0)


def _combine(dest_flat, ys, x2, shared, mod3, ew_t, gate_chunk, group_rows, n_tok, tok0):
    m, d = x2.shape
    half = d // 2
    ew_rows = ew_t.shape[1]
    tt = _pick_tile(min(m, group_rows) if mod3.shape[1] == 1 else m, 128, SUBLANES)
    sh_blk0 = tok0 // tt
    assert tok0 % tt == 0
    row_of = lambda i, dest: i
    kern = functools.partial(_combine_kernel, top_k=TOP_K, n_tok=n_tok, tok0=tok0)
    est = 2 * (3 * _nbytes((tt, d), F32) + _nbytes((tt, d), F32)) + 2 * _nbytes((TOP_K, tt, half), jnp.uint32) \
        + 6 * _nbytes((tt, d), F32)
    return pl.pallas_call(
        kern,
        out_shape=jax.ShapeDtypeStruct((m, d), F32),
        grid_spec=pltpu.PrefetchScalarGridSpec(
            num_scalar_prefetch=1,
            grid=(m // tt,),
            in_specs=[pl.BlockSpec(memory_space=pl.ANY),
                      pl.BlockSpec((tt, d), lambda i, dest: (i, 0)),
                      pl.BlockSpec((tt, d), lambda i, dest: (i + sh_blk0, 0)),
                      _mod_spec(mod3, tt, group_rows, d, row_of, lambda i, dest: gate_chunk),
                      pl.BlockSpec((tt, ew_rows), lambda i, dest: (i + sh_blk0, 0))],
            out_specs=pl.BlockSpec((tt, d), lambda i, dest: (i, 0)),
            scratch_shapes=[pltpu.VMEM((2, TOP_K, tt, half), jnp.uint32), pltpu.SemaphoreType.DMA((2,))]),
        compiler_params=_params(("arbitrary",), est),
        name="moe_combine",
    )(dest_flat, ys, x2, shared, mod3, ew_t)


def _token_mixing(x2, mod3, group_rows, p, attend):
    m, d = x2.shape
    aw, kvw, gmw, hd = p["aw"], p["kvw"], p["gmw"], p["hd"]
    tn = 512
    assert aw % tn == 0 and (2 * kvw) % tn == 0 and gmw % tn == 0 and d % tn == 0
    tm = _pick_tile(min(m, group_rows) if mod3.shape[1] == 1 else m, 1024, 16)
    h = _modulate(x2, p["norm_mix_g"], mod3, 0, 1, group_rows)
    w_in = p["w_in"]

    qkv_w = aw + 2 * kvw

    def qkv_epilogue(accs, extras, n, mi):
        gain_ref, flag_ref, bd_ref = extras
        z = accs[0]
        sq_hi, sq_lo = _split_bf16(z * z)
        ss = jnp.dot(sq_hi, bd_ref[...], preferred_element_type=F32) \
            + jnp.dot(sq_lo, bd_ref[...], preferred_element_type=F32)
        inv = lax.rsqrt(ss * (1.0 / hd) + EPS)
        return [z * jnp.where(flag_ref[...] > 0.0, inv, 1.0) * gain_ref[...]]

    qkv = _mm([(h, d, 0)], [(w_in, 0)],
              [(p["qkv_gain"], pl.BlockSpec((1, tn), lambda n, mi: (0, n))),
               (p["qkv_flag"], pl.BlockSpec((1, tn), lambda n, mi: (0, n))),
               (p["head_ones"], pl.BlockSpec((tn, tn), lambda n, mi: (0, 0)))],
              qkv_epilogue,
              [(jax.ShapeDtypeStruct((m, qkv_w), F32), pl.BlockSpec((tm, tn), lambda n, mi: (mi, n)))],
              m=m, tm=tm, tn=tn, n_tiles=qkv_w // tn, pairs=[(0, 0)], name="in_proj_qkv")[0]

    ugv_dtype = BF16 if mod3.shape[1] == 1 else F32
    ugv = _mm([(h, d, 0)], [(w_in, qkv_w // tn)], [],
              lambda accs, extras, n, mi: [_gelu(accs[0])],
              [(jax.ShapeDtypeStruct((m, 2 * gmw), ugv_dtype), pl.BlockSpec((tm, tn), lambda n, mi: (mi, n)))],
              m=m, tm=tm, tn=tn, n_tiles=2 * gmw // tn, pairs=[(0, 0)], name="in_proj_gmlp")[0]

    gates = _mm([(h, d, 0)], [(w_in, (qkv_w + 2 * gmw) // tn)], [],
                lambda accs, extras, n, mi: [jax.nn.sigmoid(accs[0])],
                [(jax.ShapeDtypeStruct((m, 2 * d), BF16), pl.BlockSpec((tm, tn), lambda n, mi: (mi, n)))],
                m=m, tm=tm, tn=tn, n_tiles=2 * d // tn, pairs=[(0, 0)], name="in_proj_gates")[0]

    o_attn, o_gmlp, aux = attend(qkv, ugv)

    nd = d // tn
    merged = _mm([(o_attn, aw, 0), (o_gmlp, gmw, 0)], [(p["w_branch_attn"], 0), (p["w_branch_gmlp"], 0)],
                 [(gates, pl.BlockSpec((tm, tn), lambda n, mi: (mi, n))),
                  (gates, pl.BlockSpec((tm, tn), lambda n, mi: (mi, n + nd)))],
                 lambda accs, extras, n, mi: [extras[0][...].astype(F32) * accs[0]
                                              + extras[1][...].astype(F32) * accs[1]],
                 [(jax.ShapeDtypeStruct((m, d), BF16), pl.BlockSpec((tm, tn), lambda n, mi: (mi, n)))],
                 m=m, tm=tm, tn=tn, n_tiles=nd, pairs=[(0, 0), (1, 1)], name="branch_merge")[0]

    x1 = _mm([(merged, d, 0)], [(p["w_out"], 0)],
             [(x2, pl.BlockSpec((tm, tn), lambda n, mi: (mi, n))),
              (mod3, _mod_spec(mod3, tm, group_rows, tn, lambda n, mi: mi, lambda n, mi: 2 * nd + n))],
             lambda accs, extras, n, mi: [extras[0][...] + extras[1][0] * accs[0]],
             [(jax.ShapeDtypeStruct((m, d), F32), pl.BlockSpec((tm, tn), lambda n, mi: (mi, n)))],
             m=m, tm=tm, tn=tn, n_tiles=nd, pairs=[(0, 0)], name="out_proj")[0]
    return x1, qkv, aux


def kernel(x_prompt, x_sample, cache_k_win, cache_v_win, c_prompt, c_sample, norm_mix_g, norm_ffn_g,
           w_ada, b_ada, w_in, q_norm_g, k_norm_g, attn_sinks, gm_ln_g, gm_ln_b, w_spatial, b_spatial,
           w_branch_attn, w_branch_gmlp, w_out, w_router, router_bias, w_gate, w_up, w_down,
           ws_gate, ws_up, ws_down):
    depth = norm_mix_g.shape[0]
    assert depth == 1, "single-layer step"
    batch, seq, d = x_prompt.shape
    db, t_new, _ = x_sample.shape
    assert t_new == 1, "one new token per sequence"
    _, _, win, n_kv, hd = cache_k_win.shape
    n_grp = attn_sinks.shape[-1]
    aw, kvw = n_kv * n_grp * hd, n_kv * hd
    gmw = gm_ln_g.shape[-1]
    n_groups, ch, _ = w_spatial.shape[1:]
    n_exp = w_router.shape[-1]
    de = w_gate.shape[-1]
    ds = ws_gate.shape[-1]
    tn = 512
    l = 0

    n_c = batch + db
    rows = -(-n_c // 16) * 16
    c_all = jnp.concatenate([c_prompt, c_sample, jnp.zeros((rows - n_c, d), F32)], axis=0)
    mods = _adaln(c_all, w_ada[l], b_ada[l])
    mod_p = mods[:batch].reshape(batch, 1, 6 * d)
    mod_s = mods[batch:n_c].reshape(1, db, 6 * d)

    gq = jnp.tile(q_norm_g[l], aw // hd)
    gk = jnp.tile(k_norm_g[l], kvw // hd)
    qkv_gain = jnp.concatenate([gq, gk, jnp.ones((kvw,), F32)]).reshape(1, -1)
    qkv_flag = jnp.concatenate([jnp.ones((aw + kvw,), F32), jnp.zeros((kvw,), F32)]).reshape(1, -1)
    hid = jnp.arange(tn) // hd
    head_ones = (hid[:, None] == hid[None, :]).astype(BF16)
    params = dict(aw=aw, kvw=kvw, gmw=gmw, hd=hd, norm_mix_g=norm_mix_g[l], w_in=w_in[l],
                  qkv_gain=qkv_gain, qkv_flag=qkv_flag, head_ones=head_ones,
                  w_branch_attn=w_branch_attn[l], w_branch_gmlp=w_branch_gmlp[l], w_out=w_out[l])
    sinks = attn_sinks[l]

    def attend_prompt(qkv, ugv):
        o_attn = _attn_prompt(qkv, sinks, batch, seq, n_kv, n_grp, hd)
        o_gmlp = _gmlp_prompt(ugv, gm_ln_g[l], gm_ln_b[l], w_spatial[l], b_spatial[l].T)
        return o_attn, o_gmlp, None

    xp2 = x_prompt.reshape(batch * seq, d)
    x1_p, qkv_p, _ = _token_mixing(xp2, mod_p, seq, params, attend_prompt)

    ck = cache_k_win[l].reshape(db, win, kvw)
    cv = cache_v_win[l].reshape(db, win, kvw)

    def attend_sample(qkv, ugv):
        q3 = qkv[:, :aw].reshape(db, n_kv * n_grp, hd)
        o3, new_k, new_v = _attn_sample(q3, qkv, ck, cv, sinks.reshape(-1, 1), n_kv, n_grp, hd)
        w00 = jnp.repeat(w_spatial[l][:, 0, 0], gmw // n_groups).reshape(1, gmw)
        b0 = jnp.repeat(b_spatial[l][:, 0], gmw // n_groups).reshape(1, gmw)
        o_gmlp, vn = _gmlp_sample(ugv, gm_ln_g[l], gm_ln_b[l], w00, b0)
        return o3.reshape(db, aw), o_gmlp, (new_k, new_v, vn)

    xs2 = x_sample.reshape(db, d)
    x1_s, _, (new_k_s, new_v_s, vn_s) = _token_mixing(xs2, mod_s, 1, params, attend_sample)

    w_router_t = w_router[l].T
    h2_p, hp_p, lg_p = _ffn_norm(x1_p, norm_ffn_g[l], mod_p, 3, 4, seq, w_router_t)
    h2_s, hp_s, lg_s = _ffn_norm(x1_s, norm_ffn_g[l], mod_s, 3, 4, 1, w_router_t)
    h2 = jnp.concatenate([h2_p, h2_s], axis=0)
    hp = jnp.concatenate([hp_p, hp_s], axis=0)
    logits_t = jnp.concatenate([lg_p, lg_s], axis=1)
    n_tok = h2.shape[0]

    eidx8, ew8, rank8, counts = _route(logits_t, router_bias[l])
    eidx, ew, rank = eidx8[:TOP_K], ew8[:TOP_K], rank8[:TOP_K]
    counts = counts[:, 0]

    tm_e = 256
    n_assign = n_tok * TOP_K
    n_tiles = -(-n_assign // tm_e) + n_exp
    n_rows = n_tiles * tm_e
    padded = (counts + tm_e - 1) // tm_e * tm_e
    pad_end = jnp.cumsum(padded)
    pad_start = pad_end - padded
    expert_ids = jnp.arange(n_exp, dtype=jnp.int32)
    dest = rank + jnp.sum(jnp.where(eidx[:, :, None] == expert_ids, pad_start.astype(jnp.int32), 0), axis=-1)
    dest_flat = dest.reshape(-1)
    tile_ids = jnp.arange(n_tiles, dtype=jnp.int32)
    tile_e = jnp.minimum(jnp.sum((pad_end[None, :] <= tile_ids[:, None] * tm_e).astype(jnp.int32), axis=1),
                         n_exp - 1).astype(jnp.int32)
    used_tiles = (pad_end[-1:] // tm_e).astype(jnp.int32)
    pad_fill = jnp.maximum(pad_end // tm_e - 1, 0).astype(jnp.int32)
    tail_fill = jnp.minimum(used_tiles + tile_ids, n_tiles - 1)
    fill_tiles = jnp.concatenate([pad_fill, tail_fill])
    n_fill = (n_exp + n_tiles - used_tiles).astype(jnp.int32)

    seg_end_tile = (pad_end // tm_e).astype(jnp.int32)[tile_e]
    next_e = tile_e[jnp.minimum(seg_end_tile, n_tiles - 1)]
    rows_in_tile = counts[tile_e] - (tile_ids - (pad_start // tm_e).astype(jnp.int32)[tile_e]) * tm_e
    tile_flags = jnp.where(seg_end_tile < used_tiles, TILE_HAS_NEXT, 0) \
        + jnp.where(rows_in_tile <= tm_e // 2, TILE_HALF_FULL, 0)
    tile_flags = tile_flags.astype(jnp.int32)

    xs = _dispatch(dest_flat, fill_tiles, n_fill, hp, n_rows, tm_e)
    act = _expert_up(tile_e, used_tiles, next_e, tile_flags, xs, w_gate[l], w_up[l], tm_e,
                     _pick_tile(de, 512, LANES))
    ys = _expert_down(tile_e, used_tiles, next_e, tile_flags, act, w_down[l], tm_e,
                      _pick_tile(d // 2, 2048, LANES))

    tm_s = _pick_tile(n_tok, 640, 16)
    tn_s = _pick_tile(ds, 256, LANES)
    sh_act = _mm([(h2, d, 0)], [(ws_gate[l], 0), (ws_up[l], 0)], [],
                 lambda accs, extras, n, mi: [_silu(accs[0]) * accs[1]],
                 [(jax.ShapeDtypeStruct((n_tok, ds), BF16), pl.BlockSpec((tm_s, tn_s), lambda n, mi: (mi, n)))],
                 m=n_tok, tm=tm_s, tn=tn_s, n_tiles=ds // tn_s, pairs=[(0, 0), (0, 1)], name="shared_up")[0]
    shared = _mm([(sh_act, ds, 0)], [(ws_down[l], 0)], [],
                 lambda accs, extras, n, mi: [accs[0]],
                 [(jax.ShapeDtypeStruct((n_tok, d), F32), pl.BlockSpec((tm_s, tn), lambda n, mi: (mi, n)))],
                 m=n_tok, tm=tm_s, tn=tn, n_tiles=d // tn, pairs=[(0, 0)], name="shared_down")[0]

    ew_t = ew8.T
    y_p = _combine(dest_flat, ys, x1_p, shared, mod_p, ew_t, 5, seq, n_tok, 0)
    y_s = _combine(dest_flat, ys, x1_s, shared, mod_s, ew_t, 5, 1, n_tok, batch * seq)

    w_keep = min(WINDOW, seq)
    qkv_p3 = qkv_p.reshape(batch, seq, aw + 2 * kvw)
    new_k_p = qkv_p3[:, seq - w_keep:, aw:aw + kvw].reshape(1, batch, w_keep, n_kv, hd)
    new_v_p = qkv_p3[:, seq - w_keep:, aw + kvw:].reshape(1, batch, w_keep, n_kv, hd)
    return (y_p.reshape(batch, seq, d), y_s.reshape(db, 1, d), new_k_p, new_v_p,
            new_k_s.reshape(1, db, win, n_kv, hd), new_v_s.reshape(1, db, win, n_kv, hd),
            vn_s.reshape(1, db, 1, gmw))
```
